```python
import jax, jax.numpy as jnp
from jax import lax
import numpy as np

D_MODEL = 1024
BATCH = 32
SEQ = 256
DEPTH = 4
DEC_BATCH = 8
DEC_SEQ = 2048
PAST_LEN = 256

GRID_W = 64
N_HEADS = 8
QK_NOPE = 64
QK_ROPE = 32
V_HEAD = 64
Q_LORA = 384
KV_LORA = 256
C_CONV = 512
CONV_K = 31
CONV_PAD = CONV_K // 2
D_IN = Q_LORA + KV_LORA + QK_ROPE + 2 * C_CONV
D_MIX = N_HEADS * V_HEAD + C_CONV
N_EXPERTS = 16
N_GROUPS = 4
EXPERTS_PER_GROUP = N_EXPERTS // N_GROUPS
TOP_K = 2
D_EXPERT = 256
ROPE_BASE = 10000.0
ROPE_PAIRS = QK_ROPE // 4
Q_BLOCK = 128
EPS = 1e-6
ATTN_SCALE = (QK_NOPE + QK_ROPE) ** -0.5

kernel_name = "hybrid_mla_conformer_moe_diffusion_step"


def rmsnorm(x, g):
    xf = x.astype(jnp.float32)
    y = xf * lax.rsqrt(jnp.mean(xf * xf, axis=-1, keepdims=True) + EPS) * g
    return y.astype(x.dtype)


def layernorm(x, g, b):
    xf = x.astype(jnp.float32)
    mu = jnp.mean(xf, axis=-1, keepdims=True)
    var = jnp.mean(jnp.square(xf - mu), axis=-1, keepdims=True)
    return ((xf - mu) * lax.rsqrt(var + EPS) * g + b).astype(x.dtype)


def ada_params(cvec, w_ada, b_ada):
    m = jax.nn.silu(cvec) @ w_ada + b_ada
    return jnp.split(m[:, None, :], 6, axis=-1)


def axial_rope_tables(n_tokens):
    rows = n_tokens // GRID_W
    row = jnp.repeat(jnp.arange(rows), GRID_W).astype(jnp.float32)
    col = jnp.tile(jnp.arange(GRID_W), rows).astype(jnp.float32)
    freqs = ROPE_BASE ** (-jnp.arange(ROPE_PAIRS, dtype=jnp.float32) / ROPE_PAIRS)
    ang = jnp.stack([row[:, None] * freqs, col[:, None] * freqs], axis=1)
    return jnp.cos(ang), jnp.sin(ang)


def apply_axial_rope(x, cos, sin):
    xr = x.reshape(x.shape[:-1] + (2, 2, ROPE_PAIRS))
    x1, x2 = xr[..., 0, :], xr[..., 1, :]
    cos = cos.astype(x.dtype)
    sin = sin.astype(x.dtype)
    out = jnp.stack([x1 * cos - x2 * sin, x2 * cos + x1 * sin], axis=-2)
    return out.reshape(x.shape)


def expand_kv(ckv, w_ukv):
    b, s, _ = ckv.shape
    kv = (ckv @ w_ukv).reshape(b, s, N_HEADS, QK_NOPE + V_HEAD)
    return kv[..., :QK_NOPE], kv[..., QK_NOPE:]


def mla_attend(q_nope, q_rope, k_nope, k_rope, v):
    b, t, h, _ = q_nope.shape
    nb = t // Q_BLOCK

    def blocks(a):
        return jnp.moveaxis(a.reshape((b, nb, Q_BLOCK) + a.shape[2:]), 1, 0)

    def attend_block(qs):
        qn, qr = qs
        s = jnp.einsum("bqhd,bkhd->bhqk", qn, k_nope) + jnp.einsum("bqhr,bkr->bhqk", qr, k_rope)
        p = jax.nn.softmax(s.astype(jnp.float32) * ATTN_SCALE, axis=-1).astype(v.dtype)
        return jnp.einsum("bhqk,bkhd->bqhd", p, v)

    o = lax.map(attend_block, (blocks(q_nope), blocks(q_rope)))
    return jnp.moveaxis(o, 0, 1).reshape(b, t, h * V_HEAD)


def conv_module(u, conv_w, conv_b, ln_g, ln_b):
    a, gate = jnp.split(u, 2, axis=-1)
    z = a * jax.nn.sigmoid(gate)
    z = lax.conv_general_dilated(z, conv_w[:, None, :], (1,), [(CONV_PAD, CONV_PAD)],
                                 dimension_numbers=("NWC", "WIO", "NWC"),
                                 feature_group_count=C_CONV) + conv_b
    return jax.nn.silu(layernorm(z, ln_g, ln_b))


def grouped_moe(h, router_w, router_b, w_gate, w_up, w_down):
    b, t, d = h.shape
    xt = h.reshape(b * t, d)
    aff = jax.nn.sigmoid((xt @ router_w).astype(jnp.float32))
    sel = aff + router_b.astype(jnp.float32)
    grp_score = lax.top_k(sel.reshape(-1, N_GROUPS, EXPERTS_PER_GROUP), TOP_K)[0].sum(-1)
    g_idx = jnp.argmax(grp_score, axis=-1)
    in_group = (jnp.arange(N_EXPERTS) // EXPERTS_PER_GROUP)[None, :] == g_idx[:, None]
    _, top_idx = lax.top_k(jnp.where(in_group, sel, -jnp.inf), TOP_K)
    top_aff = jnp.take_along_axis(aff, top_idx, axis=-1)
    wts = top_aff / jnp.sum(top_aff, axis=-1, keepdims=True)
    gates = jnp.sum(jax.nn.one_hot(top_idx, N_EXPERTS, dtype=jnp.float32) * wts[..., None], axis=1).astype(h.dtype)
    hid = jax.nn.silu(jnp.einsum("nd,edf->nef", xt, w_gate)) * jnp.einsum("nd,edf->nef", xt, w_up)
    out = jnp.einsum("nef,efd->nd", hid * gates[..., None], w_down)
    return out.reshape(b, t, d)


def trunk_layer(x, cvec, p, router_w, router_b, rope=None, ctx_ckv=None, ctx_krope=None):
    b, t, _ = x.shape
    sh1, sc1, g1, sh2, sc2, g2 = ada_params(cvec, p["w_ada"], p["b_ada"])
    h = rmsnorm(x, p["norm1_g"]) * (1 + sc1) + sh1
    proj = h @ p["w_in"]
    q_c, kv_c, k_rope, conv_in = jnp.split(proj, [Q_LORA, Q_LORA + KV_LORA, Q_LORA + KV_LORA + QK_ROPE], axis=-1)
    q = (rmsnorm(q_c, p["q_norm_g"]) @ p["w_uq"]).reshape(b, t, N_HEADS, QK_NOPE + QK_ROPE)
    q_nope, q_rope = q[..., :QK_NOPE], q[..., QK_NOPE:]
    ckv = rmsnorm(kv_c, p["kv_norm_g"])
    k_nope, v = expand_kv(ckv, p["w_ukv"])
    if ctx_ckv is None:
        attn = mla_attend(q_nope, q_rope, k_nope, k_rope, v)
    else:
        cos, sin = rope
        q_rope = apply_axial_rope(q_rope, cos[:, None], sin[:, None])
        k_rope_lat = apply_axial_rope(k_rope, cos, sin)
        ck_nope, cv = expand_kv(ctx_ckv, p["w_ukv"])
        attn = mla_attend(q_nope, q_rope,
                          jnp.concatenate([k_nope, ck_nope], axis=1),
                          jnp.concatenate([k_rope_lat, ctx_krope], axis=1),
                          jnp.concatenate([v, cv], axis=1))
    conv = conv_module(conv_in, p["conv_w"], p["conv_b"], p["conv_ln_g"], p["conv_ln_b"])
    x = x + g1 * (jnp.concatenate([attn, conv], axis=-1) @ p["w_out"])
    h2 = rmsnorm(x, p["norm2_g"]) * (1 + sc2) + sh2
    x = x + g2 * grouped_moe(h2, router_w, router_b, p["w_gate"], p["w_up"], p["w_down"])
    return x, ckv, k_rope


def setup_inputs(seed: int = 0) -> dict:
    key = jax.random.key(seed)
    ks = jax.random.split(key, 28)

    def nrm(k, shape, s):
        return jax.random.normal(k, shape, jnp.float32) * s

    return {
        "x_prompt": nrm(ks[0], (BATCH, SEQ, D_MODEL), 1.0),
        "x_sample": nrm(ks[1], (DEC_BATCH, DEC_SEQ, D_MODEL), 1.0),
        "cache_ckv": nrm(ks[2], (DEC_BATCH, DEPTH, PAST_LEN, KV_LORA), 1.0),
        "cache_krope": nrm(ks[3], (DEC_BATCH, DEPTH, PAST_LEN, QK_ROPE), 1.0),
        "c": nrm(ks[4], (DEC_BATCH, D_MODEL), 1.0),
        "c_ctx": nrm(ks[5], (D_MODEL,), 1.0),
        "w_ada": nrm(ks[6], (DEPTH, D_MODEL, 6 * D_MODEL), 0.5 * D_MODEL ** -0.5),
        "b_ada": nrm(ks[7], (DEPTH, 6 * D_MODEL), 0.01),
        "norm1_g": 1.0 + nrm(ks[8], (DEPTH, D_MODEL), 0.05),
        "w_in": nrm(ks[9], (DEPTH, D_MODEL, D_IN), D_MODEL ** -0.5),
        "q_norm_g": 1.0 + nrm(ks[10], (DEPTH, Q_LORA), 0.05),
        "w_uq": nrm(ks[11], (DEPTH, Q_LORA, N_HEADS * (QK_NOPE + QK_ROPE)), Q_LORA ** -0.5),
        "kv_norm_g": 1.0 + nrm(ks[12], (DEPTH, KV_LORA), 0.05),
        "w_ukv": nrm(ks[13], (DEPTH, KV_LORA, N_HEADS * (QK_NOPE + V_HEAD)), KV_LORA ** -0.5),
        "conv_w": nrm(ks[14], (DEPTH, CONV_K, C_CONV), CONV_K ** -0.5),
        "conv_b": nrm(ks[15], (DEPTH, C_CONV), 0.01),
        "conv_ln_g": 1.0 + nrm(ks[16], (DEPTH, C_CONV), 0.05),
        "conv_ln_b": nrm(ks[17], (DEPTH, C_CONV), 0.01),
        "w_out": nrm(ks[18], (DEPTH, D_MIX, D_MODEL), D_MIX ** -0.5),
        "norm2_g": 1.0 + nrm(ks[19], (DEPTH, D_MODEL), 0.05),
        "router_w": nrm(ks[20], (D_MODEL, N_EXPERTS), D_MODEL ** -0.5),
        "router_b": nrm(ks[21], (N_EXPERTS,), 0.01),
        "w_gate": nrm(ks[22], (DEPTH, N_EXPERTS, D_MODEL, D_EXPERT), D_MODEL ** -0.5),
        "w_up": nrm(ks[23], (DEPTH, N_EXPERTS, D_MODEL, D_EXPERT), D_MODEL ** -0.5),
        "w_down": nrm(ks[24], (DEPTH, N_EXPERTS, D_EXPERT, D_MODEL), D_EXPERT ** -0.5),
        "final_g": 1.0 + nrm(ks[25], (D_MODEL,), 0.05),
    }


def reference(x_prompt, x_sample, cache_ckv, cache_krope, c, c_ctx, w_ada, b_ada, norm1_g, w_in,
              q_norm_g, w_uq, kv_norm_g, w_ukv, conv_w, conv_b, conv_ln_g, conv_ln_b, w_out,
              norm2_g, router_w, router_b, w_gate, w_up, w_down, final_g):
    def layer(l):
        return {"w_ada": w_ada[l], "b_ada": b_ada[l], "norm1_g": norm1_g[l], "w_in": w_in[l],
                "q_norm_g": q_norm_g[l], "w_uq": w_uq[l], "kv_norm_g": kv_norm_g[l], "w_ukv": w_ukv[l],
                "conv_w": conv_w[l], "conv_b": conv_b[l], "conv_ln_g": conv_ln_g[l], "conv_ln_b": conv_ln_b[l],
                "w_out": w_out[l], "norm2_g": norm2_g[l], "w_gate": w_gate[l], "w_up": w_up[l],
                "w_down": w_down[l]}

    xp = x_prompt
    cctx = c_ctx[None, :]
    ckv_list, krope_list = [], []
    for l in range(DEPTH):
        xp, ckv, krope = trunk_layer(xp, cctx, layer(l), router_w, router_b)
        ckv_list.append(ckv)
        krope_list.append(krope)
    y_prompt = rmsnorm(xp, final_g)
    new_ckv = jnp.stack(ckv_list, axis=1)
    new_krope = jnp.stack(krope_list, axis=1)

    rope = axial_rope_tables(x_sample.shape[1])
    xs = x_sample
    for l in range(DEPTH):
        xs, _, _ = trunk_layer(xs, c, layer(l), router_w, router_b, rope, cache_ckv[:, l], cache_krope[:, l])
    y_sample = rmsnorm(xs, final_g)
    return (y_prompt, y_sample, new_ckv, new_krope)
```

```python
import functools

import jax
import jax.numpy as jnp
import numpy as np
from jax import lax
from jax.experimental import pallas as pl
from jax.experimental.pallas import tpu as pltpu

D_MODEL = 1024
DEPTH = 4
GRID_W = 64
N_HEADS = 8
QK_NOPE = 64
QK_ROPE = 32
V_HEAD = 64
Q_LORA = 384
KV_LORA = 256
C_CONV = 512
CONV_K = 31
CONV_PAD = CONV_K // 2
N_EXPERTS = 16
N_GROUPS = 4
EXPERTS_PER_GROUP = N_EXPERTS // N_GROUPS
D_EXPERT = 256
ROPE_BASE = 10000.0
ROPE_PAIRS = QK_ROPE // 4
EPS = 1e-6
ATTN_SCALE = (QK_NOPE + QK_ROPE) ** -0.5

LANES = 128
HEAD_BLOCK = LANES
N_MOD = 6
MOD_ROWS = 16
HALO = 16
VMEM_LIMIT = 48 * 1024 * 1024

BF16 = jnp.bfloat16
F32 = jnp.float32


def _rms(x, g):
    return x * lax.rsqrt(jnp.mean(x * x, axis=-1, keepdims=True) + EPS) * g


def _silu(x):
    return x * jax.nn.sigmoid(x)


def _params(*sem):
    return pltpu.CompilerParams(dimension_semantics=sem, vmem_limit_bytes=VMEM_LIMIT)


def _ada_kernel(c_ref, w_ref, b_ref, o_ref):
    s = _silu(c_ref[...]).astype(BF16)
    o_ref[...] = jnp.dot(s, w_ref[...].astype(BF16), preferred_element_type=F32) + b_ref[...]


def _ada_call(cvecs, w_ada, b_ada):
    return pl.pallas_call(
        _ada_kernel,
        grid=(DEPTH, N_MOD),
        in_specs=[
            pl.BlockSpec((MOD_ROWS, D_MODEL), lambda l, n: (0, 0)),
            pl.BlockSpec((None, D_MODEL, D_MODEL), lambda l, n: (l, 0, n)),
            pl.BlockSpec((None, 1, D_MODEL), lambda l, n: (l, 0, n)),
        ],
        out_specs=pl.BlockSpec((None, MOD_ROWS, D_MODEL), lambda l, n: (l, 0, n)),
        out_shape=jax.ShapeDtypeStruct((DEPTH, MOD_ROWS, N_MOD * D_MODEL), F32),
        compiler_params=_params("parallel", "parallel"),
        name="ada",
    )(cvecs, w_ada, b_ada.reshape(DEPTH, 1, N_MOD * D_MODEL))


W_IN_Q = 0
W_IN_KV = W_IN_Q + Q_LORA
W_IN_A = W_IN_KV + KV_LORA
W_IN_G = W_IN_A + C_CONV
W_IN_KR = W_IN_G + C_CONV
W_IN_KRS = W_IN_KR + HEAD_BLOCK
W_IN_COLS = W_IN_KRS + HEAD_BLOCK


def _premix_kernel(rope, x_ref, mod_ref, n1g_ref, win_ref, qg_ref, wuq_ref, kvg_ref, wuk_ref,
                   wuv_ref, *rest):
    if rope:
        wuqs_ref, cos_ref, sin_ref, q_ref, k_ref, v_ref, z_ref = rest
    else:
        q_ref, k_ref, v_ref, z_ref, ckv_ref, kr_ref = rest
    mod = mod_ref[0]
    sh1, sc1 = mod[0:1], mod[1:2]
    h = (_rms(x_ref[0], n1g_ref[...]) * (1.0 + sc1) + sh1).astype(BF16)
    proj = jnp.dot(h, win_ref[...], preferred_element_type=F32)
    q_c = proj[:, W_IN_Q:W_IN_KV]
    kv_c = proj[:, W_IN_KV:W_IN_A]
    a = proj[:, W_IN_A:W_IN_G]
    gate = proj[:, W_IN_G:W_IN_KR]
    kr = proj[:, W_IN_KR:W_IN_KRS]

    z_ref[0] = a * jax.nn.sigmoid(gate)

    qn = _rms(q_c, qg_ref[...]).astype(BF16)
    q = jnp.dot(qn, wuq_ref[...], preferred_element_type=F32)
    ckv = _rms(kv_c, kvg_ref[...])
    ckv_b = ckv.astype(BF16)
    kn = jnp.dot(ckv_b, wuk_ref[...], preferred_element_type=F32)
    v_ref[0] = jnp.dot(ckv_b, wuv_ref[...], preferred_element_type=F32).astype(BF16)

    if rope:
        cos = cos_ref[...]
        sin = sin_ref[...]
        qs = jnp.dot(qn, wuqs_ref[...], preferred_element_type=F32)
        kr = kr * cos + proj[:, W_IN_KRS:W_IN_COLS] * sin
    else:
        ckv_ref[0] = ckv
        kr_ref[0] = kr[:, :QK_ROPE]

    for hd in range(N_HEADS):
        sl = slice(hd * HEAD_BLOCK, (hd + 1) * HEAD_BLOCK)
        qh = q[:, sl]
        if rope:
            qh = qh * cos + qs[:, sl] * sin
        q_ref[0, :, sl] = (qh * ATTN_SCALE).astype(BF16)
        k_ref[0, :, sl] = (kn[:, sl] + kr).astype(BF16)


def _premix_call(x, mod, lw, tm, rope_tabs=None):
    b, t, _ = x.shape
    rope = rope_tabs is not None
    nt = t // tm
    per_b = (lambda i, j: (i, 0, 0)) if mod.shape[0] > 1 else (lambda i, j: (0, 0, 0))
    const2 = lambda i, j: (0, 0)
    tok = lambda i, j: (i, j, 0)
    n_in = W_IN_COLS if rope else W_IN_KRS
    in_specs = [
        pl.BlockSpec((1, tm, D_MODEL), tok),
        pl.BlockSpec((1, N_MOD, D_MODEL), per_b),
        pl.BlockSpec((1, D_MODEL), const2),
        pl.BlockSpec((D_MODEL, n_in), const2),
        pl.BlockSpec((1, Q_LORA), const2),
        pl.BlockSpec((Q_LORA, N_HEADS * HEAD_BLOCK), const2),
        pl.BlockSpec((1, KV_LORA), const2),
        pl.BlockSpec((KV_LORA, N_HEADS * HEAD_BLOCK), const2),
        pl.BlockSpec((KV_LORA, N_HEADS * V_HEAD), const2),
    ]
    args = [x, mod, lw["n1g"], lw["win"], lw["qg"], lw["wuq"], lw["kvg"], lw["wuk"], lw["wuv"]]
    out_specs = [
        pl.BlockSpec((1, tm, N_HEADS * HEAD_BLOCK), tok),
        pl.BlockSpec((1, tm, N_HEADS * HEAD_BLOCK), tok),
        pl.BlockSpec((1, tm, N_HEADS * V_HEAD), tok),
        pl.BlockSpec((1, tm, C_CONV), tok),
    ]
    out_shape = [
        jax.ShapeDtypeStruct((b, t, N_HEADS * HEAD_BLOCK), BF16),
        jax.ShapeDtypeStruct((b, t, N_HEADS * HEAD_BLOCK), BF16),
        jax.ShapeDtypeStruct((b, t, N_HEADS * V_HEAD), BF16),
        jax.ShapeDtypeStruct((b, t, C_CONV), F32),
    ]
    if rope:
        in_specs += [
            pl.BlockSpec((Q_LORA, N_HEADS * HEAD_BLOCK), const2),
            pl.BlockSpec((tm, HEAD_BLOCK), lambda i, j: (j, 0)),
            pl.BlockSpec((tm, HEAD_BLOCK), lambda i, j: (j, 0)),
        ]
        args += [lw["wuqs"], rope_tabs[0], rope_tabs[1]]
    else:
        out_specs += [
            pl.BlockSpec((1, tm, KV_LORA), tok),
            pl.BlockSpec((1, tm, QK_ROPE), tok),
        ]
        out_shape += [
            jax.ShapeDtypeStruct((b, t, KV_LORA), F32),
            jax.ShapeDtypeStruct((b, t, QK_ROPE), F32),
        ]
    return pl.pallas_call(
        functools.partial(_premix_kernel, rope),
        grid=(b, nt),
        in_specs=in_specs,
        out_specs=out_specs,
        out_shape=out_shape,
        compiler_params=_params("parallel", "parallel"),
        name="premix_rope" if rope else "premix",
    )(*args)


def _ctx_kernel(ckv_ref, kr_ref, wuk_ref, wuv_ref, k_ref, v_ref):
    ckv_b = ckv_ref[...].astype(BF16)
    kn = jnp.dot(ckv_b, wuk_ref[...], preferred_element_type=F32)
    v_ref[0] = jnp.dot(ckv_b, wuv_ref[...], preferred_element_type=F32).astype(BF16)
    kr = kr_ref[...]
    for hd in range(N_HEADS):
        sl = slice(hd * HEAD_BLOCK, (hd + 1) * HEAD_BLOCK)
        k_ref[0, :, sl] = (kn[:, sl] + kr).astype(BF16)


def _ctx_call(cache_ckv, cache_kr_pad, layer, lw):
    b, _, s, _ = cache_ckv.shape
    const2 = lambda i: (0, 0)
    return pl.pallas_call(
        _ctx_kernel,
        grid=(b,),
        in_specs=[
            pl.BlockSpec((None, None, s, KV_LORA), lambda i: (i, layer, 0, 0)),
            pl.BlockSpec((None, None, s, HEAD_BLOCK), lambda i: (i, layer, 0, 0)),
            pl.BlockSpec((KV_LORA, N_HEADS * HEAD_BLOCK), const2),
            pl.BlockSpec((KV_LORA, N_HEADS * V_HEAD), const2),
        ],
        out_specs=[
            pl.BlockSpec((1, s, N_HEADS * HEAD_BLOCK), lambda i: (i, 0, 0)),
            pl.BlockSpec((1, s, N_HEADS * V_HEAD), lambda i: (i, 0, 0)),
        ],
        out_shape=[
            jax.ShapeDtypeStruct((b, s, N_HEADS * HEAD_BLOCK), BF16),
            jax.ShapeDtypeStruct((b, s, N_HEADS * V_HEAD), BF16),
        ],
        compiler_params=_params("parallel"),
        name="ctx_kv",
    )(cache_ckv, cache_kr_pad, lw["wuk"], lw["wuv"])


_NT = (((1,), (1,)), ((), ()))


def _attn_kernel(ctx, q_ref, k_ref, v_ref, *rest):
    if ctx:
        kc_ref, vc_ref, o_ref = rest
    else:
        (o_ref,) = rest
    outs = []
    for hd in range(N_HEADS):
        sl = slice(hd * HEAD_BLOCK, (hd + 1) * HEAD_BLOCK)
        vs = slice(hd * V_HEAD, (hd + 1) * V_HEAD)
        qh = q_ref[0, :, sl]
        s = lax.dot_general(qh, k_ref[0, :, sl], _NT, preferred_element_type=F32)
        m = jnp.max(s, axis=-1, keepdims=True)
        if ctx:
            sc = lax.dot_general(qh, kc_ref[0, :, sl], _NT, preferred_element_type=F32)
            m = jnp.maximum(m, jnp.max(sc, axis=-1, keepdims=True))
        p = jnp.exp(s - m)
        den = jnp.sum(p, axis=-1, keepdims=True)
        o = jnp.dot(p.astype(BF16), v_ref[0, :, vs], preferred_element_type=F32)
        if ctx:
            pc = jnp.exp(sc - m)
            den = den + jnp.sum(pc, axis=-1, keepdims=True)
            o = o + jnp.dot(pc.astype(BF16), vc_ref[0, :, vs], preferred_element_type=F32)
        outs.append(o / den)
    o_ref[0] = jnp.concatenate(outs, axis=-1).astype(BF16)


def _attn_call(q, k, v, tq, ctx_kv=None):
    b, t, _ = q.shape
    ctx = ctx_kv is not None
    tok = lambda i, j: (i, j, 0)
    seq = lambda i, j: (i, 0, 0)
    in_specs = [
        pl.BlockSpec((1, tq, N_HEADS * HEAD_BLOCK), tok),
        pl.BlockSpec((1, t, N_HEADS * HEAD_BLOCK), seq),
        pl.BlockSpec((1, t, N_HEADS * V_HEAD), seq),
    ]
    args = [q, k, v]
    if ctx:
        s = ctx_kv[0].shape[1]
        in_specs += [
            pl.BlockSpec((1, s, N_HEADS * HEAD_BLOCK), seq),
            pl.BlockSpec((1, s, N_HEADS * V_HEAD), seq),
        ]
        args += list(ctx_kv)
    return pl.pallas_call(
        functools.partial(_attn_kernel, ctx),
        grid=(b, t // tq),
        in_specs=in_specs,
        out_specs=pl.BlockSpec((1, tq, N_HEADS * V_HEAD), tok),
        out_shape=jax.ShapeDtypeStruct((b, t, N_HEADS * V_HEAD), BF16),
        compiler_params=_params("parallel", "arbitrary"),
        name="attn_ctx" if ctx else "attn",
    )(*args)


CONV_ROWS = 64


def _route(aff, bias):
    sel = aff + bias
    rows = [sel[e:e + 1] for e in range(N_EXPERTS)]
    affr = [aff[e:e + 1] for e in range(N_EXPERTS)]
    scores = []
    for g in range(N_GROUPS):
        a, b, c, d = rows[g * EXPERTS_PER_GROUP:(g + 1) * EXPERTS_PER_GROUP]
        hi1, lo1 = jnp.maximum(a, b), jnp.minimum(a, b)
        hi2, lo2 = jnp.maximum(c, d), jnp.minimum(c, d)
        top1 = jnp.maximum(hi1, hi2)
        top2 = jnp.maximum(jnp.minimum(hi1, hi2), jnp.maximum(lo1, lo2))
        scores.append(top1 + top2)
    best = scores[0]
    gidx = jnp.zeros_like(best, dtype=jnp.int32)
    for g in range(1, N_GROUPS):
        better = scores[g] > best
        best = jnp.where(better, scores[g], best)
        gidx = jnp.where(better, g, gidx)
    gates = []
    for g in range(N_GROUPS):
        in_g = gidx == g
        s = rows[g * EXPERTS_PER_GROUP:(g + 1) * EXPERTS_PER_GROUP]
        af = affr[g * EXPERTS_PER_GROUP:(g + 1) * EXPERTS_PER_GROUP]
        picked = []
        for i in range(EXPERTS_PER_GROUP):
            rank = jnp.zeros_like(gidx)
            for j in range(EXPERTS_PER_GROUP):
                if j == i:
                    continue
                ahead = (s[j] >= s[i]) if j < i else (s[j] > s[i])
                rank = rank + ahead.astype(jnp.int32)
            picked.append(jnp.where((rank < 2) & in_g, af[i], 0.0))
        den = (picked[0] + picked[1]) + (picked[2] + picked[3])
        den = jnp.where(in_g, den, 1.0)
        gates += [pk / den for pk in picked]
    return jnp.concatenate(gates, axis=0)


def _postmix_kernel(nt, x_ref, attn_ref, z_ref, mod_ref, cw_ref, cb_ref, lg_ref, lb_ref, wout_ref,
                    n2g_ref, rwt_ref, rb_ref, x1_ref, h2_ref, gt_ref, zp_ref, cz_ref):
    tm = x_ref.shape[1]
    j = pl.program_id(1)
    t0 = pl.multiple_of(j * tm, tm)

    zp_ref[HALO:HALO + tm, :] = z_ref[0, pl.ds(t0, tm), :]
    zeros = jnp.zeros((HALO, C_CONV), F32)

    @pl.when(j == 0)
    def _():
        zp_ref[0:HALO, :] = zeros

    @pl.when(j > 0)
    def _():
        zp_ref[0:HALO, :] = z_ref[0, pl.ds(t0 - HALO, HALO), :]

    @pl.when(j == nt - 1)
    def _():
        zp_ref[HALO + tm:, :] = zeros

    @pl.when(j < nt - 1)
    def _():
        zp_ref[HALO + tm:, :] = z_ref[0, pl.ds(t0 + tm, HALO), :]

    cw = cw_ref[...]
    cb = cb_ref[...]
    for r0 in range(0, tm, CONV_ROWS):
        acc = jnp.broadcast_to(cb, (CONV_ROWS, C_CONV))
        for kk in range(CONV_K):
            off = r0 + HALO - CONV_PAD + kk
            acc = acc + zp_ref[off:off + CONV_ROWS, :] * cw[kk:kk + 1, :]
        cz_ref[r0:r0 + CONV_ROWS, :] = acc

    cz = cz_ref[...]
    mu = jnp.mean(cz, axis=-1, keepdims=True)
    cen = cz - mu
    var = jnp.mean(cen * cen, axis=-1, keepdims=True)
    conv = _silu(cen * lax.rsqrt(var + EPS) * lg_ref[...] + lb_ref[...]).astype(BF16)

    n_attn = N_HEADS * V_HEAD
    y = jnp.dot(attn_ref[0], wout_ref[0:n_attn, :], preferred_element_type=F32)
    y = y + jnp.dot(conv, wout_ref[n_attn:, :], preferred_element_type=F32)

    mod = mod_ref[0]
    g1, sh2, sc2 = mod[2:3], mod[3:4], mod[4:5]
    x1 = x_ref[0] + g1 * y
    x1_ref[0] = x1
    h2 = _rms(x1, n2g_ref[...]) * (1.0 + sc2) + sh2
    h2_ref[0] = h2.astype(BF16)

    logits = lax.dot_general(rwt_ref[...], h2, _NT, precision=lax.Precision.HIGHEST,
                             preferred_element_type=F32)
    gt_ref[...] = _route(jax.nn.sigmoid(logits), rb_ref[...])


def _postmix_call(x, attn, z, mod, lw, rwt, rb, tm):
    b, t, _ = x.shape
    nt = t // tm
    per_b = (lambda i, j: (i, 0, 0)) if mod.shape[0] > 1 else (lambda i, j: (0, 0, 0))
    const2 = lambda i, j: (0, 0)
    tok = lambda i, j: (i, j, 0)
    return pl.pallas_call(
        functools.partial(_postmix_kernel, nt),
        grid=(b, nt),
        in_specs=[
            pl.BlockSpec((1, tm, D_MODEL), tok),
            pl.BlockSpec((1, tm, N_HEADS * V_HEAD), tok),
            pl.BlockSpec((1, t, C_CONV), lambda i, j: (i, 0, 0)),
            pl.BlockSpec((1, N_MOD, D_MODEL), per_b),
            pl.BlockSpec((CONV_K, C_CONV), const2),
            pl.BlockSpec((1, C_CONV), const2),
            pl.BlockSpec((1, C_CONV), const2),
            pl.BlockSpec((1, C_CONV), const2),
            pl.BlockSpec((D_MODEL, D_MODEL), const2),
            pl.BlockSpec((1, D_MODEL), const2),
            pl.BlockSpec((N_EXPERTS, D_MODEL), const2),
            pl.BlockSpec((N_EXPERTS, 1), const2),
        ],
        out_specs=[
            pl.BlockSpec((1, tm, D_MODEL), tok),
            pl.BlockSpec((1, tm, D_MODEL), tok),
            pl.BlockSpec((N_EXPERTS, tm), lambda i, j: (0, i * nt + j)),
        ],
        out_shape=[
            jax.ShapeDtypeStruct((b, t, D_MODEL), F32),
            jax.ShapeDtypeStruct((b, t, D_MODEL), BF16),
            jax.ShapeDtypeStruct((N_EXPERTS, b * t), F32),
        ],
        scratch_shapes=[
            pltpu.VMEM((tm + 2 * HALO, C_CONV), F32),
            pltpu.VMEM((tm, C_CONV), F32),
        ],
        compiler_params=_params("parallel", "arbitrary"),
        name="postmix",
    )(x, attn, z, mod, lw["conv_w"], lw["conv_b"], lw["ln_g"], lw["ln_b"], lw["wout"],
      lw["n2g"], rwt, rb)


def _moe_kernel(final, x1_ref, h2_ref, g_ref, mod_ref, wg_ref, wu_ref, wd_ref, fg_ref, o_ref,
                acc_ref):
    e = pl.program_id(1)

    @pl.when(e == 0)
    def _():
        acc_ref[...] = jnp.zeros_like(acc_ref)

    h2 = h2_ref[...]
    gates = g_ref[...]
    lane = lax.broadcasted_iota(jnp.int32, gates.shape, 1)
    gate = jnp.sum(jnp.where(lane == e, gates, 0.0), axis=-1, keepdims=True)
    hg = jnp.dot(h2, wg_ref[0], preferred_element_type=F32)
    hu = jnp.dot(h2, wu_ref[0], preferred_element_type=F32)
    hid = (_silu(hg) * hu * gate).astype(BF16)
    acc_ref[...] += jnp.dot(hid, wd_ref[0], preferred_element_type=F32)

    @pl.when(e == N_EXPERTS - 1)
    def _():
        g2 = mod_ref[0][5:6]
        x2 = x1_ref[...] + g2 * acc_ref[...]
        if final:
            x2 = _rms(x2, fg_ref[...])
        o_ref[...] = x2


def _moe_call(x1, h2, gates, mod, lw, final_g, tm, final):
    b, t, _ = x1.shape
    n = b * t
    tiles_per_b = max(t // tm, 1)
    if mod.shape[0] > 1:
        per_b = lambda i, e: (i // tiles_per_b, 0, 0)
    else:
        per_b = lambda i, e: (0, 0, 0)
    tok = lambda i, e: (i, 0)
    out = pl.pallas_call(
        functools.partial(_moe_kernel, final),
        grid=(n // tm, N_EXPERTS),
        in_specs=[
            pl.BlockSpec((tm, D_MODEL), tok),
            pl.BlockSpec((tm, D_MODEL), tok),
            pl.BlockSpec((tm, N_EXPERTS), tok),
            pl.BlockSpec((1, N_MOD, D_MODEL), per_b),
            pl.BlockSpec((1, D_MODEL, D_EXPERT), lambda i, e: (e, 0, 0)),
            pl.BlockSpec((1, D_MODEL, D_EXPERT), lambda i, e: (e, 0, 0)),
            pl.BlockSpec((1, D_EXPERT, D_MODEL), lambda i, e: (e, 0, 0)),
            pl.BlockSpec((1, D_MODEL), lambda i, e: (0, 0)),
        ],
        out_specs=pl.BlockSpec((tm, D_MODEL), tok),
        out_shape=jax.ShapeDtypeStruct((n, D_MODEL), F32),
        scratch_shapes=[pltpu.VMEM((tm, D_MODEL), F32)],
        compiler_params=_params("parallel", "arbitrary"),
        name="moe_final" if final else "moe",
    )(x1.reshape(n, D_MODEL), h2.reshape(n, D_MODEL), gates, mod, lw["wg"], lw["wu"], lw["wd"],
      final_g)
    return out.reshape(b, t, D_MODEL)


_SWAP = np.concatenate([np.arange(8, 16), np.arange(0, 8), np.arange(24, 32), np.arange(16, 24)])


def _prep_layer(l, norm1_g, w_in, q_norm_g, w_uq, kv_norm_g, w_ukv, conv_w, conv_b, conv_ln_g,
                conv_ln_b, w_out, norm2_g, w_gate, w_up, w_down):
    win = w_in[l]
    o_kv, o_kr, o_a, o_g = Q_LORA, Q_LORA + KV_LORA, Q_LORA + KV_LORA + QK_ROPE, Q_LORA + KV_LORA + QK_ROPE + C_CONV
    wkr = win[:, o_kr:o_a]
    pad_r = ((0, 0), (0, HEAD_BLOCK - QK_ROPE))
    win_cat = jnp.concatenate(
        [win[:, :o_kv], win[:, o_kv:o_kr], win[:, o_a:o_g], win[:, o_g:],
         jnp.pad(wkr, pad_r), jnp.pad(wkr[:, _SWAP], pad_r)], axis=1).astype(BF16)

    wuq = w_uq[l].reshape(Q_LORA, N_HEADS, QK_NOPE + QK_ROPE)
    nope, rope = wuq[..., :QK_NOPE], wuq[..., QK_NOPE:]
    zpad = jnp.zeros((Q_LORA, N_HEADS, HEAD_BLOCK - QK_NOPE - QK_ROPE), F32)
    wuq_p = jnp.concatenate([rope, nope, zpad], axis=-1).reshape(Q_LORA, -1).astype(BF16)
    zrest = jnp.zeros((Q_LORA, N_HEADS, HEAD_BLOCK - QK_ROPE), F32)
    wuq_s = jnp.concatenate([rope[..., _SWAP], zrest], axis=-1).reshape(Q_LORA, -1).astype(BF16)

    wukv = w_ukv[l].reshape(KV_LORA, N_HEADS, QK_NOPE + V_HEAD)
    kz0 = jnp.zeros((KV_LORA, N_HEADS, QK_ROPE), F32)
    kz1 = jnp.zeros((KV_LORA, N_HEADS, HEAD_BLOCK - QK_NOPE - QK_ROPE), F32)
    wuk = jnp.concatenate([kz0, wukv[..., :QK_NOPE], kz1], axis=-1).reshape(KV_LORA, -1).astype(BF16)
    wuv = wukv[..., QK_NOPE:].reshape(KV_LORA, -1).astype(BF16)

    return {
        "n1g": norm1_g[l][None], "win": win_cat, "qg": q_norm_g[l][None], "wuq": wuq_p,
        "wuqs": wuq_s, "kvg": kv_norm_g[l][None], "wuk": wuk, "wuv": wuv,
        "conv_w": conv_w[l], "conv_b": conv_b[l][None], "ln_g": conv_ln_g[l][None],
        "ln_b": conv_ln_b[l][None], "wout": w_out[l].astype(BF16), "n2g": norm2_g[l][None],
        "wg": w_gate[l].astype(BF16), "wu": w_up[l].astype(BF16), "wd": w_down[l].astype(BF16),
    }


def _rope_tables(n_tokens):
    rows = n_tokens // GRID_W
    row = jnp.repeat(jnp.arange(rows), GRID_W).astype(F32)
    col = jnp.tile(jnp.arange(GRID_W), rows).astype(F32)
    freqs = ROPE_BASE ** (-jnp.arange(ROPE_PAIRS, dtype=F32) / ROPE_PAIRS)
    ar, ac = row[:, None] * freqs, col[:, None] * freqs
    cr, sr, cc, sc = jnp.cos(ar), jnp.sin(ar), jnp.cos(ac), jnp.sin(ac)
    rest = HEAD_BLOCK - QK_ROPE
    cos = jnp.concatenate([cr, cr, cc, cc, jnp.ones((n_tokens, rest), F32)], axis=1)
    sin = jnp.concatenate([-sr, sr, -sc, sc, jnp.zeros((n_tokens, rest), F32)], axis=1)
    return cos, sin


TM_PROMPT = 256
TM_SAMPLE = 256
TQ_SAMPLE = 256
TM_MOE = 1024


def kernel(x_prompt, x_sample, cache_ckv, cache_krope, c, c_ctx, w_ada, b_ada, norm1_g, w_in,
           q_norm_g, w_uq, kv_norm_g, w_ukv, conv_w, conv_b, conv_ln_g, conv_ln_b, w_out,
           norm2_g, router_w, router_b, w_gate, w_up, w_down, final_g):
    dec_b = x_sample.shape[0]
    cvecs = jnp.concatenate(
        [c_ctx[None], c, jnp.zeros((MOD_ROWS - 1 - dec_b, D_MODEL), F32)], axis=0)
    mod_all = _ada_call(cvecs, w_ada, b_ada).reshape(DEPTH, MOD_ROWS, N_MOD, D_MODEL)

    rwt = router_w.T
    rb = router_b[:, None]
    fg = final_g[None]
    rope_tabs = _rope_tables(x_sample.shape[1])
    cache_kr_pad = jnp.pad(cache_krope, ((0, 0), (0, 0), (0, 0), (0, HEAD_BLOCK - QK_ROPE)))

    xp, xs = x_prompt, x_sample
    ckvs, krs = [], []
    for l in range(DEPTH):
        lw = _prep_layer(l, norm1_g, w_in, q_norm_g, w_uq, kv_norm_g, w_ukv, conv_w, conv_b,
                         conv_ln_g, conv_ln_b, w_out, norm2_g, w_gate, w_up, w_down)
        final = l == DEPTH - 1
        mod_p = mod_all[l, 0:1]
        mod_s = mod_all[l, 1:1 + dec_b]

        q, k, v, z, ckv, kr = _premix_call(xp, mod_p, lw, TM_PROMPT)
        ckvs.append(ckv)
        krs.append(kr)
        attn = _attn_call(q, k, v, xp.shape[1])
        x1, h2, gt = _postmix_call(xp, attn, z, mod_p, lw, rwt, rb, TM_PROMPT)
        xp = _moe_call(x1, h2, gt.T, mod_p, lw, fg, TM_MOE, final)

        q, k, v, z = _premix_call(xs, mod_s, lw, TM_SAMPLE, rope_tabs)
        ctx_kv = _ctx_call(cache_ckv, cache_kr_pad, l, lw)
        attn = _attn_call(q, k, v, TQ_SAMPLE, ctx_kv)
        x1, h2, gt = _postmix_call(xs, attn, z, mod_s, lw, rwt, rb, TM_SAMPLE)
        xs = _moe_call(x1, h2, gt.T, mod_s, lw, fg, TM_MOE, final)

    return xp, xs, jnp.stack(ckvs, axis=1), jnp.stack(krs, axis=1)
```

```python
import functools

import jax
import jax.numpy as jnp
import numpy as np
from jax import lax
from jax.experimental import pallas as pl
from jax.experimental.pallas import tpu as pltpu

D_MODEL = 1024
DEPTH = 4
GRID_W = 64
N_HEADS = 8
QK_NOPE = 64
QK_ROPE = 32
V_HEAD = 64
Q_LORA = 384
KV_LORA = 256
C_CONV = 512
CONV_K = 31
CONV_PAD = CONV_K // 2
N_EXPERTS = 16
N_GROUPS = 4
EXPERTS_PER_GROUP = N_EXPERTS // N_GROUPS
D_EXPERT = 256
ROPE_BASE = 10000.0
ROPE_PAIRS = QK_ROPE // 4
EPS = 1e-6
ATTN_SCALE = (QK_NOPE + QK_ROPE) ** -0.5
LOG2E = 1.4426950408889634

LANES = 128
SUBLANES = 8
HEAD_BLOCK = LANES
N_MOD = 6
MOD_ROWS = 16
HALO = 16
VMEM_LIMIT = 48 * 1024 * 1024

BF16 = jnp.bfloat16
F32 = jnp.float32


def _rms(x, g):
    return x * lax.rsqrt(jnp.mean(x * x, axis=-1, keepdims=True) + EPS) * g


def _silu(x):
    return x * jax.nn.sigmoid(x)


def _params(*sem):
    return pltpu.CompilerParams(dimension_semantics=sem, vmem_limit_bytes=VMEM_LIMIT)


def _ada_kernel(c_ref, w_ref, b_ref, o_ref):
    s = _silu(c_ref[...]).astype(BF16)
    o_ref[...] = jnp.dot(s, w_ref[...].astype(BF16), preferred_element_type=F32) + b_ref[...]


def _ada_call(cvecs, w_ada, b_ada):
    return pl.pallas_call(
        _ada_kernel,
        grid=(DEPTH, N_MOD),
        in_specs=[
            pl.BlockSpec((MOD_ROWS, D_MODEL), lambda l, n: (0, 0)),
            pl.BlockSpec((None, D_MODEL, D_MODEL), lambda l, n: (l, 0, n)),
            pl.BlockSpec((None, 1, D_MODEL), lambda l, n: (l, 0, n)),
        ],
        out_specs=pl.BlockSpec((None, MOD_ROWS, D_MODEL), lambda l, n: (l, 0, n)),
        out_shape=jax.ShapeDtypeStruct((DEPTH, MOD_ROWS, N_MOD * D_MODEL), F32),
        compiler_params=_params("parallel", "parallel"),
        name="ada",
    )(cvecs, w_ada, b_ada.reshape(DEPTH, 1, N_MOD * D_MODEL))


W_IN_Q = 0
W_IN_KV = W_IN_Q + Q_LORA
W_IN_A = W_IN_KV + KV_LORA
W_IN_G = W_IN_A + C_CONV
W_IN_KR = W_IN_G + C_CONV
W_IN_KRS = W_IN_KR + HEAD_BLOCK
W_IN_COLS = W_IN_KRS + HEAD_BLOCK


def _premix_kernel(rope, x_ref, mod_ref, n1g_ref, win_ref, qg_ref, wuq_ref, kvg_ref, wuk_ref,
                   wuv_ref, *rest):
    if rope:
        wuqs_ref, cos_ref, sin_ref, q_ref, k_ref, v_ref, z_ref = rest
    else:
        q_ref, k_ref, v_ref, z_ref, ckv_ref, kr_ref = rest
    mod = mod_ref[0]
    sh1, sc1 = mod[0:1], mod[1:2]
    h = (_rms(x_ref[0], n1g_ref[...]) * (1.0 + sc1) + sh1).astype(BF16)
    proj = jnp.dot(h, win_ref[...], preferred_element_type=F32)
    q_c = proj[:, W_IN_Q:W_IN_KV]
    kv_c = proj[:, W_IN_KV:W_IN_A]
    a = proj[:, W_IN_A:W_IN_G]
    gate = proj[:, W_IN_G:W_IN_KR]
    kr = proj[:, W_IN_KR:W_IN_KRS]

    z_ref[0] = a * jax.nn.sigmoid(gate)

    qn = _rms(q_c, qg_ref[...]).astype(BF16)
    q = jnp.dot(qn, wuq_ref[...], preferred_element_type=F32)
    ckv = _rms(kv_c, kvg_ref[...])
    ckv_b = ckv.astype(BF16)
    kn = jnp.dot(ckv_b, wuk_ref[...], preferred_element_type=F32)
    v_ref[0] = jnp.dot(ckv_b, wuv_ref[...], preferred_element_type=F32).astype(BF16)

    if rope:
        cos = cos_ref[...]
        sin = sin_ref[...]
        qs = jnp.dot(qn, wuqs_ref[...], preferred_element_type=F32)
        kr = kr * cos + proj[:, W_IN_KRS:W_IN_COLS] * sin
    else:
        ckv_ref[0] = ckv
        kr_ref[0] = kr[:, :QK_ROPE]

    for hd in range(N_HEADS):
        sl = slice(hd * HEAD_BLOCK, (hd + 1) * HEAD_BLOCK)
        qh = q[:, sl]
        if rope:
            qh = qh * cos + qs[:, sl] * sin
        q_ref[0, :, sl] = (qh * (ATTN_SCALE * LOG2E)).astype(BF16)
        k_ref[0, :, sl] = (kn[:, sl] + kr).astype(BF16)


def _premix_call(x, mod, lw, tm, rope_tabs=None):
    b, t, _ = x.shape
    rope = rope_tabs is not None
    nt = t // tm
    per_b = (lambda i, j: (i, 0, 0)) if mod.shape[0] > 1 else (lambda i, j: (0, 0, 0))
    const2 = lambda i, j: (0, 0)
    tok = lambda i, j: (i, j, 0)
    n_in = W_IN_COLS if rope else W_IN_KRS
    in_specs = [
        pl.BlockSpec((1, tm, D_MODEL), tok),
        pl.BlockSpec((1, N_MOD, D_MODEL), per_b),
        pl.BlockSpec((1, D_MODEL), const2),
        pl.BlockSpec((D_MODEL, n_in), const2),
        pl.BlockSpec((1, Q_LORA), const2),
        pl.BlockSpec((Q_LORA, N_HEADS * HEAD_BLOCK), const2),
        pl.BlockSpec((1, KV_LORA), const2),
        pl.BlockSpec((KV_LORA, N_HEADS * HEAD_BLOCK), const2),
        pl.BlockSpec((KV_LORA, N_HEADS * V_HEAD), const2),
    ]
    args = [x, mod, lw["n1g"], lw["win"], lw["qg"], lw["wuq"], lw["kvg"], lw["wuk"], lw["wuv"]]
    out_specs = [
        pl.BlockSpec((1, tm, N_HEADS * HEAD_BLOCK), tok),
        pl.BlockSpec((1, tm, N_HEADS * HEAD_BLOCK), tok),
        pl.BlockSpec((1, tm, N_HEADS * V_HEAD), tok),
        pl.BlockSpec((1, tm, C_CONV), tok),
    ]
    out_shape = [
        jax.ShapeDtypeStruct((b, t, N_HEADS * HEAD_BLOCK), BF16),
        jax.ShapeDtypeStruct((b, t, N_HEADS * HEAD_BLOCK), BF16),
        jax.ShapeDtypeStruct((b, t, N_HEADS * V_HEAD), BF16),
        jax.ShapeDtypeStruct((b, t, C_CONV), F32),
    ]
    if rope:
        in_specs += [
            pl.BlockSpec((Q_LORA, N_HEADS * HEAD_BLOCK), const2),
            pl.BlockSpec((tm, HEAD_BLOCK), lambda i, j: (j, 0)),
            pl.BlockSpec((tm, HEAD_BLOCK), lambda i, j: (j, 0)),
        ]
        args += [lw["wuqs"], rope_tabs[0], rope_tabs[1]]
    else:
        out_specs += [
            pl.BlockSpec((1, tm, KV_LORA), tok),
            pl.BlockSpec((1, tm, QK_ROPE), tok),
        ]
        out_shape += [
            jax.ShapeDtypeStruct((b, t, KV_LORA), F32),
            jax.ShapeDtypeStruct((b, t, QK_ROPE), F32),
        ]
    return pl.pallas_call(
        functools.partial(_premix_kernel, rope),
        grid=(b, nt),
        in_specs=in_specs,
        out_specs=out_specs,
        out_shape=out_shape,
        compiler_params=_params("parallel", "parallel"),
        name="premix_rope" if rope else "premix",
    )(*args)


def _ctx_kernel(ckv_ref, kr_ref, wuk_ref, wuv_ref, k_ref, v_ref):
    ckv_b = ckv_ref[...].astype(BF16)
    kn = jnp.dot(ckv_b, wuk_ref[...], preferred_element_type=F32)
    v_ref[0] = jnp.dot(ckv_b, wuv_ref[...], preferred_element_type=F32).astype(BF16)
    kr = kr_ref[...]
    for hd in range(N_HEADS):
        sl = slice(hd * HEAD_BLOCK, (hd + 1) * HEAD_BLOCK)
        k_ref[0, :, sl] = (kn[:, sl] + kr).astype(BF16)


def _ctx_call(cache_ckv, cache_kr_pad, layer, lw):
    b, _, s, _ = cache_ckv.shape
    const2 = lambda i: (0, 0)
    return pl.pallas_call(
        _ctx_kernel,
        grid=(b,),
        in_specs=[
            pl.BlockSpec((None, None, s, KV_LORA), lambda i: (i, layer, 0, 0)),
            pl.BlockSpec((None, None, s, HEAD_BLOCK), lambda i: (i, layer, 0, 0)),
            pl.BlockSpec((KV_LORA, N_HEADS * HEAD_BLOCK), const2),
            pl.BlockSpec((KV_LORA, N_HEADS * V_HEAD), const2),
        ],
        out_specs=[
            pl.BlockSpec((1, s, N_HEADS * HEAD_BLOCK), lambda i: (i, 0, 0)),
            pl.BlockSpec((1, s, N_HEADS * V_HEAD), lambda i: (i, 0, 0)),
        ],
        out_shape=[
            jax.ShapeDtypeStruct((b, s, N_HEADS * HEAD_BLOCK), BF16),
            jax.ShapeDtypeStruct((b, s, N_HEADS * V_HEAD), BF16),
        ],
        compiler_params=_params("parallel"),
        name="ctx_kv",
    )(cache_ckv, cache_kr_pad, lw["wuk"], lw["wuv"])


_NT = (((1,), (1,)), ((), ()))


def _attn_kernel(ctx, q_ref, k_ref, v_ref, *rest):
    if ctx:
        kc_ref, vc_ref, o_ref = rest
    else:
        (o_ref,) = rest
    outs = []
    for hd in range(N_HEADS):
        sl = slice(hd * HEAD_BLOCK, (hd + 1) * HEAD_BLOCK)
        vs = slice(hd * V_HEAD, (hd + 1) * V_HEAD)
        qh = q_ref[0, :, sl]
        s = lax.dot_general(qh, k_ref[0, :, sl], _NT, preferred_element_type=F32)
        m = jnp.max(s, axis=-1, keepdims=True)
        if ctx:
            sc = lax.dot_general(qh, kc_ref[0, :, sl], _NT, preferred_element_type=F32)
            m = jnp.maximum(m, jnp.max(sc, axis=-1, keepdims=True))
        p = jnp.exp2(s - m)
        den = jnp.sum(p, axis=-1, keepdims=True)
        o = jnp.dot(p.astype(BF16), v_ref[0, :, vs], preferred_element_type=F32)
        if ctx:
            pc = jnp.exp2(sc - m)
            den = den + jnp.sum(pc, axis=-1, keepdims=True)
            o = o + jnp.dot(pc.astype(BF16), vc_ref[0, :, vs], preferred_element_type=F32)
        outs.append(o / den)
    o_ref[0] = jnp.concatenate(outs, axis=-1).astype(BF16)


def _attn_call(q, k, v, tq, ctx_kv=None):
    b, t, _ = q.shape
    ctx = ctx_kv is not None
    tok = lambda i, j: (i, j, 0)
    seq = lambda i, j: (i, 0, 0)
    in_specs = [
        pl.BlockSpec((1, tq, N_HEADS * HEAD_BLOCK), tok),
        pl.BlockSpec((1, t, N_HEADS * HEAD_BLOCK), seq),
        pl.BlockSpec((1, t, N_HEADS * V_HEAD), seq),
    ]
    args = [q, k, v]
    if ctx:
        s = ctx_kv[0].shape[1]
        in_specs += [
            pl.BlockSpec((1, s, N_HEADS * HEAD_BLOCK), seq),
            pl.BlockSpec((1, s, N_HEADS * V_HEAD), seq),
        ]
        args += list(ctx_kv)
    return pl.pallas_call(
        functools.partial(_attn_kernel, ctx),
        grid=(b, t // tq),
        in_specs=in_specs,
        out_specs=pl.BlockSpec((1, tq, N_HEADS * V_HEAD), tok),
        out_shape=jax.ShapeDtypeStruct((b, t, N_HEADS * V_HEAD), BF16),
        compiler_params=_params("parallel", "arbitrary"),
        name="attn_ctx" if ctx else "attn",
    )(*args)


CONV_ROWS = 64


def _route(aff, bias):
    sel = aff + bias
    rows = [sel[e:e + 1] for e in range(N_EXPERTS)]
    affr = [aff[e:e + 1] for e in range(N_EXPERTS)]
    scores = []
    for g in range(N_GROUPS):
        a, b, c, d = rows[g * EXPERTS_PER_GROUP:(g + 1) * EXPERTS_PER_GROUP]
        hi1, lo1 = jnp.maximum(a, b), jnp.minimum(a, b)
        hi2, lo2 = jnp.maximum(c, d), jnp.minimum(c, d)
        top1 = jnp.maximum(hi1, hi2)
        top2 = jnp.maximum(jnp.minimum(hi1, hi2), jnp.maximum(lo1, lo2))
        scores.append(top1 + top2)
    best = scores[0]
    gidx = jnp.zeros_like(best, dtype=jnp.int32)
    for g in range(1, N_GROUPS):
        better = scores[g] > best
        best = jnp.where(better, scores[g], best)
        gidx = jnp.where(better, g, gidx)
    gates = []
    for g in range(N_GROUPS):
        in_g = gidx == g
        s = rows[g * EXPERTS_PER_GROUP:(g + 1) * EXPERTS_PER_GROUP]
        af = affr[g * EXPERTS_PER_GROUP:(g + 1) * EXPERTS_PER_GROUP]
        picked = []
        for i in range(EXPERTS_PER_GROUP):
            rank = jnp.zeros_like(gidx)
            for j in range(EXPERTS_PER_GROUP):
                if j == i:
                    continue
                ahead = (s[j] >= s[i]) if j < i else (s[j] > s[i])
                rank = rank + ahead.astype(jnp.int32)
            picked.append(jnp.where((rank < 2) & in_g, af[i], 0.0))
        den = (picked[0] + picked[1]) + (picked[2] + picked[3])
        den = jnp.where(in_g, den, 1.0)
        gates += [pk / den for pk in picked]
    return jnp.concatenate(gates, axis=0)


def _postmix_kernel(nt, x_ref, attn_ref, z_ref, mod_ref, cw_ref, cb_ref, lg_ref, lb_ref, wout_ref,
                    n2g_ref, rwt_ref, rb_ref, x1_ref, h2_ref, gt_ref, zp_ref, zs_ref, cz_ref):
    tm = x_ref.shape[1]
    j = pl.program_id(1)
    t0 = pl.multiple_of(j * tm, tm)

    zp_ref[HALO:HALO + tm, :] = z_ref[0, pl.ds(t0, tm), :]
    zeros = jnp.zeros((HALO, C_CONV), F32)

    @pl.when(j == 0)
    def _():
        zp_ref[0:HALO, :] = zeros

    @pl.when(j > 0)
    def _():
        zp_ref[0:HALO, :] = z_ref[0, pl.ds(t0 - HALO, HALO), :]

    @pl.when(j == nt - 1)
    def _():
        zp_ref[HALO + tm:, :] = zeros

    @pl.when(j < nt - 1)
    def _():
        zp_ref[HALO + tm:, :] = z_ref[0, pl.ds(t0 + tm, HALO), :]

    span = zs_ref.shape[1]
    for o in range(1, SUBLANES):
        zs_ref[o - 1] = zp_ref[o:o + span, :]

    cw = cw_ref[...]
    cb = cb_ref[...]
    for r0 in range(0, tm, CONV_ROWS):
        acc = jnp.broadcast_to(cb, (CONV_ROWS, C_CONV))
        for kk in range(CONV_K):
            off = HALO - CONV_PAD + kk
            o = off % SUBLANES
            base = r0 + off - o
            if o == 0:
                tap = zp_ref[base:base + CONV_ROWS, :]
            else:
                tap = zs_ref[o - 1, base:base + CONV_ROWS, :]
            acc = acc + tap * cw[kk:kk + 1, :]
        cz_ref[r0:r0 + CONV_ROWS, :] = acc

    cz = cz_ref[...]
    mu = jnp.mean(cz, axis=-1, keepdims=True)
    cen = cz - mu
    var = jnp.mean(cen * cen, axis=-1, keepdims=True)
    conv = _silu(cen * lax.rsqrt(var + EPS) * lg_ref[...] + lb_ref[...]).astype(BF16)

    n_attn = N_HEADS * V_HEAD
    y = jnp.dot(attn_ref[0], wout_ref[0:n_attn, :], preferred_element_type=F32)
    y = y + jnp.dot(conv, wout_ref[n_attn:, :], preferred_element_type=F32)

    mod = mod_ref[0]
    g1, sh2, sc2 = mod[2:3], mod[3:4], mod[4:5]
    x1 = x_ref[0] + g1 * y
    x1_ref[0] = x1
    h2 = _rms(x1, n2g_ref[...]) * (1.0 + sc2) + sh2
    h2_ref[0] = h2.astype(BF16)

    logits = lax.dot_general(rwt_ref[...], h2, _NT, precision=lax.Precision.HIGHEST,
                             preferred_element_type=F32)
    gt_ref[...] = _route(jax.nn.sigmoid(logits), rb_ref[...])


def _postmix_call(x, attn, z, mod, lw, rwt, rb, tm):
    b, t, _ = x.shape
    nt = t // tm
    per_b = (lambda i, j: (i, 0, 0)) if mod.shape[0] > 1 else (lambda i, j: (0, 0, 0))
    const2 = lambda i, j: (0, 0)
    tok = lambda i, j: (i, j, 0)
    return pl.pallas_call(
        functools.partial(_postmix_kernel, nt),
        grid=(b, nt),
        in_specs=[
            pl.BlockSpec((1, tm, D_MODEL), tok),
            pl.BlockSpec((1, tm, N_HEADS * V_HEAD), tok),
            pl.BlockSpec((1, t, C_CONV), lambda i, j: (i, 0, 0)),
            pl.BlockSpec((1, N_MOD, D_MODEL), per_b),
            pl.BlockSpec((CONV_K, C_CONV), const2),
            pl.BlockSpec((1, C_CONV), const2),
            pl.BlockSpec((1, C_CONV), const2),
            pl.BlockSpec((1, C_CONV), const2),
            pl.BlockSpec((D_MODEL, D_MODEL), const2),
            pl.BlockSpec((1, D_MODEL), const2),
            pl.BlockSpec((N_EXPERTS, D_MODEL), const2),
            pl.BlockSpec((N_EXPERTS, 1), const2),
        ],
        out_specs=[
            pl.BlockSpec((1, tm, D_MODEL), tok),
            pl.BlockSpec((1, tm, D_MODEL), tok),
            pl.BlockSpec((N_EXPERTS, tm), lambda i, j: (0, i * nt + j)),
        ],
        out_shape=[
            jax.ShapeDtypeStruct((b, t, D_MODEL), F32),
            jax.ShapeDtypeStruct((b, t, D_MODEL), BF16),
            jax.ShapeDtypeStruct((N_EXPERTS, b * t), F32),
        ],
        scratch_shapes=[
            pltpu.VMEM((tm + 2 * HALO, C_CONV), F32),
            pltpu.VMEM((SUBLANES - 1, tm + 2 * HALO - SUBLANES, C_CONV), F32),
            pltpu.VMEM((tm, C_CONV), F32),
        ],
        compiler_params=_params("parallel", "arbitrary"),
        name="postmix",
    )(x, attn, z, mod, lw["conv_w"], lw["conv_b"], lw["ln_g"], lw["ln_b"], lw["wout"],
      lw["n2g"], rwt, rb)


def _moe_kernel(final, x1_ref, h2_ref, g_ref, mod_ref, wg_ref, wu_ref, wd_ref, fg_ref, o_ref,
                acc_ref):
    e = pl.program_id(1)

    @pl.when(e == 0)
    def _():
        acc_ref[...] = jnp.zeros_like(acc_ref)

    h2 = h2_ref[...]
    gates = g_ref[...]
    lane = lax.broadcasted_iota(jnp.int32, gates.shape, 1)
    gate = jnp.sum(jnp.where(lane == e, gates, 0.0), axis=-1, keepdims=True)
    hg = jnp.dot(h2, wg_ref[0], preferred_element_type=F32)
    hu = jnp.dot(h2, wu_ref[0], preferred_element_type=F32)
    hid = (_silu(hg) * hu * gate).astype(BF16)
    acc_ref[...] += jnp.dot(hid, wd_ref[0], preferred_element_type=F32)

    @pl.when(e == N_EXPERTS - 1)
    def _():
        g2 = mod_ref[0][5:6]
        x2 = x1_ref[...] + g2 * acc_ref[...]
        if final:
            x2 = _rms(x2, fg_ref[...])
        o_ref[...] = x2


def _moe_call(x1, h2, gates, mod, lw, final_g, tm, final):
    b, t, _ = x1.shape
    n = b * t
    tiles_per_b = max(t // tm, 1)
    if mod.shape[0] > 1:
        per_b = lambda i, e: (i // tiles_per_b, 0, 0)
    else:
        per_b = lambda i, e: (0, 0, 0)
    tok = lambda i, e: (i, 0)
    out = pl.pallas_call(
        functools.partial(_moe_kernel, final),
        grid=(n // tm, N_EXPERTS),
        in_specs=[
            pl.BlockSpec((tm, D_MODEL), tok),
            pl.BlockSpec((tm, D_MODEL), tok),
            pl.BlockSpec((tm, N_EXPERTS), tok),
            pl.BlockSpec((1, N_MOD, D_MODEL), per_b),
            pl.BlockSpec((1, D_MODEL, D_EXPERT), lambda i, e: (e, 0, 0)),
            pl.BlockSpec((1, D_MODEL, D_EXPERT), lambda i, e: (e, 0, 0)),
            pl.BlockSpec((1, D_EXPERT, D_MODEL), lambda i, e: (e, 0, 0)),
            pl.BlockSpec((1, D_MODEL), lambda i, e: (0, 0)),
        ],
        out_specs=pl.BlockSpec((tm, D_MODEL), tok),
        out_shape=jax.ShapeDtypeStruct((n, D_MODEL), F32),
        scratch_shapes=[pltpu.VMEM((tm, D_MODEL), F32)],
        compiler_params=_params("parallel", "arbitrary"),
        name="moe_final" if final else "moe",
    )(x1.reshape(n, D_MODEL), h2.reshape(n, D_MODEL), gates, mod, lw["wg"], lw["wu"], lw["wd"],
      final_g)
    return out.reshape(b, t, D_MODEL)


_SWAP = np.concatenate([np.arange(8, 16), np.arange(0, 8), np.arange(24, 32), np.arange(16, 24)])


def _prep_layer(l, norm1_g, w_in, q_norm_g, w_uq, kv_norm_g, w_ukv, conv_w, conv_b, conv_ln_g,
                conv_ln_b, w_out, norm2_g, w_gate, w_up, w_down):
    win = w_in[l]
    o_kv, o_kr, o_a, o_g = Q_LORA, Q_LORA + KV_LORA, Q_LORA + KV_LORA + QK_ROPE, Q_LORA + KV_LORA + QK_ROPE + C_CONV
    wkr = win[:, o_kr:o_a]
    pad_r = ((0, 0), (0, HEAD_BLOCK - QK_ROPE))
    win_cat = jnp.concatenate(
        [win[:, :o_kv], win[:, o_kv:o_kr], win[:, o_a:o_g], win[:, o_g:],
         jnp.pad(wkr, pad_r), jnp.pad(wkr[:, _SWAP], pad_r)], axis=1).astype(BF16)

    wuq = w_uq[l].reshape(Q_LORA, N_HEADS, QK_NOPE + QK_ROPE)
    nope, rope = wuq[..., :QK_NOPE], wuq[..., QK_NOPE:]
    zpad = jnp.zeros((Q_LORA, N_HEADS, HEAD_BLOCK - QK_NOPE - QK_ROPE), F32)
    wuq_p = jnp.concatenate([rope, nope, zpad], axis=-1).reshape(Q_LORA, -1).astype(BF16)
    zrest = jnp.zeros((Q_LORA, N_HEADS, HEAD_BLOCK - QK_ROPE), F32)
    wuq_s = jnp.concatenate([rope[..., _SWAP], zrest], axis=-1).reshape(Q_LORA, -1).astype(BF16)

    wukv = w_ukv[l].reshape(KV_LORA, N_HEADS, QK_NOPE + V_HEAD)
    kz0 = jnp.zeros((KV_LORA, N_HEADS, QK_ROPE), F32)
    kz1 = jnp.zeros((KV_LORA, N_HEADS, HEAD_BLOCK - QK_NOPE - QK_ROPE), F32)
    wuk = jnp.concatenate([kz0, wukv[..., :QK_NOPE], kz1], axis=-1).reshape(KV_LORA, -1).astype(BF16)
    wuv = wukv[..., QK_NOPE:].reshape(KV_LORA, -1).astype(BF16)

    return {
        "n1g": norm1_g[l][None], "win": win_cat, "qg": q_norm_g[l][None], "wuq": wuq_p,
        "wuqs": wuq_s, "kvg": kv_norm_g[l][None], "wuk": wuk, "wuv": wuv,
        "conv_w": conv_w[l], "conv_b": conv_b[l][None], "ln_g": conv_ln_g[l][None],
        "ln_b": conv_ln_b[l][None], "wout": w_out[l].astype(BF16), "n2g": norm2_g[l][None],
        "wg": w_gate[l].astype(BF16), "wu": w_up[l].astype(BF16), "wd": w_down[l].astype(BF16),
    }


def _rope_tables(n_tokens):
    rows = n_tokens // GRID_W
    row = jnp.repeat(jnp.arange(rows), GRID_W).astype(F32)
    col = jnp.tile(jnp.arange(GRID_W), rows).astype(F32)
    freqs = ROPE_BASE ** (-jnp.arange(ROPE_PAIRS, dtype=F32) / ROPE_PAIRS)
    ar, ac = row[:, None] * freqs, col[:, None] * freqs
    cr, sr, cc, sc = jnp.cos(ar), jnp.sin(ar), jnp.cos(ac), jnp.sin(ac)
    rest = HEAD_BLOCK - QK_ROPE
    cos = jnp.concatenate([cr, cr, cc, cc, jnp.ones((n_tokens, rest), F32)], axis=1)
    sin = jnp.concatenate([-sr, sr, -sc, sc, jnp.zeros((n_tokens, rest), F32)], axis=1)
    return cos, sin


TM_PROMPT = 256
TM_SAMPLE = 256
TQ_SAMPLE = 512
TM_MOE = 1024


def kernel(x_prompt, x_sample, cache_ckv, cache_krope, c, c_ctx, w_ada, b_ada, norm1_g, w_in,
           q_norm_g, w_uq, kv_norm_g, w_ukv, conv_w, conv_b, conv_ln_g, conv_ln_b, w_out,
           norm2_g, router_w, router_b, w_gate, w_up, w_down, final_g):
    dec_b = x_sample.shape[0]
    cvecs = jnp.concatenate(
        [c_ctx[None], c, jnp.zeros((MOD_ROWS - 1 - dec_b, D_MODEL), F32)], axis=0)
    mod_all = _ada_call(cvecs, w_ada, b_ada).reshape(DEPTH, MOD_ROWS, N_MOD, D_MODEL)

    rwt = router_w.T
    rb = router_b[:, None]
    fg = final_g[None]
    rope_tabs = _rope_tables(x_sample.shape[1])
    cache_kr_pad = jnp.pad(cache_krope, ((0, 0), (0, 0), (0, 0), (0, HEAD_BLOCK - QK_ROPE)))

    xp, xs = x_prompt, x_sample
    ckvs, krs = [], []
    for l in range(DEPTH):
        lw = _prep_layer(l, norm1_g, w_in, q_norm_g, w_uq, kv_norm_g, w_ukv, conv_w, conv_b,
                         conv_ln_g, conv_ln_b, w_out, norm2_g, w_gate, w_up, w_down)
        final = l == DEPTH - 1
        mod_p = mod_all[l, 0:1]
        mod_s = mod_all[l, 1:1 + dec_b]

        q, k, v, z, ckv, kr = _premix_call(xp, mod_p, lw, TM_PROMPT)
        ckvs.append(ckv)
        krs.append(kr)
        attn = _attn_call(q, k, v, xp.shape[1])
        x1, h2, gt = _postmix_call(xp, attn, z, mod_p, lw, rwt, rb, TM_PROMPT)
        xp = _moe_call(x1, h2, gt.T, mod_p, lw, fg, TM_MOE, final)

        q, k, v, z = _premix_call(xs, mod_s, lw, TM_SAMPLE, rope_tabs)
        ctx_kv = _ctx_call(cache_ckv, cache_kr_pad, l, lw)
        attn = _attn_call(q, k, v, TQ_SAMPLE, ctx_kv)
        x1, h2, gt = _postmix_call(xs, attn, z, mod_s, lw, rwt, rb, TM_SAMPLE)
        xs = _moe_call(x1, h2, gt.T, mod_s, lw, fg, TM_MOE, final)

    return xp, xs, jnp.stack(ckvs, axis=1), jnp.stack(krs, axis=1)
```

```python
import functools

import jax
import jax.numpy as jnp
import numpy as np
from jax import lax
from jax.experimental import pallas as pl
from jax.experimental.pallas import tpu as pltpu
from jax.experimental.pallas import tpu_sc as plsc

D_MODEL = 1024
DEPTH = 4
GRID_W = 64
N_HEADS = 8
QK_NOPE = 64
QK_ROPE = 32
V_HEAD = 64
Q_LORA = 384
KV_LORA = 256
C_CONV = 512
CONV_K = 31
CONV_PAD = CONV_K // 2
N_EXPERTS = 16
N_GROUPS = 4
EXPERTS_PER_GROUP = N_EXPERTS // N_GROUPS
D_EXPERT = 256
ROPE_BASE = 10000.0
ROPE_PAIRS = QK_ROPE // 4
EPS = 1e-6
ATTN_SCALE = (QK_NOPE + QK_ROPE) ** -0.5
LOG2E = 1.4426950408889634

LANES = 128
SUBLANES = 8
HEAD_BLOCK = LANES
N_MOD = 6
MOD_ROWS = 16
HALO = 16
VMEM_LIMIT = 48 * 1024 * 1024

BF16 = jnp.bfloat16
F32 = jnp.float32


def _rms(x, g):
    return x * lax.rsqrt(jnp.mean(x * x, axis=-1, keepdims=True) + EPS) * g


def _silu(x):
    return x * jax.nn.sigmoid(x)


def _params(*sem):
    return pltpu.CompilerParams(dimension_semantics=sem, vmem_limit_bytes=VMEM_LIMIT)


def _ada_kernel(c_ref, w_ref, b_ref, o_ref):
    s = _silu(c_ref[...]).astype(BF16)
    o_ref[...] = jnp.dot(s, w_ref[...].astype(BF16), preferred_element_type=F32) + b_ref[...]


def _ada_call(cvecs, w_ada, b_ada):
    return pl.pallas_call(
        _ada_kernel,
        grid=(DEPTH, N_MOD),
        in_specs=[
            pl.BlockSpec((MOD_ROWS, D_MODEL), lambda l, n: (0, 0)),
            pl.BlockSpec((None, D_MODEL, D_MODEL), lambda l, n: (l, 0, n)),
            pl.BlockSpec((None, 1, D_MODEL), lambda l, n: (l, 0, n)),
        ],
        out_specs=pl.BlockSpec((None, MOD_ROWS, D_MODEL), lambda l, n: (l, 0, n)),
        out_shape=jax.ShapeDtypeStruct((DEPTH, MOD_ROWS, N_MOD * D_MODEL), F32),
        compiler_params=_params("parallel", "parallel"),
        name="ada",
    )(cvecs, w_ada, b_ada.reshape(DEPTH, 1, N_MOD * D_MODEL))


W_IN_Q = 0
W_IN_KV = W_IN_Q + Q_LORA
W_IN_A = W_IN_KV + KV_LORA
W_IN_G = W_IN_A + C_CONV
W_IN_KR = W_IN_G + C_CONV
W_IN_KRS = W_IN_KR + HEAD_BLOCK
W_IN_COLS = W_IN_KRS + HEAD_BLOCK


def _premix_kernel(rope, resid, x_ref, *rest):
    if resid:
        moe_ref, modprev_ref, *rest = rest
    mod_ref, n1g_ref, win_ref, qg_ref, wuq_ref, kvg_ref, wuk_ref, wuv_ref, *rest = rest
    if rope:
        wuqs_ref, cos_ref, sin_ref, *rest = rest
    q_ref, k_ref, v_ref, z_ref, *rest = rest
    if resid:
        x2_ref, *rest = rest
    if not rope:
        ckv_ref, kr_ref = rest
    x = x_ref[0]
    if resid:
        x = x + modprev_ref[0][5:6] * moe_ref[0]
        x2_ref[0] = x
    mod = mod_ref[0]
    sh1, sc1 = mod[0:1], mod[1:2]
    h = (_rms(x, n1g_ref[...]) * (1.0 + sc1) + sh1).astype(BF16)
    proj = jnp.dot(h, win_ref[...], preferred_element_type=F32)
    q_c = proj[:, W_IN_Q:W_IN_KV]
    kv_c = proj[:, W_IN_KV:W_IN_A]
    a = proj[:, W_IN_A:W_IN_G]
    gate = proj[:, W_IN_G:W_IN_KR]
    kr = proj[:, W_IN_KR:W_IN_KRS]

    z_ref[0] = a * jax.nn.sigmoid(gate)

    qn = _rms(q_c, qg_ref[...]).astype(BF16)
    q = jnp.dot(qn, wuq_ref[...], preferred_element_type=F32)
    ckv = _rms(kv_c, kvg_ref[...])
    ckv_b = ckv.astype(BF16)
    kn = jnp.dot(ckv_b, wuk_ref[...], preferred_element_type=F32)
    v_ref[0] = jnp.dot(ckv_b, wuv_ref[...], preferred_element_type=F32).astype(BF16)

    if rope:
        cos = cos_ref[...]
        sin = sin_ref[...]
        qs = jnp.dot(qn, wuqs_ref[...], preferred_element_type=F32)
        kr = kr * cos + proj[:, W_IN_KRS:W_IN_COLS] * sin
    else:
        ckv_ref[0] = ckv
        kr_ref[0] = kr[:, :QK_ROPE]

    for hd in range(N_HEADS):
        sl = slice(hd * HEAD_BLOCK, (hd + 1) * HEAD_BLOCK)
        qh = q[:, sl]
        if rope:
            qh = qh * cos + qs[:, sl] * sin
        q_ref[0, :, sl] = (qh * (ATTN_SCALE * LOG2E)).astype(BF16)
        k_ref[0, :, sl] = (kn[:, sl] + kr).astype(BF16)


def _premix_call(x, mod, lw, tm, rope_tabs=None, moe=None, mod_prev=None):
    b, t, _ = x.shape
    rope = rope_tabs is not None
    resid = moe is not None
    nt = t // tm
    per_b = (lambda i, j: (i, 0, 0)) if mod.shape[0] > 1 else (lambda i, j: (0, 0, 0))
    const2 = lambda i, j: (0, 0)
    tok = lambda i, j: (i, j, 0)
    n_in = W_IN_COLS if rope else W_IN_KRS
    in_specs = [pl.BlockSpec((1, tm, D_MODEL), tok)]
    args = [x]
    if resid:
        in_specs += [pl.BlockSpec((1, tm, D_MODEL), tok), pl.BlockSpec((1, N_MOD, D_MODEL), per_b)]
        args += [moe, mod_prev]
    in_specs += [
        pl.BlockSpec((1, N_MOD, D_MODEL), per_b),
        pl.BlockSpec((1, D_MODEL), const2),
        pl.BlockSpec((D_MODEL, n_in), const2),
        pl.BlockSpec((1, Q_LORA), const2),
        pl.BlockSpec((Q_LORA, N_HEADS * HEAD_BLOCK), const2),
        pl.BlockSpec((1, KV_LORA), const2),
        pl.BlockSpec((KV_LORA, N_HEADS * HEAD_BLOCK), const2),
        pl.BlockSpec((KV_LORA, N_HEADS * V_HEAD), const2),
    ]
    args += [mod, lw["n1g"], lw["win"], lw["qg"], lw["wuq"], lw["kvg"], lw["wuk"], lw["wuv"]]
    out_specs = [
        pl.BlockSpec((1, tm, N_HEADS * HEAD_BLOCK), tok),
        pl.BlockSpec((1, tm, N_HEADS * HEAD_BLOCK), tok),
        pl.BlockSpec((1, tm, N_HEADS * V_HEAD), tok),
        pl.BlockSpec((1, tm, C_CONV), tok),
    ]
    out_shape = [
        jax.ShapeDtypeStruct((b, t, N_HEADS * HEAD_BLOCK), BF16),
        jax.ShapeDtypeStruct((b, t, N_HEADS * HEAD_BLOCK), BF16),
        jax.ShapeDtypeStruct((b, t, N_HEADS * V_HEAD), BF16),
        jax.ShapeDtypeStruct((b, t, C_CONV), F32),
    ]
    if resid:
        out_specs.append(pl.BlockSpec((1, tm, D_MODEL), tok))
        out_shape.append(jax.ShapeDtypeStruct((b, t, D_MODEL), F32))
    if rope:
        in_specs += [
            pl.BlockSpec((Q_LORA, N_HEADS * HEAD_BLOCK), const2),
            pl.BlockSpec((tm, HEAD_BLOCK), lambda i, j: (j, 0)),
            pl.BlockSpec((tm, HEAD_BLOCK), lambda i, j: (j, 0)),
        ]
        args += [lw["wuqs"], rope_tabs[0], rope_tabs[1]]
    else:
        out_specs += [
            pl.BlockSpec((1, tm, KV_LORA), tok),
            pl.BlockSpec((1, tm, QK_ROPE), tok),
        ]
        out_shape += [
            jax.ShapeDtypeStruct((b, t, KV_LORA), F32),
            jax.ShapeDtypeStruct((b, t, QK_ROPE), F32),
        ]
    return pl.pallas_call(
        functools.partial(_premix_kernel, rope, resid),
        grid=(b, nt),
        in_specs=in_specs,
        out_specs=out_specs,
        out_shape=out_shape,
        compiler_params=_params("parallel", "parallel"),
        name="premix_rope" if rope else "premix",
    )(*args)


def _ctx_kernel(ckv_ref, kr_ref, wuk_ref, wuv_ref, k_ref, v_ref):
    ckv_b = ckv_ref[...].astype(BF16)
    kn = jnp.dot(ckv_b, wuk_ref[...], preferred_element_type=F32)
    v_ref[0] = jnp.dot(ckv_b, wuv_ref[...], preferred_element_type=F32).astype(BF16)
    kr = kr_ref[...]
    for hd in range(N_HEADS):
        sl = slice(hd * HEAD_BLOCK, (hd + 1) * HEAD_BLOCK)
        k_ref[0, :, sl] = (kn[:, sl] + kr).astype(BF16)


def _ctx_call(cache_ckv, cache_kr_pad, layer, lw):
    b, _, s, _ = cache_ckv.shape
    const2 = lambda i: (0, 0)
    return pl.pallas_call(
        _ctx_kernel,
        grid=(b,),
        in_specs=[
            pl.BlockSpec((None, None, s, KV_LORA), lambda i: (i, layer, 0, 0)),
            pl.BlockSpec((None, None, s, HEAD_BLOCK), lambda i: (i, layer, 0, 0)),
            pl.BlockSpec((KV_LORA, N_HEADS * HEAD_BLOCK), const2),
            pl.BlockSpec((KV_LORA, N_HEADS * V_HEAD), const2),
        ],
        out_specs=[
            pl.BlockSpec((1, s, N_HEADS * HEAD_BLOCK), lambda i: (i, 0, 0)),
            pl.BlockSpec((1, s, N_HEADS * V_HEAD), lambda i: (i, 0, 0)),
        ],
        out_shape=[
            jax.ShapeDtypeStruct((b, s, N_HEADS * HEAD_BLOCK), BF16),
            jax.ShapeDtypeStruct((b, s, N_HEADS * V_HEAD), BF16),
        ],
        compiler_params=_params("parallel"),
        name="ctx_kv",
    )(cache_ckv, cache_kr_pad, lw["wuk"], lw["wuv"])


_NT = (((1,), (1,)), ((), ()))


def _attn_kernel(ctx, q_ref, k_ref, v_ref, *rest):
    if ctx:
        kc_ref, vc_ref, o_ref = rest
    else:
        (o_ref,) = rest
    outs = []
    for hd in range(N_HEADS):
        sl = slice(hd * HEAD_BLOCK, (hd + 1) * HEAD_BLOCK)
        vs = slice(hd * V_HEAD, (hd + 1) * V_HEAD)
        qh = q_ref[0, :, sl]
        s = lax.dot_general(qh, k_ref[0, :, sl], _NT, preferred_element_type=F32)
        m = jnp.max(s, axis=-1, keepdims=True)
        if ctx:
            sc = lax.dot_general(qh, kc_ref[0, :, sl], _NT, preferred_element_type=F32)
            m = jnp.maximum(m, jnp.max(sc, axis=-1, keepdims=True))
        p = jnp.exp2(s - m)
        den = jnp.sum(p, axis=-1, keepdims=True)
        o = jnp.dot(p.astype(BF16), v_ref[0, :, vs], preferred_element_type=F32)
        if ctx:
            pc = jnp.exp2(sc - m)
            den = den + jnp.sum(pc, axis=-1, keepdims=True)
            o = o + jnp.dot(pc.astype(BF16), vc_ref[0, :, vs], preferred_element_type=F32)
        outs.append(o / den)
    o_ref[0] = jnp.concatenate(outs, axis=-1).astype(BF16)


def _attn_call(q, k, v, tq, ctx_kv=None):
    b, t, _ = q.shape
    ctx = ctx_kv is not None
    tok = lambda i, j: (i, j, 0)
    seq = lambda i, j: (i, 0, 0)
    in_specs = [
        pl.BlockSpec((1, tq, N_HEADS * HEAD_BLOCK), tok),
        pl.BlockSpec((1, t, N_HEADS * HEAD_BLOCK), seq),
        pl.BlockSpec((1, t, N_HEADS * V_HEAD), seq),
    ]
    args = [q, k, v]
    if ctx:
        s = ctx_kv[0].shape[1]
        in_specs += [
            pl.BlockSpec((1, s, N_HEADS * HEAD_BLOCK), seq),
            pl.BlockSpec((1, s, N_HEADS * V_HEAD), seq),
        ]
        args += list(ctx_kv)
    return pl.pallas_call(
        functools.partial(_attn_kernel, ctx),
        grid=(b, t // tq),
        in_specs=in_specs,
        out_specs=pl.BlockSpec((1, tq, N_HEADS * V_HEAD), tok),
        out_shape=jax.ShapeDtypeStruct((b, t, N_HEADS * V_HEAD), BF16),
        compiler_params=_params("parallel", "arbitrary"),
        name="attn_ctx" if ctx else "attn",
    )(*args)


CONV_ROWS = 64


def _route(aff, bias):
    sel = aff + bias
    rows = [sel[e:e + 1] for e in range(N_EXPERTS)]
    affr = [aff[e:e + 1] for e in range(N_EXPERTS)]
    scores = []
    for g in range(N_GROUPS):
        a, b, c, d = rows[g * EXPERTS_PER_GROUP:(g + 1) * EXPERTS_PER_GROUP]
        hi1, lo1 = jnp.maximum(a, b), jnp.minimum(a, b)
        hi2, lo2 = jnp.maximum(c, d), jnp.minimum(c, d)
        top1 = jnp.maximum(hi1, hi2)
        top2 = jnp.maximum(jnp.minimum(hi1, hi2), jnp.maximum(lo1, lo2))
        scores.append(top1 + top2)
    best = scores[0]
    gidx = jnp.zeros_like(best, dtype=jnp.int32)
    for g in range(1, N_GROUPS):
        better = scores[g] > best
        best = jnp.where(better, scores[g], best)
        gidx = jnp.where(better, g, gidx)
    gates = []
    e_lo = jnp.full_like(best, float(N_EXPERTS))
    e_hi = jnp.full_like(best, -1.0)
    for g in range(N_GROUPS):
        in_g = gidx == g
        s = rows[g * EXPERTS_PER_GROUP:(g + 1) * EXPERTS_PER_GROUP]
        af = affr[g * EXPERTS_PER_GROUP:(g + 1) * EXPERTS_PER_GROUP]
        picked = []
        for i in range(EXPERTS_PER_GROUP):
            rank = jnp.zeros_like(gidx)
            for j in range(EXPERTS_PER_GROUP):
                if j == i:
                    continue
                ahead = (s[j] >= s[i]) if j < i else (s[j] > s[i])
                rank = rank + ahead.astype(jnp.int32)
            chosen = (rank < 2) & in_g
            eid = float(g * EXPERTS_PER_GROUP + i)
            e_lo = jnp.where(chosen, jnp.minimum(e_lo, eid), e_lo)
            e_hi = jnp.where(chosen, jnp.maximum(e_hi, eid), e_hi)
            picked.append(jnp.where(chosen, af[i], 0.0))
        den = (picked[0] + picked[1]) + (picked[2] + picked[3])
        den = jnp.where(in_g, den, 1.0)
        gates += [pk / den for pk in picked]
    return jnp.concatenate(gates, axis=0), e_lo, e_hi


def _postmix_kernel(nt, x_ref, attn_ref, z_ref, mod_ref, cw_ref, cb_ref, lg_ref, lb_ref, wout_ref,
                    n2g_ref, rwt_ref, rb_ref, x1_ref, h2_ref, meta_ref, ids_ref, zp_ref, zs_ref, cz_ref):
    tm = x_ref.shape[1]
    j = pl.program_id(1)
    t0 = pl.multiple_of(j * tm, tm)

    zp_ref[HALO:HALO + tm, :] = z_ref[0, pl.ds(t0, tm), :]
    zeros = jnp.zeros((HALO, C_CONV), F32)

    @pl.when(j == 0)
    def _():
        zp_ref[0:HALO, :] = zeros

    @pl.when(j > 0)
    def _():
        zp_ref[0:HALO, :] = z_ref[0, pl.ds(t0 - HALO, HALO), :]

    @pl.when(j == nt - 1)
    def _():
        zp_ref[HALO + tm:, :] = zeros

    @pl.when(j < nt - 1)
    def _():
        zp_ref[HALO + tm:, :] = z_ref[0, pl.ds(t0 + tm, HALO), :]

    span = zs_ref.shape[1]
    for o in range(1, SUBLANES):
        zs_ref[o - 1] = zp_ref[o:o + span, :]

    cw = cw_ref[...]
    cb = cb_ref[...]
    for r0 in range(0, tm, CONV_ROWS):
        acc = jnp.broadcast_to(cb, (CONV_ROWS, C_CONV))
        for kk in range(CONV_K):
            off = HALO - CONV_PAD + kk
            o = off % SUBLANES
            base = r0 + off - o
            if o == 0:
                tap = zp_ref[base:base + CONV_ROWS, :]
            else:
                tap = zs_ref[o - 1, base:base + CONV_ROWS, :]
            acc = acc + tap * cw[kk:kk + 1, :]
        cz_ref[r0:r0 + CONV_ROWS, :] = acc

    cz = cz_ref[...]
    mu = jnp.mean(cz, axis=-1, keepdims=True)
    cen = cz - mu
    var = jnp.mean(cen * cen, axis=-1, keepdims=True)
    conv = _silu(cen * lax.rsqrt(var + EPS) * lg_ref[...] + lb_ref[...]).astype(BF16)

    n_attn = N_HEADS * V_HEAD
    y = jnp.dot(attn_ref[0], wout_ref[0:n_attn, :], preferred_element_type=F32)
    y = y + jnp.dot(conv, wout_ref[n_attn:, :], preferred_element_type=F32)

    mod = mod_ref[0]
    g1, sh2, sc2 = mod[2:3], mod[3:4], mod[4:5]
    x1 = x_ref[0] + g1 * y
    x1_ref[0] = x1
    h2 = _rms(x1, n2g_ref[...]) * (1.0 + sc2) + sh2
    h2_ref[0] = h2

    logits = lax.dot_general(rwt_ref[...], h2, _NT, precision=lax.Precision.HIGHEST,
                             preferred_element_type=F32)
    gates, e_lo, e_hi = _route(jax.nn.sigmoid(logits), rb_ref[...])
    row = lax.broadcasted_iota(jnp.int32, (SUBLANES, tm), 0)
    ids = jnp.where(row == 0, e_lo, jnp.where(row == 1, e_hi, 0.0))
    ids_ref[...] = ids
    rec = jnp.concatenate([gates, ids, jnp.zeros((LANES - N_EXPERTS - SUBLANES, tm), F32)], axis=0)
    meta_ref[0] = rec.T


def _postmix_call(x, attn, z, mod, lw, rwt, rb, tm):
    b, t, _ = x.shape
    nt = t // tm
    per_b = (lambda i, j: (i, 0, 0)) if mod.shape[0] > 1 else (lambda i, j: (0, 0, 0))
    const2 = lambda i, j: (0, 0)
    tok = lambda i, j: (i, j, 0)
    return pl.pallas_call(
        functools.partial(_postmix_kernel, nt),
        grid=(b, nt),
        in_specs=[
            pl.BlockSpec((1, tm, D_MODEL), tok),
            pl.BlockSpec((1, tm, N_HEADS * V_HEAD), tok),
            pl.BlockSpec((1, t, C_CONV), lambda i, j: (i, 0, 0)),
            pl.BlockSpec((1, N_MOD, D_MODEL), per_b),
            pl.BlockSpec((CONV_K, C_CONV), const2),
            pl.BlockSpec((1, C_CONV), const2),
            pl.BlockSpec((1, C_CONV), const2),
            pl.BlockSpec((1, C_CONV), const2),
            pl.BlockSpec((D_MODEL, D_MODEL), const2),
            pl.BlockSpec((1, D_MODEL), const2),
            pl.BlockSpec((N_EXPERTS, D_MODEL), const2),
            pl.BlockSpec((N_EXPERTS, 1), const2),
        ],
        out_specs=[
            pl.BlockSpec((1, tm, D_MODEL), tok),
            pl.BlockSpec((1, tm, D_MODEL), tok),
            pl.BlockSpec((1, tm, LANES), tok),
            pl.BlockSpec((SUBLANES, tm), lambda i, j: (0, i * nt + j)),
        ],
        out_shape=[
            jax.ShapeDtypeStruct((b, t, D_MODEL), F32),
            jax.ShapeDtypeStruct((b, t, D_MODEL), F32),
            jax.ShapeDtypeStruct((b, t, LANES), F32),
            jax.ShapeDtypeStruct((SUBLANES, b * t), F32),
        ],
        scratch_shapes=[
            pltpu.VMEM((tm + 2 * HALO, C_CONV), F32),
            pltpu.VMEM((SUBLANES - 1, tm + 2 * HALO - SUBLANES, C_CONV), F32),
            pltpu.VMEM((tm, C_CONV), F32),
        ],
        compiler_params=_params("parallel", "arbitrary"),
        name="postmix",
    )(x, attn, z, mod, lw["conv_w"], lw["conv_b"], lw["ln_g"], lw["ln_b"], lw["wout"],
      lw["n2g"], rwt, rb)


N_PAIRS = EXPERTS_PER_GROUP * (EXPERTS_PER_GROUP - 1) // 2
N_CLASSES = N_GROUPS * N_PAIRS
_PAIR_LO = np.array([i for i in range(EXPERTS_PER_GROUP) for j in range(i + 1, EXPERTS_PER_GROUP)])
_PAIR_HI = np.array([j for i in range(EXPERTS_PER_GROUP) for j in range(i + 1, EXPERTS_PER_GROUP)])
MOE_ROWS = 256
SC_CORES = 2
SC_SUBCORES = 16
SC_WORKERS = SC_CORES * SC_SUBCORES
SC_WINDOW = 32


def _slot_rows(n):
    return n + N_CLASSES * MOE_ROWS


def _plan(ids, n_slots):
    e_lo = ids[0].astype(jnp.int32)
    e_hi = ids[1].astype(jnp.int32)
    grp = e_lo // EXPERTS_PER_GROUP
    i = e_lo % EXPERTS_PER_GROUP
    j = e_hi % EXPERTS_PER_GROUP
    cls = grp * N_PAIRS + (i * (2 * EXPERTS_PER_GROUP - 1 - i)) // 2 + (j - i - 1)
    onehot = (cls[None, :] == jnp.arange(N_CLASSES, dtype=jnp.int32)[:, None]).astype(jnp.int32)
    csum = jnp.cumsum(onehot, axis=1)
    counts = csum[:, -1]
    padded = (counts + MOE_ROWS - 1) // MOE_ROWS * MOE_ROWS
    pend = jnp.cumsum(padded)
    pstart = pend - padded
    pos = jnp.sum(onehot * (csum - 1 + pstart[:, None]), axis=0).astype(jnp.int32)
    tile_start = jnp.arange(n_slots // MOE_ROWS, dtype=jnp.int32) * MOE_ROWS
    tcls = jnp.minimum(jnp.sum(tile_start[:, None] >= pend[None, :], axis=1), N_CLASSES - 1)
    tgrp, tpair = tcls // N_PAIRS, tcls % N_PAIRS
    t_lo = tgrp * EXPERTS_PER_GROUP + jnp.asarray(_PAIR_LO, jnp.int32)[tpair]
    t_hi = tgrp * EXPERTS_PER_GROUP + jnp.asarray(_PAIR_HI, jnp.int32)[tpair]
    tile_experts = jnp.stack([t_lo, t_hi]).astype(jnp.int32)
    n_used = (pend[-1] // MOE_ROWS).astype(jnp.int32).reshape(1)
    return pos, tile_experts, n_used


def _sc_mesh():
    return plsc.VectorSubcoreMesh(core_axis_name="core", subcore_axis_name="subcore")


def _sc_worker():
    return lax.axis_index("core") * SC_SUBCORES + lax.axis_index("subcore")


def _dispatch_call(h2, meta, pos, n_slots):
    n = h2.shape[0]
    per_worker = n // SC_WORKERS

    @pl.kernel(
        out_type=[jax.ShapeDtypeStruct((n_slots, D_MODEL), F32),
                  jax.ShapeDtypeStruct((n_slots, LANES), F32)],
        mesh=_sc_mesh(),
        scratch_types=[pltpu.VMEM((per_worker,), jnp.int32),
                       pltpu.VMEM((SC_WINDOW, D_MODEL), F32),
                       pltpu.VMEM((SC_WINDOW, LANES), F32)],
    )
    def dispatch(h2_hbm, meta_hbm, pos_hbm, xs_hbm, ms_hbm, slot_v, row_v, rec_v):
        wid = _sc_worker()
        base = wid * per_worker
        pltpu.sync_copy(pos_hbm.at[wid], slot_v)

        @pl.loop(0, per_worker // SC_WINDOW)
        def _(step):
            rows = pl.ds(base + step * SC_WINDOW, SC_WINDOW)
            slots = slot_v.at[pl.ds(step * SC_WINDOW, SC_WINDOW)]
            pltpu.sync_copy(h2_hbm.at[rows], row_v)
            pltpu.sync_copy(row_v, xs_hbm.at[slots])
            pltpu.sync_copy(meta_hbm.at[rows], rec_v)
            pltpu.sync_copy(rec_v, ms_hbm.at[slots])

    return dispatch(h2, meta, pos.reshape(SC_WORKERS, per_worker))


def _combine_call(ys, pos):
    n = pos.shape[0]
    per_worker = n // SC_WORKERS

    @pl.kernel(
        out_type=jax.ShapeDtypeStruct((n, D_MODEL), F32),
        mesh=_sc_mesh(),
        scratch_types=[pltpu.VMEM((per_worker,), jnp.int32),
                       pltpu.VMEM((SC_WINDOW, D_MODEL), F32)],
    )
    def combine(ys_hbm, pos_hbm, out_hbm, slot_v, row_v):
        wid = _sc_worker()
        base = wid * per_worker
        pltpu.sync_copy(pos_hbm.at[wid], slot_v)

        @pl.loop(0, per_worker // SC_WINDOW)
        def _(step):
            slots = slot_v.at[pl.ds(step * SC_WINDOW, SC_WINDOW)]
            pltpu.sync_copy(ys_hbm.at[slots], row_v)
            pltpu.sync_copy(row_v, out_hbm.at[pl.ds(base + step * SC_WINDOW, SC_WINDOW)])

    return combine(ys, pos.reshape(SC_WORKERS, per_worker))


def _experts_kernel(te_ref, nu_ref, xs_ref, ms_ref, wg_lo, wu_lo, wd_lo, wg_hi, wu_hi, wd_hi,
                    ys_ref):
    i = pl.program_id(0)

    @pl.when(i < nu_ref[0])
    def _():
        x = xs_ref[...].astype(BF16)
        rec = ms_ref[...]
        lane = lax.broadcasted_iota(jnp.int32, rec.shape, 1)

        def expert(e, wg_ref, wu_ref, wd_ref):
            gate = jnp.sum(jnp.where(lane == e, rec, 0.0), axis=-1, keepdims=True)
            hg = jnp.dot(x, wg_ref[0], preferred_element_type=F32)
            hu = jnp.dot(x, wu_ref[0], preferred_element_type=F32)
            hid = (_silu(hg) * hu * gate).astype(BF16)
            return jnp.dot(hid, wd_ref[0], preferred_element_type=F32)

        ys_ref[...] = (expert(te_ref[0, i], wg_lo, wu_lo, wd_lo)
                       + expert(te_ref[1, i], wg_hi, wu_hi, wd_hi))

    @pl.when(i >= nu_ref[0])
    def _():
        ys_ref[...] = jnp.zeros_like(ys_ref)


def _experts_call(xs, ms, tile_experts, n_used, lw):
    n_slots = xs.shape[0]
    rows = lambda i, te, nu: (i, 0)
    lo = lambda i, te, nu: (te[0, i], 0, 0)
    hi = lambda i, te, nu: (te[1, i], 0, 0)
    up_spec = lambda f: pl.BlockSpec((1, D_MODEL, D_EXPERT), f)
    down_spec = lambda f: pl.BlockSpec((1, D_EXPERT, D_MODEL), f)
    return pl.pallas_call(
        _experts_kernel,
        grid_spec=pltpu.PrefetchScalarGridSpec(
            num_scalar_prefetch=2,
            grid=(n_slots // MOE_ROWS,),
            in_specs=[
                pl.BlockSpec((MOE_ROWS, D_MODEL), rows),
                pl.BlockSpec((MOE_ROWS, LANES), rows),
                up_spec(lo), up_spec(lo), down_spec(lo),
                up_spec(hi), up_spec(hi), down_spec(hi),
            ],
            out_specs=pl.BlockSpec((MOE_ROWS, D_MODEL), rows),
        ),
        out_shape=jax.ShapeDtypeStruct((n_slots, D_MODEL), F32),
        compiler_params=_params("arbitrary"),
        name="experts",
    )(tile_experts, n_used, xs, ms, lw["wg"], lw["wu"], lw["wd"], lw["wg"], lw["wu"], lw["wd"])


def _routed_moe(h2, meta, ids, lw):
    b, t, _ = h2.shape
    n = b * t
    n_slots = _slot_rows(n)
    pos, tile_experts, n_used = _plan(ids, n_slots)
    xs, ms = _dispatch_call(h2.reshape(n, D_MODEL), meta.reshape(n, LANES), pos, n_slots)
    ys = _experts_call(xs, ms, tile_experts, n_used, lw)
    return _combine_call(ys, pos).reshape(b, t, D_MODEL)


def _final_kernel(x1_ref, moe_ref, mod_ref, fg_ref, o_ref):
    x2 = x1_ref[0] + mod_ref[0][5:6] * moe_ref[0]
    o_ref[0] = _rms(x2, fg_ref[...])


def _final_call(x1, moe, mod, final_g, tm):
    b, t, _ = x1.shape
    per_b = (lambda i, j: (i, 0, 0)) if mod.shape[0] > 1 else (lambda i, j: (0, 0, 0))
    tok = lambda i, j: (i, j, 0)
    return pl.pallas_call(
        _final_kernel,
        grid=(b, t // tm),
        in_specs=[
            pl.BlockSpec((1, tm, D_MODEL), tok),
            pl.BlockSpec((1, tm, D_MODEL), tok),
            pl.BlockSpec((1, N_MOD, D_MODEL), per_b),
            pl.BlockSpec((1, D_MODEL), lambda i, j: (0, 0)),
        ],
        out_specs=pl.BlockSpec((1, tm, D_MODEL), tok),
        out_shape=jax.ShapeDtypeStruct((b, t, D_MODEL), F32),
        compiler_params=_params("parallel", "parallel"),
        name="final_norm",
    )(x1, moe, mod, final_g)


_SWAP = np.concatenate([np.arange(8, 16), np.arange(0, 8), np.arange(24, 32), np.arange(16, 24)])


def _prep_layer(l, norm1_g, w_in, q_norm_g, w_uq, kv_norm_g, w_ukv, conv_w, conv_b, conv_ln_g,
                conv_ln_b, w_out, norm2_g, w_gate, w_up, w_down):
    win = w_in[l]
    o_kv, o_kr, o_a, o_g = Q_LORA, Q_LORA + KV_LORA, Q_LORA + KV_LORA + QK_ROPE, Q_LORA + KV_LORA + QK_ROPE + C_CONV
    wkr = win[:, o_kr:o_a]
    pad_r = ((0, 0), (0, HEAD_BLOCK - QK_ROPE))
    win_cat = jnp.concatenate(
        [win[:, :o_kv], win[:, o_kv:o_kr], win[:, o_a:o_g], win[:, o_g:],
         jnp.pad(wkr, pad_r), jnp.pad(wkr[:, _SWAP], pad_r)], axis=1).astype(BF16)

    wuq = w_uq[l].reshape(Q_LORA, N_HEADS, QK_NOPE + QK_ROPE)
    nope, rope = wuq[..., :QK_NOPE], wuq[..., QK_NOPE:]
    zpad = jnp.zeros((Q_LORA, N_HEADS, HEAD_BLOCK - QK_NOPE - QK_ROPE), F32)
    wuq_p = jnp.concatenate([rope, nope, zpad], axis=-1).reshape(Q_LORA, -1).astype(BF16)
    zrest = jnp.zeros((Q_LORA, N_HEADS, HEAD_BLOCK - QK_ROPE), F32)
    wuq_s = jnp.concatenate([rope[..., _SWAP], zrest], axis=-1).reshape(Q_LORA, -1).astype(BF16)

    wukv = w_ukv[l].reshape(KV_LORA, N_HEADS, QK_NOPE + V_HEAD)
    kz0 = jnp.zeros((KV_LORA, N_HEADS, QK_ROPE), F32)
    kz1 = jnp.zeros((KV_LORA, N_HEADS, HEAD_BLOCK - QK_NOPE - QK_ROPE), F32)
    wuk = jnp.concatenate([kz0, wukv[..., :QK_NOPE], kz1], axis=-1).reshape(KV_LORA, -1).astype(BF16)
    wuv = wukv[..., QK_NOPE:].reshape(KV_LORA, -1).astype(BF16)

    return {
        "n1g": norm1_g[l][None], "win": win_cat, "qg": q_norm_g[l][None], "wuq": wuq_p,
        "wuqs": wuq_s, "kvg": kv_norm_g[l][None], "wuk": wuk, "wuv": wuv,
        "conv_w": conv_w[l], "conv_b": conv_b[l][None], "ln_g": conv_ln_g[l][None],
        "ln_b": conv_ln_b[l][None], "wout": w_out[l].astype(BF16), "n2g": norm2_g[l][None],
        "wg": w_gate[l].astype(BF16), "wu": w_up[l].astype(BF16), "wd": w_down[l].astype(BF16),
    }


def _rope_tables(n_tokens):
    rows = n_tokens // GRID_W
    row = jnp.repeat(jnp.arange(rows), GRID_W).astype(F32)
    col = jnp.tile(jnp.arange(GRID_W), rows).astype(F32)
    freqs = ROPE_BASE ** (-jnp.arange(ROPE_PAIRS, dtype=F32) / ROPE_PAIRS)
    ar, ac = row[:, None] * freqs, col[:, None] * freqs
    cr, sr, cc, sc = jnp.cos(ar), jnp.sin(ar), jnp.cos(ac), jnp.sin(ac)
    rest = HEAD_BLOCK - QK_ROPE
    cos = jnp.concatenate([cr, cr, cc, cc, jnp.ones((n_tokens, rest), F32)], axis=1)
    sin = jnp.concatenate([-sr, sr, -sc, sc, jnp.zeros((n_tokens, rest), F32)], axis=1)
    return cos, sin


TM_PROMPT = 256
TM_SAMPLE = 256
TQ_SAMPLE = 512


def kernel(x_prompt, x_sample, cache_ckv, cache_krope, c, c_ctx, w_ada, b_ada, norm1_g, w_in,
           q_norm_g, w_uq, kv_norm_g, w_ukv, conv_w, conv_b, conv_ln_g, conv_ln_b, w_out,
           norm2_g, router_w, router_b, w_gate, w_up, w_down, final_g):
    dec_b = x_sample.shape[0]
    cvecs = jnp.concatenate(
        [c_ctx[None], c, jnp.zeros((MOD_ROWS - 1 - dec_b, D_MODEL), F32)], axis=0)
    mod_all = _ada_call(cvecs, w_ada, b_ada).reshape(DEPTH, MOD_ROWS, N_MOD, D_MODEL)

    rwt = router_w.T
    rb = router_b[:, None]
    fg = final_g[None]
    rope_tabs = _rope_tables(x_sample.shape[1])
    cache_kr_pad = jnp.pad(cache_krope, ((0, 0), (0, 0), (0, 0), (0, HEAD_BLOCK - QK_ROPE)))

    xp, xs = x_prompt, x_sample
    moe_p = moe_s = mod_p = mod_s = None
    ckvs, krs = [], []
    for l in range(DEPTH):
        lw = _prep_layer(l, norm1_g, w_in, q_norm_g, w_uq, kv_norm_g, w_ukv, conv_w, conv_b,
                         conv_ln_g, conv_ln_b, w_out, norm2_g, w_gate, w_up, w_down)
        prev_p, prev_s = mod_p, mod_s
        mod_p = mod_all[l, 0:1]
        mod_s = mod_all[l, 1:1 + dec_b]

        outs = _premix_call(xp, mod_p, lw, TM_PROMPT, moe=moe_p, mod_prev=prev_p)
        if l > 0:
            q, k, v, z, xp, ckv, kr = outs
        else:
            q, k, v, z, ckv, kr = outs
        ckvs.append(ckv)
        krs.append(kr)
        attn = _attn_call(q, k, v, xp.shape[1])
        xp, h2, meta, ids = _postmix_call(xp, attn, z, mod_p, lw, rwt, rb, TM_PROMPT)
        moe_p = _routed_moe(h2, meta, ids, lw)

        outs = _premix_call(xs, mod_s, lw, TM_SAMPLE, rope_tabs, moe=moe_s, mod_prev=prev_s)
        if l > 0:
            q, k, v, z, xs = outs
        else:
            q, k, v, z = outs
        ctx_kv = _ctx_call(cache_ckv, cache_kr_pad, l, lw)
        attn = _attn_call(q, k, v, TQ_SAMPLE, ctx_kv)
        xs, h2, meta, ids = _postmix_call(xs, attn, z, mod_s, lw, rwt, rb, TM_SAMPLE)
        moe_s = _routed_moe(h2, meta, ids, lw)

    y_prompt = _final_call(xp, moe_p, mod_p, fg, TM_PROMPT)
    y_sample = _final_call(xs, moe_s, mod_s, fg, TM_SAMPLE)
    return y_prompt, y_sample, jnp.stack(ckvs, axis=1), jnp.stack(krs, axis=1)
```

```python
import functools

import jax
import jax.numpy as jnp
import numpy as np
from jax import lax
from jax.experimental import pallas as pl
from jax.experimental.pallas import tpu as pltpu
from jax.experimental.pallas import tpu_sc as plsc

D_MODEL = 1024
DEPTH = 4
GRID_W = 64
N_HEADS = 8
QK_NOPE = 64
QK_ROPE = 32
V_HEAD = 64
Q_LORA = 384
KV_LORA = 256
C_CONV = 512
CONV_K = 31
CONV_PAD = CONV_K // 2
N_EXPERTS = 16
N_GROUPS = 4
EXPERTS_PER_GROUP = N_EXPERTS // N_GROUPS
D_EXPERT = 256
ROPE_BASE = 10000.0
ROPE_PAIRS = QK_ROPE // 4
EPS = 1e-6
ATTN_SCALE = (QK_NOPE + QK_ROPE) ** -0.5
LOG2E = 1.4426950408889634

LANES = 128
SUBLANES = 8
HEAD_BLOCK = LANES
N_MOD = 6
MOD_ROWS = 16
HALO = 16
VMEM_LIMIT = 48 * 1024 * 1024

BF16 = jnp.bfloat16
F32 = jnp.float32


def _rms(x, g):
    return x * lax.rsqrt(jnp.mean(x * x, axis=-1, keepdims=True) + EPS) * g


def _silu(x):
    return x * jax.nn.sigmoid(x)


def _ones_lane(width):
    lane = lax.broadcasted_iota(jnp.int32, (1, width), 1)
    return jnp.where(lane % HEAD_BLOCK == V_HEAD, 1.0, 0.0)


def _params(*sem):
    return pltpu.CompilerParams(dimension_semantics=sem, vmem_limit_bytes=VMEM_LIMIT)


def _ada_kernel(c_ref, w_ref, b_ref, o_ref):
    s = _silu(c_ref[...]).astype(BF16)
    o_ref[...] = jnp.dot(s, w_ref[...].astype(BF16), preferred_element_type=F32) + b_ref[...]


def _ada_call(cvecs, w_ada, b_ada):
    return pl.pallas_call(
        _ada_kernel,
        grid=(DEPTH, N_MOD),
        in_specs=[
            pl.BlockSpec((MOD_ROWS, D_MODEL), lambda l, n: (0, 0)),
            pl.BlockSpec((None, D_MODEL, D_MODEL), lambda l, n: (l, 0, n)),
            pl.BlockSpec((None, 1, D_MODEL), lambda l, n: (l, 0, n)),
        ],
        out_specs=pl.BlockSpec((None, MOD_ROWS, D_MODEL), lambda l, n: (l, 0, n)),
        out_shape=jax.ShapeDtypeStruct((DEPTH, MOD_ROWS, N_MOD * D_MODEL), F32),
        compiler_params=_params("parallel", "parallel"),
        name="ada",
    )(cvecs, w_ada, b_ada.reshape(DEPTH, 1, N_MOD * D_MODEL))


W_IN_Q = 0
W_IN_KV = W_IN_Q + Q_LORA
W_IN_A = W_IN_KV + KV_LORA
W_IN_G = W_IN_A + C_CONV
W_IN_KR = W_IN_G + C_CONV
W_IN_KRS = W_IN_KR + HEAD_BLOCK
W_IN_COLS = W_IN_KRS + HEAD_BLOCK


def _premix_kernel(rope, resid, x_ref, *rest):
    if resid:
        moe_ref, modprev_ref, *rest = rest
    mod_ref, n1g_ref, win_ref, qg_ref, wuq_ref, kvg_ref, wuk_ref, wuv_ref, *rest = rest
    if rope:
        wuqs_ref, cos_ref, sin_ref, *rest = rest
    q_ref, k_ref, v_ref, z_ref, *rest = rest
    if resid:
        x2_ref, *rest = rest
    if not rope:
        ckv_ref, kr_ref = rest
    x = x_ref[0]
    if resid:
        x = x + modprev_ref[0][5:6] * moe_ref[0]
        x2_ref[0] = x
    mod = mod_ref[0]
    sh1, sc1 = mod[0:1], mod[1:2]
    h = (_rms(x, n1g_ref[...]) * (1.0 + sc1) + sh1).astype(BF16)
    proj = jnp.dot(h, win_ref[...], preferred_element_type=F32)
    q_c = proj[:, W_IN_Q:W_IN_KV]
    kv_c = proj[:, W_IN_KV:W_IN_A]
    a = proj[:, W_IN_A:W_IN_G]
    gate = proj[:, W_IN_G:W_IN_KR]
    kr = proj[:, W_IN_KR:W_IN_KRS]

    z_ref[0] = a * jax.nn.sigmoid(gate)

    qn = _rms(q_c, qg_ref[...]).astype(BF16)
    q = jnp.dot(qn, wuq_ref[...], preferred_element_type=F32)
    ckv = _rms(kv_c, kvg_ref[...])
    ckv_b = ckv.astype(BF16)
    kn = jnp.dot(ckv_b, wuk_ref[...], preferred_element_type=F32)
    v = jnp.dot(ckv_b, wuv_ref[...], preferred_element_type=F32)
    v_ref[0] = (v + _ones_lane(v.shape[1])).astype(BF16)

    if rope:
        cos = cos_ref[...]
        sin = sin_ref[...]
        qs = jnp.dot(qn, wuqs_ref[...], preferred_element_type=F32)
        kr = kr * cos + proj[:, W_IN_KRS:W_IN_COLS] * sin
    else:
        ckv_ref[0] = ckv
        kr_ref[0] = kr[:, :QK_ROPE]

    for hd in range(N_HEADS):
        sl = slice(hd * HEAD_BLOCK, (hd + 1) * HEAD_BLOCK)
        qh = q[:, sl]
        if rope:
            qh = qh * cos + qs[:, sl] * sin
        q_ref[0, :, sl] = (qh * (ATTN_SCALE * LOG2E)).astype(BF16)
        k_ref[0, :, sl] = (kn[:, sl] + kr).astype(BF16)


def _premix_call(x, mod, lw, tm, rope_tabs=None, moe=None, mod_prev=None):
    b, t, _ = x.shape
    rope = rope_tabs is not None
    resid = moe is not None
    nt = t // tm
    per_b = (lambda i, j: (i, 0, 0)) if mod.shape[0] > 1 else (lambda i, j: (0, 0, 0))
    const2 = lambda i, j: (0, 0)
    tok = lambda i, j: (i, j, 0)
    n_in = W_IN_COLS if rope else W_IN_KRS
    in_specs = [pl.BlockSpec((1, tm, D_MODEL), tok)]
    args = [x]
    if resid:
        in_specs += [pl.BlockSpec((1, tm, D_MODEL), tok), pl.BlockSpec((1, N_MOD, D_MODEL), per_b)]
        args += [moe, mod_prev]
    in_specs += [
        pl.BlockSpec((1, N_MOD, D_MODEL), per_b),
        pl.BlockSpec((1, D_MODEL), const2),
        pl.BlockSpec((D_MODEL, n_in), const2),
        pl.BlockSpec((1, Q_LORA), const2),
        pl.BlockSpec((Q_LORA, N_HEADS * HEAD_BLOCK), const2),
        pl.BlockSpec((1, KV_LORA), const2),
        pl.BlockSpec((KV_LORA, N_HEADS * HEAD_BLOCK), const2),
        pl.BlockSpec((KV_LORA, N_HEADS * HEAD_BLOCK), const2),
    ]
    args += [mod, lw["n1g"], lw["win"], lw["qg"], lw["wuq"], lw["kvg"], lw["wuk"], lw["wuv"]]
    out_specs = [
        pl.BlockSpec((1, tm, N_HEADS * HEAD_BLOCK), tok),
        pl.BlockSpec((1, tm, N_HEADS * HEAD_BLOCK), tok),
        pl.BlockSpec((1, tm, N_HEADS * HEAD_BLOCK), tok),
        pl.BlockSpec((1, tm, C_CONV), tok),
    ]
    out_shape = [
        jax.ShapeDtypeStruct((b, t, N_HEADS * HEAD_BLOCK), BF16),
        jax.ShapeDtypeStruct((b, t, N_HEADS * HEAD_BLOCK), BF16),
        jax.ShapeDtypeStruct((b, t, N_HEADS * HEAD_BLOCK), BF16),
        jax.ShapeDtypeStruct((b, t, C_CONV), F32),
    ]
    if resid:
        out_specs.append(pl.BlockSpec((1, tm, D_MODEL), tok))
        out_shape.append(jax.ShapeDtypeStruct((b, t, D_MODEL), F32))
    if rope:
        in_specs += [
            pl.BlockSpec((Q_LORA, N_HEADS * HEAD_BLOCK), const2),
            pl.BlockSpec((tm, HEAD_BLOCK), lambda i, j: (j, 0)),
            pl.BlockSpec((tm, HEAD_BLOCK), lambda i, j: (j, 0)),
        ]
        args += [lw["wuqs"], rope_tabs[0], rope_tabs[1]]
    else:
        out_specs += [
            pl.BlockSpec((1, tm, KV_LORA), tok),
            pl.BlockSpec((1, tm, QK_ROPE), tok),
        ]
        out_shape += [
            jax.ShapeDtypeStruct((b, t, KV_LORA), F32),
            jax.ShapeDtypeStruct((b, t, QK_ROPE), F32),
        ]
    return pl.pallas_call(
        functools.partial(_premix_kernel, rope, resid),
        grid=(b, nt),
        in_specs=in_specs,
        out_specs=out_specs,
        out_shape=out_shape,
        compiler_params=_params("parallel", "parallel"),
        name="premix_rope" if rope else "premix",
    )(*args)


def _ctx_kernel(ckv_ref, kr_ref, wuk_ref, wuv_ref, k_ref, v_ref):
    ckv_b = ckv_ref[...].astype(BF16)
    kn = jnp.dot(ckv_b, wuk_ref[...], preferred_element_type=F32)
    v = jnp.dot(ckv_b, wuv_ref[...], preferred_element_type=F32)
    v_ref[0] = (v + _ones_lane(v.shape[1])).astype(BF16)
    kr = kr_ref[...]
    for hd in range(N_HEADS):
        sl = slice(hd * HEAD_BLOCK, (hd + 1) * HEAD_BLOCK)
        k_ref[0, :, sl] = (kn[:, sl] + kr).astype(BF16)


def _ctx_call(cache_ckv, cache_kr_pad, layer, lw):
    b, _, s, _ = cache_ckv.shape
    const2 = lambda i: (0, 0)
    return pl.pallas_call(
        _ctx_kernel,
        grid=(b,),
        in_specs=[
            pl.BlockSpec((None, None, s, KV_LORA), lambda i: (i, layer, 0, 0)),
            pl.BlockSpec((None, None, s, HEAD_BLOCK), lambda i: (i, layer, 0, 0)),
            pl.BlockSpec((KV_LORA, N_HEADS * HEAD_BLOCK), const2),
            pl.BlockSpec((KV_LORA, N_HEADS * HEAD_BLOCK), const2),
        ],
        out_specs=[
            pl.BlockSpec((1, s, N_HEADS * HEAD_BLOCK), lambda i: (i, 0, 0)),
            pl.BlockSpec((1, s, N_HEADS * HEAD_BLOCK), lambda i: (i, 0, 0)),
        ],
        out_shape=[
            jax.ShapeDtypeStruct((b, s, N_HEADS * HEAD_BLOCK), BF16),
            jax.ShapeDtypeStruct((b, s, N_HEADS * HEAD_BLOCK), BF16),
        ],
        compiler_params=_params("parallel"),
        name="ctx_kv",
    )(cache_ckv, cache_kr_pad, lw["wuk"], lw["wuv"])


_NT = (((1,), (1,)), ((), ()))


def _attn_kernel(ctx, q_ref, k_ref, v_ref, *rest):
    if ctx:
        kc_ref, vc_ref, o_ref = rest
    else:
        (o_ref,) = rest
    outs = []
    for hd in range(N_HEADS):
        sl = slice(hd * HEAD_BLOCK, (hd + 1) * HEAD_BLOCK)
        qh = q_ref[0, :, sl]
        s = lax.dot_general(qh, k_ref[0, :, sl], _NT, preferred_element_type=F32)
        m = jnp.max(s, axis=-1, keepdims=True)
        if ctx:
            sc = lax.dot_general(qh, kc_ref[0, :, sl], _NT, preferred_element_type=F32)
            m = jnp.maximum(m, jnp.max(sc, axis=-1, keepdims=True))
        p = jnp.exp2(s - m)
        o = jnp.dot(p.astype(BF16), v_ref[0, :, sl], preferred_element_type=F32)
        if ctx:
            pc = jnp.exp2(sc - m)
            o = o + jnp.dot(pc.astype(BF16), vc_ref[0, :, sl], preferred_element_type=F32)
        outs.append(o[:, :V_HEAD] / o[:, V_HEAD:V_HEAD + 1])
    o_ref[0] = jnp.concatenate(outs, axis=-1).astype(BF16)


def _attn_call(q, k, v, tq, ctx_kv=None):
    b, t, _ = q.shape
    ctx = ctx_kv is not None
    tok = lambda i, j: (i, j, 0)
    seq = lambda i, j: (i, 0, 0)
    in_specs = [
        pl.BlockSpec((1, tq, N_HEADS * HEAD_BLOCK), tok),
        pl.BlockSpec((1, t, N_HEADS * HEAD_BLOCK), seq),
        pl.BlockSpec((1, t, N_HEADS * HEAD_BLOCK), seq),
    ]
    args = [q, k, v]
    if ctx:
        s = ctx_kv[0].shape[1]
        in_specs += [
            pl.BlockSpec((1, s, N_HEADS * HEAD_BLOCK), seq),
            pl.BlockSpec((1, s, N_HEADS * HEAD_BLOCK), seq),
        ]
        args += list(ctx_kv)
    return pl.pallas_call(
        functools.partial(_attn_kernel, ctx),
        grid=(b, t // tq),
        in_specs=in_specs,
        out_specs=pl.BlockSpec((1, tq, N_HEADS * V_HEAD), tok),
        out_shape=jax.ShapeDtypeStruct((b, t, N_HEADS * V_HEAD), BF16),
        compiler_params=_params("parallel", "arbitrary"),
        name="attn_ctx" if ctx else "attn",
    )(*args)


CONV_ROWS = 64


def _route(aff, bias):
    sel = aff + bias
    rows = [sel[e:e + 1] for e in range(N_EXPERTS)]
    affr = [aff[e:e + 1] for e in range(N_EXPERTS)]
    scores = []
    for g in range(N_GROUPS):
        a, b, c, d = rows[g * EXPERTS_PER_GROUP:(g + 1) * EXPERTS_PER_GROUP]
        hi1, lo1 = jnp.maximum(a, b), jnp.minimum(a, b)
        hi2, lo2 = jnp.maximum(c, d), jnp.minimum(c, d)
        top1 = jnp.maximum(hi1, hi2)
        top2 = jnp.maximum(jnp.minimum(hi1, hi2), jnp.maximum(lo1, lo2))
        scores.append(top1 + top2)
    best = scores[0]
    gidx = jnp.zeros_like(best, dtype=jnp.int32)
    for g in range(1, N_GROUPS):
        better = scores[g] > best
        best = jnp.where(better, scores[g], best)
        gidx = jnp.where(better, g, gidx)
    gates = []
    e_lo = jnp.full_like(best, float(N_EXPERTS))
    e_hi = jnp.full_like(best, -1.0)
    for g in range(N_GROUPS):
        in_g = gidx == g
        s = rows[g * EXPERTS_PER_GROUP:(g + 1) * EXPERTS_PER_GROUP]
        af = affr[g * EXPERTS_PER_GROUP:(g + 1) * EXPERTS_PER_GROUP]
        picked = []
        for i in range(EXPERTS_PER_GROUP):
            rank = jnp.zeros_like(gidx)
            for j in range(EXPERTS_PER_GROUP):
                if j == i:
                    continue
                ahead = (s[j] >= s[i]) if j < i else (s[j] > s[i])
                rank = rank + ahead.astype(jnp.int32)
            chosen = (rank < 2) & in_g
            eid = float(g * EXPERTS_PER_GROUP + i)
            e_lo = jnp.where(chosen, jnp.minimum(e_lo, eid), e_lo)
            e_hi = jnp.where(chosen, jnp.maximum(e_hi, eid), e_hi)
            picked.append(jnp.where(chosen, af[i], 0.0))
        den = (picked[0] + picked[1]) + (picked[2] + picked[3])
        den = jnp.where(in_g, den, 1.0)
        gates += [pk / den for pk in picked]
    return jnp.concatenate(gates, axis=0), e_lo, e_hi


def _postmix_kernel(nt, x_ref, attn_ref, z_ref, mod_ref, cw_ref, cb_ref, lg_ref, lb_ref, wout_ref,
                    n2g_ref, rwt_ref, rb_ref, x1_ref, h2_ref, meta_ref, ids_ref, zp_ref, zs_ref, cz_ref):
    tm = x_ref.shape[1]
    j = pl.program_id(1)
    t0 = pl.multiple_of(j * tm, tm)

    zp_ref[HALO:HALO + tm, :] = z_ref[0, pl.ds(t0, tm), :]
    zeros = jnp.zeros((HALO, C_CONV), F32)

    @pl.when(j == 0)
    def _():
        zp_ref[0:HALO, :] = zeros

    @pl.when(j > 0)
    def _():
        zp_ref[0:HALO, :] = z_ref[0, pl.ds(t0 - HALO, HALO), :]

    @pl.when(j == nt - 1)
    def _():
        zp_ref[HALO + tm:, :] = zeros

    @pl.when(j < nt - 1)
    def _():
        zp_ref[HALO + tm:, :] = z_ref[0, pl.ds(t0 + tm, HALO), :]

    span = zs_ref.shape[1]
    for o in range(1, SUBLANES):
        zs_ref[o - 1] = zp_ref[o:o + span, :]

    cw = cw_ref[...]
    cb = cb_ref[...]
    for r0 in range(0, tm, CONV_ROWS):
        acc = jnp.broadcast_to(cb, (CONV_ROWS, C_CONV))
        for kk in range(CONV_K):
            off = HALO - CONV_PAD + kk
            o = off % SUBLANES
            base = r0 + off - o
            if o == 0:
                tap = zp_ref[base:base + CONV_ROWS, :]
            else:
                tap = zs_ref[o - 1, base:base + CONV_ROWS, :]
            acc = acc + tap * cw[kk:kk + 1, :]
        cz_ref[r0:r0 + CONV_ROWS, :] = acc

    cz = cz_ref[...]
    mu = jnp.mean(cz, axis=-1, keepdims=True)
    cen = cz - mu
    var = jnp.mean(cen * cen, axis=-1, keepdims=True)
    conv = _silu(cen * lax.rsqrt(var + EPS) * lg_ref[...] + lb_ref[...]).astype(BF16)

    n_attn = N_HEADS * V_HEAD
    y = jnp.dot(attn_ref[0], wout_ref[0:n_attn, :], preferred_element_type=F32)
    y = y + jnp.dot(conv, wout_ref[n_attn:, :], preferred_element_type=F32)

    mod = mod_ref[0]
    g1, sh2, sc2 = mod[2:3], mod[3:4], mod[4:5]
    x1 = x_ref[0] + g1 * y
    x1_ref[0] = x1
    h2 = _rms(x1, n2g_ref[...]) * (1.0 + sc2) + sh2
    h2_ref[0] = h2

    logits = lax.dot_general(rwt_ref[...], h2, _NT, precision=lax.Precision.HIGHEST,
                             preferred_element_type=F32)
    gates, e_lo, e_hi = _route(jax.nn.sigmoid(logits), rb_ref[...])
    row = lax.broadcasted_iota(jnp.int32, (SUBLANES, tm), 0)
    ids = jnp.where(row == 0, e_lo, jnp.where(row == 1, e_hi, 0.0))
    ids_ref[...] = ids
    rec = jnp.concatenate([gates, ids, jnp.zeros((LANES - N_EXPERTS - SUBLANES, tm), F32)], axis=0)
    meta_ref[0] = rec.T


def _postmix_call(x, attn, z, mod, lw, rwt, rb, tm):
    b, t, _ = x.shape
    nt = t // tm
    per_b = (lambda i, j: (i, 0, 0)) if mod.shape[0] > 1 else (lambda i, j: (0, 0, 0))
    const2 = lambda i, j: (0, 0)
    tok = lambda i, j: (i, j, 0)
    return pl.pallas_call(
        functools.partial(_postmix_kernel, nt),
        grid=(b, nt),
        in_specs=[
            pl.BlockSpec((1, tm, D_MODEL), tok),
            pl.BlockSpec((1, tm, N_HEADS * V_HEAD), tok),
            pl.BlockSpec((1, t, C_CONV), lambda i, j: (i, 0, 0)),
            pl.BlockSpec((1, N_MOD, D_MODEL), per_b),
            pl.BlockSpec((CONV_K, C_CONV), const2),
            pl.BlockSpec((1, C_CONV), const2),
            pl.BlockSpec((1, C_CONV), const2),
            pl.BlockSpec((1, C_CONV), const2),
            pl.BlockSpec((D_MODEL, D_MODEL), const2),
            pl.BlockSpec((1, D_MODEL), const2),
            pl.BlockSpec((N_EXPERTS, D_MODEL), const2),
            pl.BlockSpec((N_EXPERTS, 1), const2),
        ],
        out_specs=[
            pl.BlockSpec((1, tm, D_MODEL), tok),
            pl.BlockSpec((1, tm, D_MODEL), tok),
            pl.BlockSpec((1, tm, LANES), tok),
            pl.BlockSpec((SUBLANES, tm), lambda i, j: (0, i * nt + j)),
        ],
        out_shape=[
            jax.ShapeDtypeStruct((b, t, D_MODEL), F32),
            jax.ShapeDtypeStruct((b, t, D_MODEL), F32),
            jax.ShapeDtypeStruct((b, t, LANES), F32),
            jax.ShapeDtypeStruct((SUBLANES, b * t), F32),
        ],
        scratch_shapes=[
            pltpu.VMEM((tm + 2 * HALO, C_CONV), F32),
            pltpu.VMEM((SUBLANES - 1, tm + 2 * HALO - SUBLANES, C_CONV), F32),
            pltpu.VMEM((tm, C_CONV), F32),
        ],
        compiler_params=_params("parallel", "arbitrary"),
        name="postmix",
    )(x, attn, z, mod, lw["conv_w"], lw["conv_b"], lw["ln_g"], lw["ln_b"], lw["wout"],
      lw["n2g"], rwt, rb)


N_PAIRS = EXPERTS_PER_GROUP * (EXPERTS_PER_GROUP - 1) // 2
N_CLASSES = N_GROUPS * N_PAIRS
_PAIR_LO = np.array([i for i in range(EXPERTS_PER_GROUP) for j in range(i + 1, EXPERTS_PER_GROUP)])
_PAIR_HI = np.array([j for i in range(EXPERTS_PER_GROUP) for j in range(i + 1, EXPERTS_PER_GROUP)])
MOE_ROWS = 256
SC_CORES = 2
SC_SUBCORES = 16
SC_WORKERS = SC_CORES * SC_SUBCORES
SC_WINDOW = 32


def _slot_rows(n):
    return n + N_CLASSES * MOE_ROWS


def _plan(ids, n_slots):
    e_lo = ids[0].astype(jnp.int32)
    e_hi = ids[1].astype(jnp.int32)
    grp = e_lo // EXPERTS_PER_GROUP
    i = e_lo % EXPERTS_PER_GROUP
    j = e_hi % EXPERTS_PER_GROUP
    cls = grp * N_PAIRS + (i * (2 * EXPERTS_PER_GROUP - 1 - i)) // 2 + (j - i - 1)
    onehot = (cls[None, :] == jnp.arange(N_CLASSES, dtype=jnp.int32)[:, None]).astype(jnp.int32)
    csum = jnp.cumsum(onehot, axis=1)
    counts = csum[:, -1]
    padded = (counts + MOE_ROWS - 1) // MOE_ROWS * MOE_ROWS
    pend = jnp.cumsum(padded)
    pstart = pend - padded
    pos = jnp.sum(onehot * (csum - 1 + pstart[:, None]), axis=0).astype(jnp.int32)
    tile_start = jnp.arange(n_slots // MOE_ROWS, dtype=jnp.int32) * MOE_ROWS
    tcls = jnp.minimum(jnp.sum(tile_start[:, None] >= pend[None, :], axis=1), N_CLASSES - 1)
    tgrp, tpair = tcls // N_PAIRS, tcls % N_PAIRS
    t_lo = tgrp * EXPERTS_PER_GROUP + jnp.asarray(_PAIR_LO, jnp.int32)[tpair]
    t_hi = tgrp * EXPERTS_PER_GROUP + jnp.asarray(_PAIR_HI, jnp.int32)[tpair]
    tile_experts = jnp.stack([t_lo, t_hi]).astype(jnp.int32)
    n_used = (pend[-1] // MOE_ROWS).astype(jnp.int32).reshape(1)
    return pos, tile_experts, n_used


def _sc_mesh():
    return plsc.VectorSubcoreMesh(core_axis_name="core", subcore_axis_name="subcore")


def _sc_worker():
    return lax.axis_index("core") * SC_SUBCORES + lax.axis_index("subcore")


def _dispatch_call(h2, meta, pos, n_slots):
    n = h2.shape[0]
    per_worker = n // SC_WORKERS

    @pl.kernel(
        out_type=[jax.ShapeDtypeStruct((n_slots, D_MODEL), F32),
                  jax.ShapeDtypeStruct((n_slots, LANES), F32)],
        mesh=_sc_mesh(),
        scratch_types=[pltpu.VMEM((per_worker,), jnp.int32),
                       pltpu.VMEM((SC_WINDOW, D_MODEL), F32),
                       pltpu.VMEM((SC_WINDOW, LANES), F32)],
    )
    def dispatch(h2_hbm, meta_hbm, pos_hbm, xs_hbm, ms_hbm, slot_v, row_v, rec_v):
        wid = _sc_worker()
        base = wid * per_worker
        pltpu.sync_copy(pos_hbm.at[wid], slot_v)

        @pl.loop(0, per_worker // SC_WINDOW)
        def _(step):
            rows = pl.ds(base + step * SC_WINDOW, SC_WINDOW)
            slots = slot_v.at[pl.ds(step * SC_WINDOW, SC_WINDOW)]
            pltpu.sync_copy(h2_hbm.at[rows], row_v)
            pltpu.sync_copy(row_v, xs_hbm.at[slots])
            pltpu.sync_copy(meta_hbm.at[rows], rec_v)
            pltpu.sync_copy(rec_v, ms_hbm.at[slots])

    return dispatch(h2, meta, pos.reshape(SC_WORKERS, per_worker))


def _combine_call(ys, pos):
    n = pos.shape[0]
    per_worker = n // SC_WORKERS

    @pl.kernel(
        out_type=jax.ShapeDtypeStruct((n, D_MODEL), F32),
        mesh=_sc_mesh(),
        scratch_types=[pltpu.VMEM((per_worker,), jnp.int32),
                       pltpu.VMEM((SC_WINDOW, D_MODEL), F32)],
    )
    def combine(ys_hbm, pos_hbm, out_hbm, slot_v, row_v):
        wid = _sc_worker()
        base = wid * per_worker
        pltpu.sync_copy(pos_hbm.at[wid], slot_v)

        @pl.loop(0, per_worker // SC_WINDOW)
        def _(step):
            slots = slot_v.at[pl.ds(step * SC_WINDOW, SC_WINDOW)]
            pltpu.sync_copy(ys_hbm.at[slots], row_v)
            pltpu.sync_copy(row_v, out_hbm.at[pl.ds(base + step * SC_WINDOW, SC_WINDOW)])

    return combine(ys, pos.reshape(SC_WORKERS, per_worker))


def _experts_kernel(te_ref, nu_ref, xs_ref, ms_ref, wg_ref, wu_ref, wd_ref, ys_ref):
    i = pl.program_id(0)

    @pl.when(i < nu_ref[0])
    def _():
        x = xs_ref[...].astype(BF16)
        rec = ms_ref[...]
        lane = lax.broadcasted_iota(jnp.int32, rec.shape, 1)

        def expert(e):
            gate = jnp.sum(jnp.where(lane == e, rec, 0.0), axis=-1, keepdims=True)
            hg = jnp.dot(x, wg_ref[e], preferred_element_type=F32)
            hu = jnp.dot(x, wu_ref[e], preferred_element_type=F32)
            hid = (_silu(hg) * hu * gate).astype(BF16)
            return jnp.dot(hid, wd_ref[e], preferred_element_type=F32)

        ys_ref[...] = expert(te_ref[0, i]) + expert(te_ref[1, i])

    @pl.when(i >= nu_ref[0])
    def _():
        ys_ref[...] = jnp.zeros_like(ys_ref)


def _experts_call(xs, ms, tile_experts, n_used, lw):
    n_slots = xs.shape[0]
    rows = lambda i, te, nu: (i, 0)
    resident = pl.BlockSpec(memory_space=pltpu.VMEM)
    return pl.pallas_call(
        _experts_kernel,
        grid_spec=pltpu.PrefetchScalarGridSpec(
            num_scalar_prefetch=2,
            grid=(n_slots // MOE_ROWS,),
            in_specs=[
                pl.BlockSpec((MOE_ROWS, D_MODEL), rows),
                pl.BlockSpec((MOE_ROWS, LANES), rows),
                resident, resident, resident,
            ],
            out_specs=pl.BlockSpec((MOE_ROWS, D_MODEL), rows),
        ),
        out_shape=jax.ShapeDtypeStruct((n_slots, D_MODEL), F32),
        compiler_params=_params("arbitrary"),
        name="experts",
    )(tile_experts, n_used, xs, ms, lw["wg"], lw["wu"], lw["wd"])


def _routed_moe(h2, meta, ids, lw):
    b, t, _ = h2.shape
    n = b * t
    n_slots = _slot_rows(n)
    pos, tile_experts, n_used = _plan(ids, n_slots)
    xs, ms = _dispatch_call(h2.reshape(n, D_MODEL), meta.reshape(n, LANES), pos, n_slots)
    ys = _experts_call(xs, ms, tile_experts, n_used, lw)
    return _combine_call(ys, pos).reshape(b, t, D_MODEL)


def _final_kernel(x1_ref, moe_ref, mod_ref, fg_ref, o_ref):
    x2 = x1_ref[0] + mod_ref[0][5:6] * moe_ref[0]
    o_ref[0] = _rms(x2, fg_ref[...])


def _final_call(x1, moe, mod, final_g, tm):
    b, t, _ = x1.shape
    per_b = (lambda i, j: (i, 0, 0)) if mod.shape[0] > 1 else (lambda i, j: (0, 0, 0))
    tok = lambda i, j: (i, j, 0)
    return pl.pallas_call(
        _final_kernel,
        grid=(b, t // tm),
        in_specs=[
            pl.BlockSpec((1, tm, D_MODEL), tok),
            pl.BlockSpec((1, tm, D_MODEL), tok),
            pl.BlockSpec((1, N_MOD, D_MODEL), per_b),
            pl.BlockSpec((1, D_MODEL), lambda i, j: (0, 0)),
        ],
        out_specs=pl.BlockSpec((1, tm, D_MODEL), tok),
        out_shape=jax.ShapeDtypeStruct((b, t, D_MODEL), F32),
        compiler_params=_params("parallel", "parallel"),
        name="final_norm",
    )(x1, moe, mod, final_g)


_SWAP = np.concatenate([np.arange(8, 16), np.arange(0, 8), np.arange(24, 32), np.arange(16, 24)])


def _prep_layer(l, norm1_g, w_in, q_norm_g, w_uq, kv_norm_g, w_ukv, conv_w, conv_b, conv_ln_g,
                conv_ln_b, w_out, norm2_g, w_gate, w_up, w_down):
    win = w_in[l]
    o_kv, o_kr, o_a, o_g = Q_LORA, Q_LORA + KV_LORA, Q_LORA + KV_LORA + QK_ROPE, Q_LORA + KV_LORA + QK_ROPE + C_CONV
    wkr = win[:, o_kr:o_a]
    pad_r = ((0, 0), (0, HEAD_BLOCK - QK_ROPE))
    win_cat = jnp.concatenate(
        [win[:, :o_kv], win[:, o_kv:o_kr], win[:, o_a:o_g], win[:, o_g:],
         jnp.pad(wkr, pad_r), jnp.pad(wkr[:, _SWAP], pad_r)], axis=1).astype(BF16)

    wuq = w_uq[l].reshape(Q_LORA, N_HEADS, QK_NOPE + QK_ROPE)
    nope, rope = wuq[..., :QK_NOPE], wuq[..., QK_NOPE:]
    zpad = jnp.zeros((Q_LORA, N_HEADS, HEAD_BLOCK - QK_NOPE - QK_ROPE), F32)
    wuq_p = jnp.concatenate([rope, nope, zpad], axis=-1).reshape(Q_LORA, -1).astype(BF16)
    zrest = jnp.zeros((Q_LORA, N_HEADS, HEAD_BLOCK - QK_ROPE), F32)
    wuq_s = jnp.concatenate([rope[..., _SWAP], zrest], axis=-1).reshape(Q_LORA, -1).astype(BF16)

    wukv = w_ukv[l].reshape(KV_LORA, N_HEADS, QK_NOPE + V_HEAD)
    kz0 = jnp.zeros((KV_LORA, N_HEADS, QK_ROPE), F32)
    kz1 = jnp.zeros((KV_LORA, N_HEADS, HEAD_BLOCK - QK_NOPE - QK_ROPE), F32)
    wuk = jnp.concatenate([kz0, wukv[..., :QK_NOPE], kz1], axis=-1).reshape(KV_LORA, -1).astype(BF16)
    vz = jnp.zeros((KV_LORA, N_HEADS, HEAD_BLOCK - V_HEAD), F32)
    wuv = jnp.concatenate([wukv[..., QK_NOPE:], vz], axis=-1).reshape(KV_LORA, -1).astype(BF16)

    return {
        "n1g": norm1_g[l][None], "win": win_cat, "qg": q_norm_g[l][None], "wuq": wuq_p,
        "wuqs": wuq_s, "kvg": kv_norm_g[l][None], "wuk": wuk, "wuv": wuv,
        "conv_w": conv_w[l], "conv_b": conv_b[l][None], "ln_g": conv_ln_g[l][None],
        "ln_b": conv_ln_b[l][None], "wout": w_out[l].astype(BF16), "n2g": norm2_g[l][None],
        "wg": w_gate[l].astype(BF16), "wu": w_up[l].astype(BF16), "wd": w_down[l].astype(BF16),
    }


def _rope_tables(n_tokens):
    rows = n_tokens // GRID_W
    row = jnp.repeat(jnp.arange(rows), GRID_W).astype(F32)
    col = jnp.tile(jnp.arange(GRID_W), rows).astype(F32)
    freqs = ROPE_BASE ** (-jnp.arange(ROPE_PAIRS, dtype=F32) / ROPE_PAIRS)
    ar, ac = row[:, None] * freqs, col[:, None] * freqs
    cr, sr, cc, sc = jnp.cos(ar), jnp.sin(ar), jnp.cos(ac), jnp.sin(ac)
    rest = HEAD_BLOCK - QK_ROPE
    cos = jnp.concatenate([cr, cr, cc, cc, jnp.ones((n_tokens, rest), F32)], axis=1)
    sin = jnp.concatenate([-sr, sr, -sc, sc, jnp.zeros((n_tokens, rest), F32)], axis=1)
    return cos, sin


TM_PROMPT = 256
TM_SAMPLE = 512
TQ_SAMPLE = 512


def kernel(x_prompt, x_sample, cache_ckv, cache_krope, c, c_ctx, w_ada, b_ada, norm1_g, w_in,
           q_norm_g, w_uq, kv_norm_g, w_ukv, conv_w, conv_b, conv_ln_g, conv_ln_b, w_out,
           norm2_g, router_w, router_b, w_gate, w_up, w_down, final_g):
    dec_b = x_sample.shape[0]
    cvecs = jnp.concatenate(
        [c_ctx[None], c, jnp.zeros((MOD_ROWS - 1 - dec_b, D_MODEL), F32)], axis=0)
    mod_all = _ada_call(cvecs, w_ada, b_ada).reshape(DEPTH, MOD_ROWS, N_MOD, D_MODEL)

    rwt = router_w.T
    rb = router_b[:, None]
    fg = final_g[None]
    rope_tabs = _rope_tables(x_sample.shape[1])
    cache_kr_pad = jnp.pad(cache_krope, ((0, 0), (0, 0), (0, 0), (0, HEAD_BLOCK - QK_ROPE)))

    xp, xs = x_prompt, x_sample
    moe_p = moe_s = mod_p = mod_s = None
    ckvs, krs = [], []
    for l in range(DEPTH):
        lw = _prep_layer(l, norm1_g, w_in, q_norm_g, w_uq, kv_norm_g, w_ukv, conv_w, conv_b,
                         conv_ln_g, conv_ln_b, w_out, norm2_g, w_gate, w_up, w_down)
        prev_p, prev_s = mod_p, mod_s
        mod_p = mod_all[l, 0:1]
        mod_s = mod_all[l, 1:1 + dec_b]

        outs = _premix_call(xp, mod_p, lw, TM_PROMPT, moe=moe_p, mod_prev=prev_p)
        if l > 0:
            q, k, v, z, xp, ckv, kr = outs
        else:
            q, k, v, z, ckv, kr = outs
        ckvs.append(ckv)
        krs.append(kr)
        attn = _attn_call(q, k, v, xp.shape[1])
        xp, h2, meta, ids = _postmix_call(xp, attn, z, mod_p, lw, rwt, rb, TM_PROMPT)
        moe_p = _routed_moe(h2, meta, ids, lw)

        outs = _premix_call(xs, mod_s, lw, TM_SAMPLE, rope_tabs, moe=moe_s, mod_prev=prev_s)
        if l > 0:
            q, k, v, z, xs = outs
        else:
            q, k, v, z = outs
        ctx_kv = _ctx_call(cache_ckv, cache_kr_pad, l, lw)
        attn = _attn_call(q, k, v, TQ_SAMPLE, ctx_kv)
        xs, h2, meta, ids = _postmix_call(xs, attn, z, mod_s, lw, rwt, rb, TM_SAMPLE)
        moe_s = _routed_moe(h2, meta, ids, lw)

    y_prompt = _final_call(xp, moe_p, mod_p, fg, TM_PROMPT)
    y_sample = _final_call(xs, moe_s, mod_s, fg, TM_SAMPLE)
    return y_prompt, y_sample, jnp.stack(ckvs, axis=1), jnp.stack(krs, axis=1)
```

```python
import functools

import jax
import jax.numpy as jnp
import numpy as np
from jax import lax
from jax.experimental import pallas as pl
from jax.experimental.pallas import tpu as pltpu
from jax.experimental.pallas import tpu_sc as plsc

D_MODEL = 1024
DEPTH = 4
GRID_W = 64
N_HEADS = 8
QK_NOPE = 64
QK_ROPE = 32
V_HEAD = 64
Q_LORA = 384
KV_LORA = 256
C_CONV = 512
CONV_K = 31
CONV_PAD = CONV_K // 2
N_EXPERTS = 16
N_GROUPS = 4
EXPERTS_PER_GROUP = N_EXPERTS // N_GROUPS
D_EXPERT = 256
ROPE_BASE = 10000.0
ROPE_PAIRS = QK_ROPE // 4
EPS = 1e-6
ATTN_SCALE = (QK_NOPE + QK_ROPE) ** -0.5
LOG2E = 1.4426950408889634

LANES = 128
SUBLANES = 8
HEAD_BLOCK = LANES
N_MOD = 6
MOD_ROWS = 16
HALO = 16
VMEM_LIMIT = 48 * 1024 * 1024

BF16 = jnp.bfloat16
F32 = jnp.float32


def _rms(x, g):
    return x * lax.rsqrt(jnp.mean(x * x, axis=-1, keepdims=True) + EPS) * g


def _silu(x):
    return x * jax.nn.sigmoid(x)


def _ones_lane(width):
    lane = lax.broadcasted_iota(jnp.int32, (1, width), 1)
    return jnp.where(lane % HEAD_BLOCK == V_HEAD, 1.0, 0.0)


def _pack_bf16_pairs(x):
    half = x.shape[1] // 2
    xb = x.astype(BF16).astype(F32)
    hi = pltpu.bitcast(xb[:, :half], jnp.uint32)
    lo = pltpu.bitcast(xb[:, half:], jnp.uint32)
    return pltpu.bitcast(hi | (lo >> 16), jnp.int32)


def _unpack_bf16_pairs(w):
    u = pltpu.bitcast(w, jnp.uint32)
    hi = pltpu.bitcast(u & jnp.uint32(0xFFFF0000), F32).astype(BF16)
    lo = pltpu.bitcast(u << 16, F32).astype(BF16)
    return jnp.concatenate([hi, lo], axis=1)


def _params(*sem):
    return pltpu.CompilerParams(dimension_semantics=sem, vmem_limit_bytes=VMEM_LIMIT)


def _ada_kernel(c_ref, w_ref, b_ref, o_ref):
    s = _silu(c_ref[...]).astype(BF16)
    o_ref[...] = jnp.dot(s, w_ref[...].astype(BF16), preferred_element_type=F32) + b_ref[...]


def _ada_call(cvecs, w_ada, b_ada):
    return pl.pallas_call(
        _ada_kernel,
        grid=(DEPTH, N_MOD),
        in_specs=[
            pl.BlockSpec((MOD_ROWS, D_MODEL), lambda l, n: (0, 0)),
            pl.BlockSpec((None, D_MODEL, D_MODEL), lambda l, n: (l, 0, n)),
            pl.BlockSpec((None, 1, D_MODEL), lambda l, n: (l, 0, n)),
        ],
        out_specs=pl.BlockSpec((None, MOD_ROWS, D_MODEL), lambda l, n: (l, 0, n)),
        out_shape=jax.ShapeDtypeStruct((DEPTH, MOD_ROWS, N_MOD * D_MODEL), F32),
        compiler_params=_params("parallel", "parallel"),
        name="ada",
    )(cvecs, w_ada, b_ada.reshape(DEPTH, 1, N_MOD * D_MODEL))


W_IN_Q = 0
W_IN_KV = W_IN_Q + Q_LORA
W_IN_A = W_IN_KV + KV_LORA
W_IN_G = W_IN_A + C_CONV
W_IN_KR = W_IN_G + C_CONV
W_IN_COLS = W_IN_KR + HEAD_BLOCK


def _swap_halves(x):
    lane = lax.broadcasted_iota(jnp.int32, x.shape, 1)
    first = lane % (2 * ROPE_PAIRS) < ROPE_PAIRS
    from_right = pltpu.roll(x, HEAD_BLOCK - ROPE_PAIRS, 1)
    from_left = pltpu.roll(x, ROPE_PAIRS, 1)
    return jnp.where(first, from_right, from_left)


def _premix_kernel(rope, resid, x_ref, *rest):
    if resid:
        moe_ref, modprev_ref, *rest = rest
    mod_ref, n1g_ref, win_ref, qg_ref, wuq_ref, kvg_ref, wuk_ref, wuv_ref, *rest = rest
    if rope:
        cos_ref, sin_ref, *rest = rest
    q_ref, k_ref, v_ref, z_ref, *rest = rest
    if resid:
        x2_ref, *rest = rest
    if not rope:
        ckv_ref, kr_ref = rest
    x = x_ref[0]
    if resid:
        x = x + modprev_ref[0][5:6] * moe_ref[0]
        x2_ref[0] = x
    mod = mod_ref[0]
    sh1, sc1 = mod[0:1], mod[1:2]
    h = (_rms(x, n1g_ref[...]) * (1.0 + sc1) + sh1).astype(BF16)
    proj = jnp.dot(h, win_ref[...], preferred_element_type=F32)
    q_c = proj[:, W_IN_Q:W_IN_KV]
    kv_c = proj[:, W_IN_KV:W_IN_A]
    a = proj[:, W_IN_A:W_IN_G]
    gate = proj[:, W_IN_G:W_IN_KR]
    kr = proj[:, W_IN_KR:W_IN_COLS]

    z_ref[0] = a * jax.nn.sigmoid(gate)

    qn = _rms(q_c, qg_ref[...]).astype(BF16)
    q = jnp.dot(qn, wuq_ref[...], preferred_element_type=F32)
    ckv = _rms(kv_c, kvg_ref[...])
    ckv_b = ckv.astype(BF16)
    kn = jnp.dot(ckv_b, wuk_ref[...], preferred_element_type=F32)
    v = jnp.dot(ckv_b, wuv_ref[...], preferred_element_type=F32)

    if rope:
        v = v + _ones_lane(v.shape[1])
        cos = cos_ref[...]
        sin = sin_ref[...]
        kr = kr * cos + _swap_halves(kr) * sin
    else:
        ckv_ref[0] = ckv
        kr_ref[0] = kr[:, :QK_ROPE]
    v_ref[0] = v.astype(BF16)

    for hd in range(N_HEADS):
        sl = slice(hd * HEAD_BLOCK, (hd + 1) * HEAD_BLOCK)
        qh = q[:, sl]
        if rope:
            qh = qh * cos + _swap_halves(qh) * sin
        q_ref[0, :, sl] = (qh * (ATTN_SCALE * LOG2E)).astype(BF16)
        k_ref[0, :, sl] = (kn[:, sl] + kr).astype(BF16)


def _premix_call(x, mod, lw, tm, rope_tabs=None, moe=None, mod_prev=None):
    b, t, _ = x.shape
    rope = rope_tabs is not None
    resid = moe is not None
    nt = t // tm
    per_b = (lambda i, j: (i, 0, 0)) if mod.shape[0] > 1 else (lambda i, j: (0, 0, 0))
    const2 = lambda i, j: (0, 0)
    tok = lambda i, j: (i, j, 0)
    v_cols = N_HEADS * (HEAD_BLOCK if rope else V_HEAD)
    in_specs = [pl.BlockSpec((1, tm, D_MODEL), tok)]
    args = [x]
    if resid:
        in_specs += [pl.BlockSpec((1, tm, D_MODEL), tok), pl.BlockSpec((1, N_MOD, D_MODEL), per_b)]
        args += [moe, mod_prev]
    in_specs += [
        pl.BlockSpec((1, N_MOD, D_MODEL), per_b),
        pl.BlockSpec((1, D_MODEL), const2),
        pl.BlockSpec((D_MODEL, W_IN_COLS), const2),
        pl.BlockSpec((1, Q_LORA), const2),
        pl.BlockSpec((Q_LORA, N_HEADS * HEAD_BLOCK), const2),
        pl.BlockSpec((1, KV_LORA), const2),
        pl.BlockSpec((KV_LORA, N_HEADS * HEAD_BLOCK), const2),
        pl.BlockSpec((KV_LORA, v_cols), const2),
    ]
    args += [mod, lw["n1g"], lw["win"], lw["qg"], lw["wuq"], lw["kvg"], lw["wuk"],
             lw["wuv_ones"] if rope else lw["wuv"]]
    out_specs = [
        pl.BlockSpec((1, tm, N_HEADS * HEAD_BLOCK), tok),
        pl.BlockSpec((1, tm, N_HEADS * HEAD_BLOCK), tok),
        pl.BlockSpec((1, tm, v_cols), tok),
        pl.BlockSpec((1, tm, C_CONV), tok),
    ]
    out_shape = [
        jax.ShapeDtypeStruct((b, t, N_HEADS * HEAD_BLOCK), BF16),
        jax.ShapeDtypeStruct((b, t, N_HEADS * HEAD_BLOCK), BF16),
        jax.ShapeDtypeStruct((b, t, v_cols), BF16),
        jax.ShapeDtypeStruct((b, t, C_CONV), F32),
    ]
    if resid:
        out_specs.append(pl.BlockSpec((1, tm, D_MODEL), tok))
        out_shape.append(jax.ShapeDtypeStruct((b, t, D_MODEL), F32))
    if rope:
        in_specs += [
            pl.BlockSpec((tm, HEAD_BLOCK), lambda i, j: (j, 0)),
            pl.BlockSpec((tm, HEAD_BLOCK), lambda i, j: (j, 0)),
        ]
        args += [rope_tabs[0], rope_tabs[1]]
    else:
        out_specs += [
            pl.BlockSpec((1, tm, KV_LORA), tok),
            pl.BlockSpec((1, tm, QK_ROPE), tok),
        ]
        out_shape += [
            jax.ShapeDtypeStruct((b, t, KV_LORA), F32),
            jax.ShapeDtypeStruct((b, t, QK_ROPE), F32),
        ]
    return pl.pallas_call(
        functools.partial(_premix_kernel, rope, resid),
        grid=(b, nt),
        in_specs=in_specs,
        out_specs=out_specs,
        out_shape=out_shape,
        compiler_params=_params("parallel", "parallel"),
        name="premix_rope" if rope else "premix",
    )(*args)


def _ctx_kernel(ckv_ref, kr_ref, wuk_ref, wuv_ref, k_ref, v_ref):
    ckv_b = ckv_ref[...].astype(BF16)
    kn = jnp.dot(ckv_b, wuk_ref[...], preferred_element_type=F32)
    v = jnp.dot(ckv_b, wuv_ref[...], preferred_element_type=F32)
    v_ref[0] = (v + _ones_lane(v.shape[1])).astype(BF16)
    kr = kr_ref[...]
    for hd in range(N_HEADS):
        sl = slice(hd * HEAD_BLOCK, (hd + 1) * HEAD_BLOCK)
        k_ref[0, :, sl] = (kn[:, sl] + kr).astype(BF16)


def _ctx_call(cache_ckv, cache_kr_pad, layer, lw):
    b, _, s, _ = cache_ckv.shape
    const2 = lambda i: (0, 0)
    return pl.pallas_call(
        _ctx_kernel,
        grid=(b,),
        in_specs=[
            pl.BlockSpec((None, None, s, KV_LORA), lambda i: (i, layer, 0, 0)),
            pl.BlockSpec((None, None, s, HEAD_BLOCK), lambda i: (i, layer, 0, 0)),
            pl.BlockSpec((KV_LORA, N_HEADS * HEAD_BLOCK), const2),
            pl.BlockSpec((KV_LORA, N_HEADS * HEAD_BLOCK), const2),
        ],
        out_specs=[
            pl.BlockSpec((1, s, N_HEADS * HEAD_BLOCK), lambda i: (i, 0, 0)),
            pl.BlockSpec((1, s, N_HEADS * HEAD_BLOCK), lambda i: (i, 0, 0)),
        ],
        out_shape=[
            jax.ShapeDtypeStruct((b, s, N_HEADS * HEAD_BLOCK), BF16),
            jax.ShapeDtypeStruct((b, s, N_HEADS * HEAD_BLOCK), BF16),
        ],
        compiler_params=_params("parallel"),
        name="ctx_kv",
    )(cache_ckv, cache_kr_pad, lw["wuk"], lw["wuv_ones"])


_NT = (((1,), (1,)), ((), ()))


def _attn_kernel(ctx, q_ref, k_ref, v_ref, *rest):
    if ctx:
        kc_ref, vc_ref, o_ref = rest
    else:
        (o_ref,) = rest
    outs = []
    for hd in range(N_HEADS):
        sl = slice(hd * HEAD_BLOCK, (hd + 1) * HEAD_BLOCK)
        qh = q_ref[0, :, sl]
        s = lax.dot_general(qh, k_ref[0, :, sl], _NT, preferred_element_type=F32)
        m = jnp.max(s, axis=-1, keepdims=True)
        if ctx:
            sc = lax.dot_general(qh, kc_ref[0, :, sl], _NT, preferred_element_type=F32)
            m = jnp.maximum(m, jnp.max(sc, axis=-1, keepdims=True))
        p = jnp.exp2(s - m)
        if ctx:
            pc = jnp.exp2(sc - m)
            o = jnp.dot(p.astype(BF16), v_ref[0, :, sl], preferred_element_type=F32)
            o = o + jnp.dot(pc.astype(BF16), vc_ref[0, :, sl], preferred_element_type=F32)
            outs.append(o[:, :V_HEAD] / o[:, V_HEAD:V_HEAD + 1])
        else:
            vs = slice(hd * V_HEAD, (hd + 1) * V_HEAD)
            o = jnp.dot(p.astype(BF16), v_ref[0, :, vs], preferred_element_type=F32)
            outs.append(o / jnp.sum(p, axis=-1, keepdims=True))
    o_ref[0] = jnp.concatenate(outs, axis=-1).astype(BF16)


def _attn_call(q, k, v, tq, ctx_kv=None):
    b, t, _ = q.shape
    ctx = ctx_kv is not None
    tok = lambda i, j: (i, j, 0)
    seq = lambda i, j: (i, 0, 0)
    in_specs = [
        pl.BlockSpec((1, tq, N_HEADS * HEAD_BLOCK), tok),
        pl.BlockSpec((1, t, N_HEADS * HEAD_BLOCK), seq),
        pl.BlockSpec((1, t, v.shape[2]), seq),
    ]
    args = [q, k, v]
    if ctx:
        s = ctx_kv[0].shape[1]
        in_specs += [
            pl.BlockSpec((1, s, N_HEADS * HEAD_BLOCK), seq),
            pl.BlockSpec((1, s, N_HEADS * HEAD_BLOCK), seq),
        ]
        args += list(ctx_kv)
    return pl.pallas_call(
        functools.partial(_attn_kernel, ctx),
        grid=(b, t // tq),
        in_specs=in_specs,
        out_specs=pl.BlockSpec((1, tq, N_HEADS * V_HEAD), tok),
        out_shape=jax.ShapeDtypeStruct((b, t, N_HEADS * V_HEAD), BF16),
        compiler_params=_params("parallel", "arbitrary"),
        name="attn_ctx" if ctx else "attn",
    )(*args)


CONV_ROWS = 64


def _route(aff, bias):
    sel = aff + bias
    rows = [sel[e:e + 1] for e in range(N_EXPERTS)]
    affr = [aff[e:e + 1] for e in range(N_EXPERTS)]
    scores = []
    for g in range(N_GROUPS):
        a, b, c, d = rows[g * EXPERTS_PER_GROUP:(g + 1) * EXPERTS_PER_GROUP]
        hi1, lo1 = jnp.maximum(a, b), jnp.minimum(a, b)
        hi2, lo2 = jnp.maximum(c, d), jnp.minimum(c, d)
        top1 = jnp.maximum(hi1, hi2)
        top2 = jnp.maximum(jnp.minimum(hi1, hi2), jnp.maximum(lo1, lo2))
        scores.append(top1 + top2)
    best = scores[0]
    gidx = jnp.zeros_like(best, dtype=jnp.int32)
    for g in range(1, N_GROUPS):
        better = scores[g] > best
        best = jnp.where(better, scores[g], best)
        gidx = jnp.where(better, g, gidx)
    gates = []
    e_lo = jnp.full_like(best, float(N_EXPERTS))
    e_hi = jnp.full_like(best, -1.0)
    for g in range(N_GROUPS):
        in_g = gidx == g
        s = rows[g * EXPERTS_PER_GROUP:(g + 1) * EXPERTS_PER_GROUP]
        af = affr[g * EXPERTS_PER_GROUP:(g + 1) * EXPERTS_PER_GROUP]
        picked = []
        for i in range(EXPERTS_PER_GROUP):
            rank = jnp.zeros_like(gidx)
            for j in range(EXPERTS_PER_GROUP):
                if j == i:
                    continue
                ahead = (s[j] >= s[i]) if j < i else (s[j] > s[i])
                rank = rank + ahead.astype(jnp.int32)
            chosen = (rank < 2) & in_g
            eid = float(g * EXPERTS_PER_GROUP + i)
            e_lo = jnp.where(chosen, jnp.minimum(e_lo, eid), e_lo)
            e_hi = jnp.where(chosen, jnp.maximum(e_hi, eid), e_hi)
            picked.append(jnp.where(chosen, af[i], 0.0))
        den = (picked[0] + picked[1]) + (picked[2] + picked[3])
        den = jnp.where(in_g, den, 1.0)
        gates += [pk / den for pk in picked]
    return jnp.concatenate(gates, axis=0), e_lo, e_hi


def _postmix_kernel(nt, x_ref, attn_ref, z_ref, mod_ref, cw_ref, cb_ref, lg_ref, lb_ref, wout_ref,
                    n2g_ref, rwt_ref, rb_ref, x1_ref, h2_ref, meta_ref, ids_ref, zp_ref, zs_ref, cz_ref):
    tm = x_ref.shape[1]
    j = pl.program_id(1)
    t0 = pl.multiple_of(j * tm, tm)

    zp_ref[HALO:HALO + tm, :] = z_ref[0, pl.ds(t0, tm), :]
    zeros = jnp.zeros((HALO, C_CONV), F32)

    @pl.when(j == 0)
    def _():
        zp_ref[0:HALO, :] = zeros

    @pl.when(j > 0)
    def _():
        zp_ref[0:HALO, :] = z_ref[0, pl.ds(t0 - HALO, HALO), :]

    @pl.when(j == nt - 1)
    def _():
        zp_ref[HALO + tm:, :] = zeros

    @pl.when(j < nt - 1)
    def _():
        zp_ref[HALO + tm:, :] = z_ref[0, pl.ds(t0 + tm, HALO), :]

    span = zs_ref.shape[1]
    for o in range(1, SUBLANES):
        zs_ref[o - 1] = zp_ref[o:o + span, :]

    cw = cw_ref[...]
    cb = cb_ref[...]
    for r0 in range(0, tm, CONV_ROWS):
        acc = jnp.broadcast_to(cb, (CONV_ROWS, C_CONV))
        for kk in range(CONV_K):
            off = HALO - CONV_PAD + kk
            o = off % SUBLANES
            base = r0 + off - o
            if o == 0:
                tap = zp_ref[base:base + CONV_ROWS, :]
            else:
                tap = zs_ref[o - 1, base:base + CONV_ROWS, :]
            acc = acc + tap * cw[kk:kk + 1, :]
        cz_ref[r0:r0 + CONV_ROWS, :] = acc

    cz = cz_ref[...]
    mu = jnp.mean(cz, axis=-1, keepdims=True)
    cen = cz - mu
    var = jnp.mean(cen * cen, axis=-1, keepdims=True)
    conv = _silu(cen * lax.rsqrt(var + EPS) * lg_ref[...] + lb_ref[...]).astype(BF16)

    n_attn = N_HEADS * V_HEAD
    y = jnp.dot(attn_ref[0], wout_ref[0:n_attn, :], preferred_element_type=F32)
    y = y + jnp.dot(conv, wout_ref[n_attn:, :], preferred_element_type=F32)

    mod = mod_ref[0]
    g1, sh2, sc2 = mod[2:3], mod[3:4], mod[4:5]
    x1 = x_ref[0] + g1 * y
    x1_ref[0] = x1
    h2 = _rms(x1, n2g_ref[...]) * (1.0 + sc2) + sh2
    h2_ref[0] = _pack_bf16_pairs(h2)

    logits = lax.dot_general(rwt_ref[...], h2, _NT, precision=lax.Precision.HIGHEST,
                             preferred_element_type=F32)
    gates, e_lo, e_hi = _route(jax.nn.sigmoid(logits), rb_ref[...])
    row = lax.broadcasted_iota(jnp.int32, (SUBLANES, tm), 0)
    ids = jnp.where(row == 0, e_lo, jnp.where(row == 1, e_hi, 0.0))
    ids_ref[...] = ids
    rec = jnp.concatenate([gates, ids, jnp.zeros((LANES - N_EXPERTS - SUBLANES, tm), F32)], axis=0)
    meta_ref[0] = rec.T


def _postmix_call(x, attn, z, mod, lw, rwt, rb, tm):
    b, t, _ = x.shape
    nt = t // tm
    per_b = (lambda i, j: (i, 0, 0)) if mod.shape[0] > 1 else (lambda i, j: (0, 0, 0))
    const2 = lambda i, j: (0, 0)
    tok = lambda i, j: (i, j, 0)
    return pl.pallas_call(
        functools.partial(_postmix_kernel, nt),
        grid=(b, nt),
        in_specs=[
            pl.BlockSpec((1, tm, D_MODEL), tok),
            pl.BlockSpec((1, tm, N_HEADS * V_HEAD), tok),
            pl.BlockSpec((1, t, C_CONV), lambda i, j: (i, 0, 0)),
            pl.BlockSpec((1, N_MOD, D_MODEL), per_b),
            pl.BlockSpec((CONV_K, C_CONV), const2),
            pl.BlockSpec((1, C_CONV), const2),
            pl.BlockSpec((1, C_CONV), const2),
            pl.BlockSpec((1, C_CONV), const2),
            pl.BlockSpec((D_MODEL, D_MODEL), const2),
            pl.BlockSpec((1, D_MODEL), const2),
            pl.BlockSpec((N_EXPERTS, D_MODEL), const2),
            pl.BlockSpec((N_EXPERTS, 1), const2),
        ],
        out_specs=[
            pl.BlockSpec((1, tm, D_MODEL), tok),
            pl.BlockSpec((1, tm, D_MODEL // 2), tok),
            pl.BlockSpec((1, tm, LANES), tok),
            pl.BlockSpec((SUBLANES, tm), lambda i, j: (0, i * nt + j)),
        ],
        out_shape=[
            jax.ShapeDtypeStruct((b, t, D_MODEL), F32),
            jax.ShapeDtypeStruct((b, t, D_MODEL // 2), jnp.int32),
            jax.ShapeDtypeStruct((b, t, LANES), F32),
            jax.ShapeDtypeStruct((SUBLANES, b * t), F32),
        ],
        scratch_shapes=[
            pltpu.VMEM((tm + 2 * HALO, C_CONV), F32),
            pltpu.VMEM((SUBLANES - 1, tm + 2 * HALO - SUBLANES, C_CONV), F32),
            pltpu.VMEM((tm, C_CONV), F32),
        ],
        compiler_params=_params("parallel", "arbitrary"),
        name="postmix",
    )(x, attn, z, mod, lw["conv_w"], lw["conv_b"], lw["ln_g"], lw["ln_b"], lw["wout"],
      lw["n2g"], rwt, rb)


N_PAIRS = EXPERTS_PER_GROUP * (EXPERTS_PER_GROUP - 1) // 2
N_CLASSES = N_GROUPS * N_PAIRS
_PAIR_LO = np.array([i for i in range(EXPERTS_PER_GROUP) for j in range(i + 1, EXPERTS_PER_GROUP)])
_PAIR_HI = np.array([j for i in range(EXPERTS_PER_GROUP) for j in range(i + 1, EXPERTS_PER_GROUP)])
MOE_ROWS = 256
SC_CORES = 2
SC_SUBCORES = 16
SC_WORKERS = SC_CORES * SC_SUBCORES
SC_WINDOW = 32


def _slot_rows(n):
    return n + N_CLASSES * MOE_ROWS


def _plan(ids, n_slots):
    e_lo = ids[0].astype(jnp.int32)
    e_hi = ids[1].astype(jnp.int32)
    grp = e_lo // EXPERTS_PER_GROUP
    i = e_lo % EXPERTS_PER_GROUP
    j = e_hi % EXPERTS_PER_GROUP
    cls = grp * N_PAIRS + (i * (2 * EXPERTS_PER_GROUP - 1 - i)) // 2 + (j - i - 1)
    onehot = (cls[None, :] == jnp.arange(N_CLASSES, dtype=jnp.int32)[:, None]).astype(jnp.int32)
    csum = jnp.cumsum(onehot, axis=1)
    counts = csum[:, -1]
    padded = (counts + MOE_ROWS - 1) // MOE_ROWS * MOE_ROWS
    pend = jnp.cumsum(padded)
    pstart = pend - padded
    pos = jnp.sum(onehot * (csum - 1 + pstart[:, None]), axis=0).astype(jnp.int32)
    tile_start = jnp.arange(n_slots // MOE_ROWS, dtype=jnp.int32) * MOE_ROWS
    tcls = jnp.minimum(jnp.sum(tile_start[:, None] >= pend[None, :], axis=1), N_CLASSES - 1)
    tgrp, tpair = tcls // N_PAIRS, tcls % N_PAIRS
    t_lo = tgrp * EXPERTS_PER_GROUP + jnp.asarray(_PAIR_LO, jnp.int32)[tpair]
    t_hi = tgrp * EXPERTS_PER_GROUP + jnp.asarray(_PAIR_HI, jnp.int32)[tpair]
    tile_experts = jnp.stack([t_lo, t_hi]).astype(jnp.int32)
    n_used = (pend[-1] // MOE_ROWS).astype(jnp.int32).reshape(1)
    return pos, tile_experts, n_used


def _sc_mesh():
    return plsc.VectorSubcoreMesh(core_axis_name="core", subcore_axis_name="subcore")


def _sc_worker():
    return lax.axis_index("core") * SC_SUBCORES + lax.axis_index("subcore")


def _dispatch_call(h2, meta, pos, n_slots):
    n = h2.shape[0]
    per_worker = n // SC_WORKERS

    @pl.kernel(
        out_type=[jax.ShapeDtypeStruct((n_slots, h2.shape[1]), h2.dtype),
                  jax.ShapeDtypeStruct((n_slots, LANES), F32)],
        mesh=_sc_mesh(),
        scratch_types=[pltpu.VMEM((per_worker,), jnp.int32),
                       pltpu.VMEM((SC_WINDOW, h2.shape[1]), h2.dtype),
                       pltpu.VMEM((SC_WINDOW, LANES), F32)],
    )
    def dispatch(h2_hbm, meta_hbm, pos_hbm, xs_hbm, ms_hbm, slot_v, row_v, rec_v):
        wid = _sc_worker()
        base = wid * per_worker
        pltpu.sync_copy(pos_hbm.at[wid], slot_v)

        @pl.loop(0, per_worker // SC_WINDOW)
        def _(step):
            rows = pl.ds(base + step * SC_WINDOW, SC_WINDOW)
            slots = slot_v.at[pl.ds(step * SC_WINDOW, SC_WINDOW)]
            pltpu.sync_copy(h2_hbm.at[rows], row_v)
            pltpu.sync_copy(row_v, xs_hbm.at[slots])
            pltpu.sync_copy(meta_hbm.at[rows], rec_v)
            pltpu.sync_copy(rec_v, ms_hbm.at[slots])

    return dispatch(h2, meta, pos.reshape(SC_WORKERS, per_worker))


def _combine_call(ys, pos):
    n = pos.shape[0]
    per_worker = n // SC_WORKERS

    @pl.kernel(
        out_type=jax.ShapeDtypeStruct((n, D_MODEL), F32),
        mesh=_sc_mesh(),
        scratch_types=[pltpu.VMEM((per_worker,), jnp.int32),
                       pltpu.VMEM((SC_WINDOW, D_MODEL), F32)],
    )
    def combine(ys_hbm, pos_hbm, out_hbm, slot_v, row_v):
        wid = _sc_worker()
        base = wid * per_worker
        pltpu.sync_copy(pos_hbm.at[wid], slot_v)

        @pl.loop(0, per_worker // SC_WINDOW)
        def _(step):
            slots = slot_v.at[pl.ds(step * SC_WINDOW, SC_WINDOW)]
            pltpu.sync_copy(ys_hbm.at[slots], row_v)
            pltpu.sync_copy(row_v, out_hbm.at[pl.ds(base + step * SC_WINDOW, SC_WINDOW)])

    return combine(ys, pos.reshape(SC_WORKERS, per_worker))


def _experts_kernel(te_ref, nu_ref, xs_ref, ms_ref, wg_ref, wu_ref, wd_ref, ys_ref):
    i = pl.program_id(0)

    @pl.when(i < nu_ref[0])
    def _():
        x = _unpack_bf16_pairs(xs_ref[...])
        rec = ms_ref[...]
        lane = lax.broadcasted_iota(jnp.int32, rec.shape, 1)

        def expert(e):
            gate = jnp.sum(jnp.where(lane == e, rec, 0.0), axis=-1, keepdims=True)
            hg = jnp.dot(x, wg_ref[e], preferred_element_type=F32)
            hu = jnp.dot(x, wu_ref[e], preferred_element_type=F32)
            hid = (_silu(hg) * hu * gate).astype(BF16)
            return jnp.dot(hid, wd_ref[e], preferred_element_type=F32)

        ys_ref[...] = expert(te_ref[0, i]) + expert(te_ref[1, i])

    @pl.when(i >= nu_ref[0])
    def _():
        ys_ref[...] = jnp.zeros_like(ys_ref)


def _experts_call(xs, ms, tile_experts, n_used, lw):
    n_slots = xs.shape[0]
    rows = lambda i, te, nu: (i, 0)
    resident = pl.BlockSpec(memory_space=pltpu.VMEM)
    return pl.pallas_call(
        _experts_kernel,
        grid_spec=pltpu.PrefetchScalarGridSpec(
            num_scalar_prefetch=2,
            grid=(n_slots // MOE_ROWS,),
            in_specs=[
                pl.BlockSpec((MOE_ROWS, D_MODEL // 2), rows),
                pl.BlockSpec((MOE_ROWS, LANES), rows),
                resident, resident, resident,
            ],
            out_specs=pl.BlockSpec((MOE_ROWS, D_MODEL), rows),
        ),
        out_shape=jax.ShapeDtypeStruct((n_slots, D_MODEL), F32),
        compiler_params=_params("arbitrary"),
        name="experts",
    )(tile_experts, n_used, xs, ms, lw["wg"], lw["wu"], lw["wd"])


def _routed_moe(h2, meta, ids, lw):
    b, t, _ = h2.shape
    n = b * t
    n_slots = _slot_rows(n)
    pos, tile_experts, n_used = _plan(ids, n_slots)
    xs, ms = _dispatch_call(h2.reshape(n, h2.shape[2]), meta.reshape(n, LANES), pos, n_slots)
    ys = _experts_call(xs, ms, tile_experts, n_used, lw)
    return _combine_call(ys, pos).reshape(b, t, D_MODEL)


def _final_kernel(x1_ref, moe_ref, mod_ref, fg_ref, o_ref):
    x2 = x1_ref[0] + mod_ref[0][5:6] * moe_ref[0]
    o_ref[0] = _rms(x2, fg_ref[...])


def _final_call(x1, moe, mod, final_g, tm):
    b, t, _ = x1.shape
    per_b = (lambda i, j: (i, 0, 0)) if mod.shape[0] > 1 else (lambda i, j: (0, 0, 0))
    tok = lambda i, j: (i, j, 0)
    return pl.pallas_call(
        _final_kernel,
        grid=(b, t // tm),
        in_specs=[
            pl.BlockSpec((1, tm, D_MODEL), tok),
            pl.BlockSpec((1, tm, D_MODEL), tok),
            pl.BlockSpec((1, N_MOD, D_MODEL), per_b),
            pl.BlockSpec((1, D_MODEL), lambda i, j: (0, 0)),
        ],
        out_specs=pl.BlockSpec((1, tm, D_MODEL), tok),
        out_shape=jax.ShapeDtypeStruct((b, t, D_MODEL), F32),
        compiler_params=_params("parallel", "parallel"),
        name="final_norm",
    )(x1, moe, mod, final_g)


def _prep_layer(l, norm1_g, w_in, q_norm_g, w_uq, kv_norm_g, w_ukv, conv_w, conv_b, conv_ln_g,
                conv_ln_b, w_out, norm2_g, w_gate, w_up, w_down):
    win = w_in[l]
    o_kv, o_kr, o_a, o_g = Q_LORA, Q_LORA + KV_LORA, Q_LORA + KV_LORA + QK_ROPE, Q_LORA + KV_LORA + QK_ROPE + C_CONV
    wkr = win[:, o_kr:o_a]
    pad_r = ((0, 0), (0, HEAD_BLOCK - QK_ROPE))
    win_cat = jnp.concatenate(
        [win[:, :o_kv], win[:, o_kv:o_kr], win[:, o_a:o_g], win[:, o_g:],
         jnp.pad(wkr, pad_r)], axis=1).astype(BF16)

    wuq = w_uq[l].reshape(Q_LORA, N_HEADS, QK_NOPE + QK_ROPE)
    nope, rope = wuq[..., :QK_NOPE], wuq[..., QK_NOPE:]
    zpad = jnp.zeros((Q_LORA, N_HEADS, HEAD_BLOCK - QK_NOPE - QK_ROPE), F32)
    wuq_p = jnp.concatenate([rope, nope, zpad], axis=-1).reshape(Q_LORA, -1).astype(BF16)

    wukv = w_ukv[l].reshape(KV_LORA, N_HEADS, QK_NOPE + V_HEAD)
    kz0 = jnp.zeros((KV_LORA, N_HEADS, QK_ROPE), F32)
    kz1 = jnp.zeros((KV_LORA, N_HEADS, HEAD_BLOCK - QK_NOPE - QK_ROPE), F32)
    wuk = jnp.concatenate([kz0, wukv[..., :QK_NOPE], kz1], axis=-1).reshape(KV_LORA, -1).astype(BF16)
    wuv = wukv[..., QK_NOPE:].reshape(KV_LORA, -1).astype(BF16)
    vz = jnp.zeros((KV_LORA, N_HEADS, HEAD_BLOCK - V_HEAD), F32)
    wuv_ones = jnp.concatenate([wukv[..., QK_NOPE:], vz], axis=-1).reshape(KV_LORA, -1).astype(BF16)

    return {
        "n1g": norm1_g[l][None], "win": win_cat, "qg": q_norm_g[l][None], "wuq": wuq_p,
        "kvg": kv_norm_g[l][None], "wuk": wuk, "wuv": wuv, "wuv_ones": wuv_ones,
        "conv_w": conv_w[l], "conv_b": conv_b[l][None], "ln_g": conv_ln_g[l][None],
        "ln_b": conv_ln_b[l][None], "wout": w_out[l].astype(BF16), "n2g": norm2_g[l][None],
        "wg": w_gate[l].astype(BF16), "wu": w_up[l].astype(BF16), "wd": w_down[l].astype(BF16),
    }


def _rope_tables(n_tokens):
    rows = n_tokens // GRID_W
    row = jnp.repeat(jnp.arange(rows), GRID_W).astype(F32)
    col = jnp.tile(jnp.arange(GRID_W), rows).astype(F32)
    freqs = ROPE_BASE ** (-jnp.arange(ROPE_PAIRS, dtype=F32) / ROPE_PAIRS)
    ar, ac = row[:, None] * freqs, col[:, None] * freqs
    cr, sr, cc, sc = jnp.cos(ar), jnp.sin(ar), jnp.cos(ac), jnp.sin(ac)
    rest = HEAD_BLOCK - QK_ROPE
    cos = jnp.concatenate([cr, cr, cc, cc, jnp.ones((n_tokens, rest), F32)], axis=1)
    sin = jnp.concatenate([-sr, sr, -sc, sc, jnp.zeros((n_tokens, rest), F32)], axis=1)
    return cos, sin


TM_PROMPT = 256
TM_SAMPLE = 512
TQ_SAMPLE = 512


def kernel(x_prompt, x_sample, cache_ckv, cache_krope, c, c_ctx, w_ada, b_ada, norm1_g, w_in,
           q_norm_g, w_uq, kv_norm_g, w_ukv, conv_w, conv_b, conv_ln_g, conv_ln_b, w_out,
           norm2_g, router_w, router_b, w_gate, w_up, w_down, final_g):
    dec_b = x_sample.shape[0]
    cvecs = jnp.concatenate(
        [c_ctx[None], c, jnp.zeros((MOD_ROWS - 1 - dec_b, D_MODEL), F32)], axis=0)
    mod_all = _ada_call(cvecs, w_ada, b_ada).reshape(DEPTH, MOD_ROWS, N_MOD, D_MODEL)

    rwt = router_w.T
    rb = router_b[:, None]
    fg = final_g[None]
    rope_tabs = _rope_tables(x_sample.shape[1])
    cache_kr_pad = jnp.pad(cache_krope, ((0, 0), (0, 0), (0, 0), (0, HEAD_BLOCK - QK_ROPE)))

    xp, xs = x_prompt, x_sample
    moe_p = moe_s = mod_p = mod_s = None
    ckvs, krs = [], []
    for l in range(DEPTH):
        lw = _prep_layer(l, norm1_g, w_in, q_norm_g, w_uq, kv_norm_g, w_ukv, conv_w, conv_b,
                         conv_ln_g, conv_ln_b, w_out, norm2_g, w_gate, w_up, w_down)
        prev_p, prev_s = mod_p, mod_s
        mod_p = mod_all[l, 0:1]
        mod_s = mod_all[l, 1:1 + dec_b]

        outs = _premix_call(xp, mod_p, lw, TM_PROMPT, moe=moe_p, mod_prev=prev_p)
        if l > 0:
            q, k, v, z, xp, ckv, kr = outs
        else:
            q, k, v, z, ckv, kr = outs
        ckvs.append(ckv)
        krs.append(kr)
        attn = _attn_call(q, k, v, xp.shape[1])
        xp, h2, meta, ids = _postmix_call(xp, attn, z, mod_p, lw, rwt, rb, TM_PROMPT)
        moe_p = _routed_moe(h2, meta, ids, lw)

        outs = _premix_call(xs, mod_s, lw, TM_SAMPLE, rope_tabs, moe=moe_s, mod_prev=prev_s)
        if l > 0:
            q, k, v, z, xs = outs
        else:
            q, k, v, z = outs
        ctx_kv = _ctx_call(cache_ckv, cache_kr_pad, l, lw)
        attn = _attn_call(q, k, v, TQ_SAMPLE, ctx_kv)
        xs, h2, meta, ids = _postmix_call(xs, attn, z, mod_s, lw, rwt, rb, TM_SAMPLE)
        moe_s = _routed_moe(h2, meta, ids, lw)

    y_prompt = _final_call(xp, moe_p, mod_p, fg, TM_PROMPT)
    y_sample = _final_call(xs, moe_s, mod_s, fg, TM_SAMPLE)
    return y_prompt, y_sample, jnp.stack(ckvs, axis=1), jnp.stack(krs, axis=1)
```

```python
import functools

import jax
import jax.numpy as jnp
import numpy as np
from jax import lax
from jax.experimental import pallas as pl
from jax.experimental.pallas import tpu as pltpu
from jax.experimental.pallas import tpu_sc as plsc

D_MODEL = 1024
DEPTH = 4
GRID_W = 64
N_HEADS = 8
QK_NOPE = 64
QK_ROPE = 32
V_HEAD = 64
Q_LORA = 384
KV_LORA = 256
C_CONV = 512
CONV_K = 31
CONV_PAD = CONV_K // 2
N_EXPERTS = 16
N_GROUPS = 4
EXPERTS_PER_GROUP = N_EXPERTS // N_GROUPS
D_EXPERT = 256
ROPE_BASE = 10000.0
ROPE_PAIRS = QK_ROPE // 4
EPS = 1e-6
ATTN_SCALE = (QK_NOPE + QK_ROPE) ** -0.5
LOG2E = 1.4426950408889634

LANES = 128
SUBLANES = 8
HEAD_BLOCK = LANES
N_MOD = 6
MOD_ROWS = 16
HALO = 16
VMEM_LIMIT = 48 * 1024 * 1024

BF16 = jnp.bfloat16
F32 = jnp.float32


def _rms(x, g):
    return x * lax.rsqrt(jnp.mean(x * x, axis=-1, keepdims=True) + EPS) * g


def _silu(x):
    return x * jax.nn.sigmoid(x)


def _ones_lane(width):
    lane = lax.broadcasted_iota(jnp.int32, (1, width), 1)
    return jnp.where(lane % HEAD_BLOCK == V_HEAD, 1.0, 0.0)


def _pack_bf16_pairs(x):
    half = x.shape[1] // 2
    xb = x.astype(BF16).astype(F32)
    hi = pltpu.bitcast(xb[:, :half], jnp.uint32)
    lo = pltpu.bitcast(xb[:, half:], jnp.uint32)
    return pltpu.bitcast(hi | (lo >> 16), jnp.int32)


def _unpack_bf16_pairs(w):
    u = pltpu.bitcast(w, jnp.uint32)
    hi = pltpu.bitcast(u & jnp.uint32(0xFFFF0000), F32).astype(BF16)
    lo = pltpu.bitcast(u << 16, F32).astype(BF16)
    return jnp.concatenate([hi, lo], axis=1)


def _params(*sem):
    return pltpu.CompilerParams(dimension_semantics=sem, vmem_limit_bytes=VMEM_LIMIT)


def _ada_kernel(c_ref, w_ref, b_ref, o_ref):
    s = _silu(c_ref[...]).astype(BF16)
    o_ref[...] = jnp.dot(s, w_ref[...].astype(BF16), preferred_element_type=F32) + b_ref[...]


def _ada_call(cvecs, w_ada, b_ada):
    return pl.pallas_call(
        _ada_kernel,
        grid=(DEPTH, N_MOD),
        in_specs=[
            pl.BlockSpec((MOD_ROWS, D_MODEL), lambda l, n: (0, 0)),
            pl.BlockSpec((None, D_MODEL, D_MODEL), lambda l, n: (l, 0, n)),
            pl.BlockSpec((None, 1, D_MODEL), lambda l, n: (l, 0, n)),
        ],
        out_specs=pl.BlockSpec((None, MOD_ROWS, D_MODEL), lambda l, n: (l, 0, n)),
        out_shape=jax.ShapeDtypeStruct((DEPTH, MOD_ROWS, N_MOD * D_MODEL), F32),
        compiler_params=_params("parallel", "parallel"),
        name="ada",
    )(cvecs, w_ada, b_ada.reshape(DEPTH, 1, N_MOD * D_MODEL))


W_IN_Q = 0
W_IN_KV = W_IN_Q + Q_LORA
W_IN_A = W_IN_KV + KV_LORA
W_IN_G = W_IN_A + C_CONV
W_IN_KR = W_IN_G + C_CONV
W_IN_COLS = W_IN_KR + HEAD_BLOCK


def _swap_halves(x):
    lane = lax.broadcasted_iota(jnp.int32, x.shape, 1)
    first = lane % (2 * ROPE_PAIRS) < ROPE_PAIRS
    from_right = pltpu.roll(x, HEAD_BLOCK - ROPE_PAIRS, 1)
    from_left = pltpu.roll(x, ROPE_PAIRS, 1)
    return jnp.where(first, from_right, from_left)


def _premix_kernel(rope, resid, x_ref, *rest):
    if resid:
        moe_ref, modprev_ref, *rest = rest
    mod_ref, n1g_ref, win_ref, qg_ref, wuq_ref, kvg_ref, wuk_ref, wuv_ref, *rest = rest
    if rope:
        cos_ref, sin_ref, *rest = rest
    q_ref, k_ref, v_ref, z_ref, *rest = rest
    if resid:
        x2_ref, *rest = rest
    if not rope:
        ckv_ref, kr_ref = rest
    x = x_ref[0]
    if resid:
        x = x + modprev_ref[0][5:6] * moe_ref[0]
        x2_ref[0] = x
    mod = mod_ref[0]
    sh1, sc1 = mod[0:1], mod[1:2]
    h = (_rms(x, n1g_ref[...]) * (1.0 + sc1) + sh1).astype(BF16)
    proj = jnp.dot(h, win_ref[...], preferred_element_type=F32)
    q_c = proj[:, W_IN_Q:W_IN_KV]
    kv_c = proj[:, W_IN_KV:W_IN_A]
    a = proj[:, W_IN_A:W_IN_G]
    gate = proj[:, W_IN_G:W_IN_KR]
    kr = proj[:, W_IN_KR:W_IN_COLS]

    z_ref[0] = a * jax.nn.sigmoid(gate)

    qn = _rms(q_c, qg_ref[...]).astype(BF16)
    q = jnp.dot(qn, wuq_ref[...], preferred_element_type=F32)
    ckv = _rms(kv_c, kvg_ref[...])
    ckv_b = ckv.astype(BF16)
    kn = jnp.dot(ckv_b, wuk_ref[...], preferred_element_type=F32)
    v = jnp.dot(ckv_b, wuv_ref[...], preferred_element_type=F32)

    if rope:
        v = v + _ones_lane(v.shape[1])
        cos = cos_ref[...]
        sin = sin_ref[...]
        kr = kr * cos + _swap_halves(kr) * sin
    else:
        ckv_ref[0] = ckv
        kr_ref[0] = kr[:, :QK_ROPE]
    v_ref[0] = v.astype(BF16)

    for hd in range(N_HEADS):
        sl = slice(hd * HEAD_BLOCK, (hd + 1) * HEAD_BLOCK)
        qh = q[:, sl]
        if rope:
            qh = qh * cos + _swap_halves(qh) * sin
        q_ref[0, :, sl] = (qh * (ATTN_SCALE * LOG2E)).astype(BF16)
        k_ref[0, :, sl] = (kn[:, sl] + kr).astype(BF16)


def _premix_call(x, mod, lw, tm, rope_tabs=None, moe=None, mod_prev=None):
    b, t, _ = x.shape
    rope = rope_tabs is not None
    resid = moe is not None
    nt = t // tm
    per_b = (lambda i, j: (i, 0, 0)) if mod.shape[0] > 1 else (lambda i, j: (0, 0, 0))
    const2 = lambda i, j: (0, 0)
    tok = lambda i, j: (i, j, 0)
    v_cols = N_HEADS * (HEAD_BLOCK if rope else V_HEAD)
    in_specs = [pl.BlockSpec((1, tm, D_MODEL), tok)]
    args = [x]
    if resid:
        in_specs += [pl.BlockSpec((1, tm, D_MODEL), tok), pl.BlockSpec((1, N_MOD, D_MODEL), per_b)]
        args += [moe, mod_prev]
    in_specs += [
        pl.BlockSpec((1, N_MOD, D_MODEL), per_b),
        pl.BlockSpec((1, D_MODEL), const2),
        pl.BlockSpec((D_MODEL, W_IN_COLS), const2),
        pl.BlockSpec((1, Q_LORA), const2),
        pl.BlockSpec((Q_LORA, N_HEADS * HEAD_BLOCK), const2),
        pl.BlockSpec((1, KV_LORA), const2),
        pl.BlockSpec((KV_LORA, N_HEADS * HEAD_BLOCK), const2),
        pl.BlockSpec((KV_LORA, v_cols), const2),
    ]
    args += [mod, lw["n1g"], lw["win"], lw["qg"], lw["wuq"], lw["kvg"], lw["wuk"],
             lw["wuv_ones"] if rope else lw["wuv"]]
    out_specs = [
        pl.BlockSpec((1, tm, N_HEADS * HEAD_BLOCK), tok),
        pl.BlockSpec((1, tm, N_HEADS * HEAD_BLOCK), tok),
        pl.BlockSpec((1, tm, v_cols), tok),
        pl.BlockSpec((1, tm, C_CONV), tok),
    ]
    out_shape = [
        jax.ShapeDtypeStruct((b, t, N_HEADS * HEAD_BLOCK), BF16),
        jax.ShapeDtypeStruct((b, t, N_HEADS * HEAD_BLOCK), BF16),
        jax.ShapeDtypeStruct((b, t, v_cols), BF16),
        jax.ShapeDtypeStruct((b, t, C_CONV), F32),
    ]
    if resid:
        out_specs.append(pl.BlockSpec((1, tm, D_MODEL), tok))
        out_shape.append(jax.ShapeDtypeStruct((b, t, D_MODEL), F32))
    if rope:
        in_specs += [
            pl.BlockSpec((tm, HEAD_BLOCK), lambda i, j: (j, 0)),
            pl.BlockSpec((tm, HEAD_BLOCK), lambda i, j: (j, 0)),
        ]
        args += [rope_tabs[0], rope_tabs[1]]
    else:
        out_specs += [
            pl.BlockSpec((1, tm, KV_LORA), tok),
            pl.BlockSpec((1, tm, QK_ROPE), tok),
        ]
        out_shape += [
            jax.ShapeDtypeStruct((b, t, KV_LORA), F32),
            jax.ShapeDtypeStruct((b, t, QK_ROPE), F32),
        ]
    return pl.pallas_call(
        functools.partial(_premix_kernel, rope, resid),
        grid=(b, nt),
        in_specs=in_specs,
        out_specs=out_specs,
        out_shape=out_shape,
        compiler_params=_params("parallel", "parallel"),
        name="premix_rope" if rope else "premix",
    )(*args)


def _ctx_kernel(ckv_ref, kr_ref, wuk_ref, wuv_ref, k_ref, v_ref):
    ckv_b = ckv_ref[...].astype(BF16)
    kn = jnp.dot(ckv_b, wuk_ref[...], preferred_element_type=F32)
    v = jnp.dot(ckv_b, wuv_ref[...], preferred_element_type=F32)
    v_ref[0] = (v + _ones_lane(v.shape[1])).astype(BF16)
    kr = kr_ref[...]
    for hd in range(N_HEADS):
        sl = slice(hd * HEAD_BLOCK, (hd + 1) * HEAD_BLOCK)
        k_ref[0, :, sl] = (kn[:, sl] + kr).astype(BF16)


def _ctx_call(cache_ckv, cache_kr_pad, layer, lw):
    b, _, s, _ = cache_ckv.shape
    const2 = lambda i: (0, 0)
    return pl.pallas_call(
        _ctx_kernel,
        grid=(b,),
        in_specs=[
            pl.BlockSpec((None, None, s, KV_LORA), lambda i: (i, layer, 0, 0)),
            pl.BlockSpec((None, None, s, HEAD_BLOCK), lambda i: (i, layer, 0, 0)),
            pl.BlockSpec((KV_LORA, N_HEADS * HEAD_BLOCK), const2),
            pl.BlockSpec((KV_LORA, N_HEADS * HEAD_BLOCK), const2),
        ],
        out_specs=[
            pl.BlockSpec((1, s, N_HEADS * HEAD_BLOCK), lambda i: (i, 0, 0)),
            pl.BlockSpec((1, s, N_HEADS * HEAD_BLOCK), lambda i: (i, 0, 0)),
        ],
        out_shape=[
            jax.ShapeDtypeStruct((b, s, N_HEADS * HEAD_BLOCK), BF16),
            jax.ShapeDtypeStruct((b, s, N_HEADS * HEAD_BLOCK), BF16),
        ],
        compiler_params=_params("parallel"),
        name="ctx_kv",
    )(cache_ckv, cache_kr_pad, lw["wuk"], lw["wuv_ones"])


_NT = (((1,), (1,)), ((), ()))


def _attn_kernel(ctx, q_ref, k_ref, v_ref, *rest):
    if ctx:
        kc_ref, vc_ref, o_ref = rest
    else:
        (o_ref,) = rest
    outs = []
    for hd in range(N_HEADS):
        sl = slice(hd * HEAD_BLOCK, (hd + 1) * HEAD_BLOCK)
        qh = q_ref[0, :, sl]
        s = lax.dot_general(qh, k_ref[0, :, sl], _NT, preferred_element_type=F32)
        m = jnp.max(s, axis=-1, keepdims=True)
        if ctx:
            sc = lax.dot_general(qh, kc_ref[0, :, sl], _NT, preferred_element_type=F32)
            m = jnp.maximum(m, jnp.max(sc, axis=-1, keepdims=True))
        p = jnp.exp2(s - m)
        if ctx:
            pc = jnp.exp2(sc - m)
            o = jnp.dot(p.astype(BF16), v_ref[0, :, sl], preferred_element_type=F32)
            o = o + jnp.dot(pc.astype(BF16), vc_ref[0, :, sl], preferred_element_type=F32)
            outs.append(o[:, :V_HEAD] / o[:, V_HEAD:V_HEAD + 1])
        else:
            vs = slice(hd * V_HEAD, (hd + 1) * V_HEAD)
            o = jnp.dot(p.astype(BF16), v_ref[0, :, vs], preferred_element_type=F32)
            outs.append(o / jnp.sum(p, axis=-1, keepdims=True))
    o_ref[0] = jnp.concatenate(outs, axis=-1).astype(BF16)


def _attn_call(q, k, v, tq, ctx_kv=None):
    b, t, _ = q.shape
    ctx = ctx_kv is not None
    tok = lambda i, j: (i, j, 0)
    seq = lambda i, j: (i, 0, 0)
    in_specs = [
        pl.BlockSpec((1, tq, N_HEADS * HEAD_BLOCK), tok),
        pl.BlockSpec((1, t, N_HEADS * HEAD_BLOCK), seq),
        pl.BlockSpec((1, t, v.shape[2]), seq),
    ]
    args = [q, k, v]
    if ctx:
        s = ctx_kv[0].shape[1]
        in_specs += [
            pl.BlockSpec((1, s, N_HEADS * HEAD_BLOCK), seq),
            pl.BlockSpec((1, s, N_HEADS * HEAD_BLOCK), seq),
        ]
        args += list(ctx_kv)
    return pl.pallas_call(
        functools.partial(_attn_kernel, ctx),
        grid=(b, t // tq),
        in_specs=in_specs,
        out_specs=pl.BlockSpec((1, tq, N_HEADS * V_HEAD), tok),
        out_shape=jax.ShapeDtypeStruct((b, t, N_HEADS * V_HEAD), BF16),
        compiler_params=_params("parallel", "arbitrary"),
        name="attn_ctx" if ctx else "attn",
    )(*args)


CONV_ROWS = 64


def _route(aff, bias):
    sel = aff + bias
    rows = [sel[e:e + 1] for e in range(N_EXPERTS)]
    affr = [aff[e:e + 1] for e in range(N_EXPERTS)]
    scores = []
    for g in range(N_GROUPS):
        a, b, c, d = rows[g * EXPERTS_PER_GROUP:(g + 1) * EXPERTS_PER_GROUP]
        hi1, lo1 = jnp.maximum(a, b), jnp.minimum(a, b)
        hi2, lo2 = jnp.maximum(c, d), jnp.minimum(c, d)
        top1 = jnp.maximum(hi1, hi2)
        top2 = jnp.maximum(jnp.minimum(hi1, hi2), jnp.maximum(lo1, lo2))
        scores.append(top1 + top2)
    best = scores[0]
    gidx = jnp.zeros_like(best, dtype=jnp.int32)
    for g in range(1, N_GROUPS):
        better = scores[g] > best
        best = jnp.where(better, scores[g], best)
        gidx = jnp.where(better, g, gidx)
    gates = []
    e_lo = jnp.full_like(best, float(N_EXPERTS))
    e_hi = jnp.full_like(best, -1.0)
    for g in range(N_GROUPS):
        in_g = gidx == g
        s = rows[g * EXPERTS_PER_GROUP:(g + 1) * EXPERTS_PER_GROUP]
        af = affr[g * EXPERTS_PER_GROUP:(g + 1) * EXPERTS_PER_GROUP]
        picked = []
        for i in range(EXPERTS_PER_GROUP):
            rank = jnp.zeros_like(gidx)
            for j in range(EXPERTS_PER_GROUP):
                if j == i:
                    continue
                ahead = (s[j] >= s[i]) if j < i else (s[j] > s[i])
                rank = rank + ahead.astype(jnp.int32)
            chosen = (rank < 2) & in_g
            eid = float(g * EXPERTS_PER_GROUP + i)
            e_lo = jnp.where(chosen, jnp.minimum(e_lo, eid), e_lo)
            e_hi = jnp.where(chosen, jnp.maximum(e_hi, eid), e_hi)
            picked.append(jnp.where(chosen, af[i], 0.0))
        den = (picked[0] + picked[1]) + (picked[2] + picked[3])
        den = jnp.where(in_g, den, 1.0)
        gates += [pk / den for pk in picked]
    return jnp.concatenate(gates, axis=0), e_lo, e_hi


def _postmix_kernel(nt, x_ref, attn_ref, z_ref, mod_ref, cw_ref, cb_ref, lg_ref, lb_ref, wout_ref,
                    n2g_ref, rwt_ref, rb_ref, x1_ref, h2_ref, meta_ref, ids_ref, zp_ref, zs_ref, cz_ref):
    tm = x_ref.shape[1]
    j = pl.program_id(1)
    t0 = pl.multiple_of(j * tm, tm)

    zp_ref[HALO:HALO + tm, :] = z_ref[0, pl.ds(t0, tm), :]
    zeros = jnp.zeros((HALO, C_CONV), F32)

    @pl.when(j == 0)
    def _():
        zp_ref[0:HALO, :] = zeros

    @pl.when(j > 0)
    def _():
        zp_ref[0:HALO, :] = z_ref[0, pl.ds(t0 - HALO, HALO), :]

    @pl.when(j == nt - 1)
    def _():
        zp_ref[HALO + tm:, :] = zeros

    @pl.when(j < nt - 1)
    def _():
        zp_ref[HALO + tm:, :] = z_ref[0, pl.ds(t0 + tm, HALO), :]

    span = zs_ref.shape[1]
    for o in range(1, SUBLANES):
        zs_ref[o - 1] = zp_ref[o:o + span, :]

    cw = cw_ref[...]
    cb = cb_ref[...]
    for r0 in range(0, tm, CONV_ROWS):
        acc = jnp.broadcast_to(cb, (CONV_ROWS, C_CONV))
        for kk in range(CONV_K):
            off = HALO - CONV_PAD + kk
            o = off % SUBLANES
            base = r0 + off - o
            if o == 0:
                tap = zp_ref[base:base + CONV_ROWS, :]
            else:
                tap = zs_ref[o - 1, base:base + CONV_ROWS, :]
            acc = acc + tap * cw[kk:kk + 1, :]
        cz_ref[r0:r0 + CONV_ROWS, :] = acc

    cz = cz_ref[...]
    mu = jnp.mean(cz, axis=-1, keepdims=True)
    cen = cz - mu
    var = jnp.mean(cen * cen, axis=-1, keepdims=True)
    conv = _silu(cen * lax.rsqrt(var + EPS) * lg_ref[...] + lb_ref[...]).astype(BF16)

    n_attn = N_HEADS * V_HEAD
    y = jnp.dot(attn_ref[0], wout_ref[0:n_attn, :], preferred_element_type=F32)
    y = y + jnp.dot(conv, wout_ref[n_attn:, :], preferred_element_type=F32)

    mod = mod_ref[0]
    g1, sh2, sc2 = mod[2:3], mod[3:4], mod[4:5]
    x1 = x_ref[0] + g1 * y
    x1_ref[0] = x1
    h2 = _rms(x1, n2g_ref[...]) * (1.0 + sc2) + sh2
    h2_ref[0] = _pack_bf16_pairs(h2)

    logits = lax.dot_general(rwt_ref[...], h2, _NT, precision=lax.Precision.HIGHEST,
                             preferred_element_type=F32)
    gates, e_lo, e_hi = _route(jax.nn.sigmoid(logits), rb_ref[...])
    row = lax.broadcasted_iota(jnp.int32, (SUBLANES, tm), 0)
    ids = jnp.where(row == 0, e_lo, jnp.where(row == 1, e_hi, 0.0))
    ids_ref[...] = ids
    rec = jnp.concatenate([gates, ids, jnp.zeros((LANES - N_EXPERTS - SUBLANES, tm), F32)], axis=0)
    meta_ref[0] = rec.T


def _postmix_call(x, attn, z, mod, lw, rwt, rb, tm):
    b, t, _ = x.shape
    nt = t // tm
    per_b = (lambda i, j: (i, 0, 0)) if mod.shape[0] > 1 else (lambda i, j: (0, 0, 0))
    const2 = lambda i, j: (0, 0)
    tok = lambda i, j: (i, j, 0)
    return pl.pallas_call(
        functools.partial(_postmix_kernel, nt),
        grid=(b, nt),
        in_specs=[
            pl.BlockSpec((1, tm, D_MODEL), tok),
            pl.BlockSpec((1, tm, N_HEADS * V_HEAD), tok),
            pl.BlockSpec((1, t, C_CONV), lambda i, j: (i, 0, 0)),
            pl.BlockSpec((1, N_MOD, D_MODEL), per_b),
            pl.BlockSpec((CONV_K, C_CONV), const2),
            pl.BlockSpec((1, C_CONV), const2),
            pl.BlockSpec((1, C_CONV), const2),
            pl.BlockSpec((1, C_CONV), const2),
            pl.BlockSpec((D_MODEL, D_MODEL), const2),
            pl.BlockSpec((1, D_MODEL), const2),
            pl.BlockSpec((N_EXPERTS, D_MODEL), const2),
            pl.BlockSpec((N_EXPERTS, 1), const2),
        ],
        out_specs=[
            pl.BlockSpec((1, tm, D_MODEL), tok),
            pl.BlockSpec((1, tm, D_MODEL // 2), tok),
            pl.BlockSpec((1, tm, LANES), tok),
            pl.BlockSpec((SUBLANES, tm), lambda i, j: (0, i * nt + j)),
        ],
        out_shape=[
            jax.ShapeDtypeStruct((b, t, D_MODEL), F32),
            jax.ShapeDtypeStruct((b, t, D_MODEL // 2), jnp.int32),
            jax.ShapeDtypeStruct((b, t, LANES), F32),
            jax.ShapeDtypeStruct((SUBLANES, b * t), F32),
        ],
        scratch_shapes=[
            pltpu.VMEM((tm + 2 * HALO, C_CONV), F32),
            pltpu.VMEM((SUBLANES - 1, tm + 2 * HALO - SUBLANES, C_CONV), F32),
            pltpu.VMEM((tm, C_CONV), F32),
        ],
        compiler_params=_params("parallel", "arbitrary"),
        name="postmix",
    )(x, attn, z, mod, lw["conv_w"], lw["conv_b"], lw["ln_g"], lw["ln_b"], lw["wout"],
      lw["n2g"], rwt, rb)


N_PAIRS = EXPERTS_PER_GROUP * (EXPERTS_PER_GROUP - 1) // 2
N_CLASSES = N_GROUPS * N_PAIRS
_PAIR_LO = np.array([i for i in range(EXPERTS_PER_GROUP) for j in range(i + 1, EXPERTS_PER_GROUP)])
_PAIR_HI = np.array([j for i in range(EXPERTS_PER_GROUP) for j in range(i + 1, EXPERTS_PER_GROUP)])
MOE_ROWS = 256
MOE_STEP_TILES = 4
SC_CORES = 2
SC_SUBCORES = 16
SC_WORKERS = SC_CORES * SC_SUBCORES
SC_WINDOW = 32


def _slot_rows(n):
    return n + N_CLASSES * MOE_ROWS


def _plan(ids, n_slots):
    e_lo = ids[0].astype(jnp.int32)
    e_hi = ids[1].astype(jnp.int32)
    grp = e_lo // EXPERTS_PER_GROUP
    i = e_lo % EXPERTS_PER_GROUP
    j = e_hi % EXPERTS_PER_GROUP
    cls = grp * N_PAIRS + (i * (2 * EXPERTS_PER_GROUP - 1 - i)) // 2 + (j - i - 1)
    onehot = (cls[None, :] == jnp.arange(N_CLASSES, dtype=jnp.int32)[:, None]).astype(jnp.int32)
    csum = jnp.cumsum(onehot, axis=1)
    counts = csum[:, -1]
    padded = (counts + MOE_ROWS - 1) // MOE_ROWS * MOE_ROWS
    pend = jnp.cumsum(padded)
    pstart = pend - padded
    pos = jnp.sum(onehot * (csum - 1 + pstart[:, None]), axis=0).astype(jnp.int32)
    tile_start = jnp.arange(n_slots // MOE_ROWS, dtype=jnp.int32) * MOE_ROWS
    tcls = jnp.minimum(jnp.sum(tile_start[:, None] >= pend[None, :], axis=1), N_CLASSES - 1)
    tgrp, tpair = tcls // N_PAIRS, tcls % N_PAIRS
    t_lo = tgrp * EXPERTS_PER_GROUP + jnp.asarray(_PAIR_LO, jnp.int32)[tpair]
    t_hi = tgrp * EXPERTS_PER_GROUP + jnp.asarray(_PAIR_HI, jnp.int32)[tpair]
    tile_experts = jnp.stack([t_lo, t_hi]).astype(jnp.int32)
    n_used = (pend[-1] // MOE_ROWS).astype(jnp.int32).reshape(1)
    return pos, tile_experts, n_used


def _sc_mesh():
    return plsc.VectorSubcoreMesh(core_axis_name="core", subcore_axis_name="subcore")


def _sc_worker():
    return lax.axis_index("core") * SC_SUBCORES + lax.axis_index("subcore")


def _dispatch_call(h2, meta, pos, n_slots):
    n = h2.shape[0]
    per_worker = n // SC_WORKERS

    @pl.kernel(
        out_type=[jax.ShapeDtypeStruct((n_slots, h2.shape[1]), h2.dtype),
                  jax.ShapeDtypeStruct((n_slots, LANES), F32)],
        mesh=_sc_mesh(),
        scratch_types=[pltpu.VMEM((per_worker,), jnp.int32),
                       pltpu.VMEM((SC_WINDOW, h2.shape[1]), h2.dtype),
                       pltpu.VMEM((SC_WINDOW, LANES), F32)],
    )
    def dispatch(h2_hbm, meta_hbm, pos_hbm, xs_hbm, ms_hbm, slot_v, row_v, rec_v):
        wid = _sc_worker()
        base = wid * per_worker
        pltpu.sync_copy(pos_hbm.at[wid], slot_v)

        @pl.loop(0, per_worker // SC_WINDOW)
        def _(step):
            rows = pl.ds(base + step * SC_WINDOW, SC_WINDOW)
            slots = slot_v.at[pl.ds(step * SC_WINDOW, SC_WINDOW)]
            pltpu.sync_copy(h2_hbm.at[rows], row_v)
            pltpu.sync_copy(row_v, xs_hbm.at[slots])
            pltpu.sync_copy(meta_hbm.at[rows], rec_v)
            pltpu.sync_copy(rec_v, ms_hbm.at[slots])

    return dispatch(h2, meta, pos.reshape(SC_WORKERS, per_worker))


def _combine_call(ys, pos):
    n = pos.shape[0]
    per_worker = n // SC_WORKERS

    @pl.kernel(
        out_type=jax.ShapeDtypeStruct((n, D_MODEL), F32),
        mesh=_sc_mesh(),
        scratch_types=[pltpu.VMEM((per_worker,), jnp.int32),
                       pltpu.VMEM((SC_WINDOW, D_MODEL), F32)],
    )
    def combine(ys_hbm, pos_hbm, out_hbm, slot_v, row_v):
        wid = _sc_worker()
        base = wid * per_worker
        pltpu.sync_copy(pos_hbm.at[wid], slot_v)

        @pl.loop(0, per_worker // SC_WINDOW)
        def _(step):
            slots = slot_v.at[pl.ds(step * SC_WINDOW, SC_WINDOW)]
            pltpu.sync_copy(ys_hbm.at[slots], row_v)
            pltpu.sync_copy(row_v, out_hbm.at[pl.ds(base + step * SC_WINDOW, SC_WINDOW)])

    return combine(ys, pos.reshape(SC_WORKERS, per_worker))


def _last_used_step(nu_ref):
    return jnp.maximum((nu_ref[0] - 1) // MOE_STEP_TILES, 0)


def _experts_kernel(te_ref, nu_ref, xs_ref, ms_ref, wg_ref, wu_ref, wd_ref, ys_ref):
    step = pl.program_id(0)
    live = step <= _last_used_step(nu_ref)
    for sub in range(MOE_STEP_TILES):
        tile = step * MOE_STEP_TILES + sub
        rows = slice(sub * MOE_ROWS, (sub + 1) * MOE_ROWS)

        @pl.when(tile < nu_ref[0])
        def _(tile=tile, rows=rows):
            x = _unpack_bf16_pairs(xs_ref[rows, :])
            rec = ms_ref[rows, :]
            lane = lax.broadcasted_iota(jnp.int32, rec.shape, 1)

            def expert(e):
                gate = jnp.sum(jnp.where(lane == e, rec, 0.0), axis=-1, keepdims=True)
                hg = jnp.dot(x, wg_ref[e], preferred_element_type=F32)
                hu = jnp.dot(x, wu_ref[e], preferred_element_type=F32)
                hid = (_silu(hg) * hu * gate).astype(BF16)
                return jnp.dot(hid, wd_ref[e], preferred_element_type=F32)

            ys_ref[rows, :] = expert(te_ref[0, tile]) + expert(te_ref[1, tile])

        @pl.when((tile >= nu_ref[0]) & live)
        def _(rows=rows):
            ys_ref[rows, :] = jnp.zeros((MOE_ROWS, D_MODEL), F32)


def _experts_call(xs, ms, tile_experts, n_used, lw):
    n_slots = xs.shape[0]
    step_rows = MOE_ROWS * MOE_STEP_TILES
    rows = lambda i, te, nu: (jnp.minimum(i, _last_used_step(nu)), 0)
    resident = pl.BlockSpec(memory_space=pltpu.VMEM)
    return pl.pallas_call(
        _experts_kernel,
        grid_spec=pltpu.PrefetchScalarGridSpec(
            num_scalar_prefetch=2,
            grid=(n_slots // step_rows,),
            in_specs=[
                pl.BlockSpec((step_rows, D_MODEL // 2), rows),
                pl.BlockSpec((step_rows, LANES), rows),
                resident, resident, resident,
            ],
            out_specs=pl.BlockSpec((step_rows, D_MODEL), rows),
        ),
        out_shape=jax.ShapeDtypeStruct((n_slots, D_MODEL), F32),
        compiler_params=_params("arbitrary"),
        name="experts",
    )(tile_experts, n_used, xs, ms, lw["wg"], lw["wu"], lw["wd"])


def _routed_moe(h2, meta, ids, lw):
    b, t, _ = h2.shape
    n = b * t
    n_slots = _slot_rows(n)
    pos, tile_experts, n_used = _plan(ids, n_slots)
    xs, ms = _dispatch_call(h2.reshape(n, h2.shape[2]), meta.reshape(n, LANES), pos, n_slots)
    ys = _experts_call(xs, ms, tile_experts, n_used, lw)
    return _combine_call(ys, pos).reshape(b, t, D_MODEL)


def _final_kernel(x1_ref, moe_ref, mod_ref, fg_ref, o_ref):
    x2 = x1_ref[0] + mod_ref[0][5:6] * moe_ref[0]
    o_ref[0] = _rms(x2, fg_ref[...])


def _final_call(x1, moe, mod, final_g, tm):
    b, t, _ = x1.shape
    per_b = (lambda i, j: (i, 0, 0)) if mod.shape[0] > 1 else (lambda i, j: (0, 0, 0))
    tok = lambda i, j: (i, j, 0)
    return pl.pallas_call(
        _final_kernel,
        grid=(b, t // tm),
        in_specs=[
            pl.BlockSpec((1, tm, D_MODEL), tok),
            pl.BlockSpec((1, tm, D_MODEL), tok),
            pl.BlockSpec((1, N_MOD, D_MODEL), per_b),
            pl.BlockSpec((1, D_MODEL), lambda i, j: (0, 0)),
        ],
        out_specs=pl.BlockSpec((1, tm, D_MODEL), tok),
        out_shape=jax.ShapeDtypeStruct((b, t, D_MODEL), F32),
        compiler_params=_params("parallel", "parallel"),
        name="final_norm",
    )(x1, moe, mod, final_g)


def _prep_layer(l, norm1_g, w_in, q_norm_g, w_uq, kv_norm_g, w_ukv, conv_w, conv_b, conv_ln_g,
                conv_ln_b, w_out, norm2_g, w_gate, w_up, w_down):
    win = w_in[l]
    o_kv, o_kr, o_a, o_g = Q_LORA, Q_LORA + KV_LORA, Q_LORA + KV_LORA + QK_ROPE, Q_LORA + KV_LORA + QK_ROPE + C_CONV
    wkr = win[:, o_kr:o_a]
    pad_r = ((0, 0), (0, HEAD_BLOCK - QK_ROPE))
    win_cat = jnp.concatenate(
        [win[:, :o_kv], win[:, o_kv:o_kr], win[:, o_a:o_g], win[:, o_g:],
         jnp.pad(wkr, pad_r)], axis=1).astype(BF16)

    wuq = w_uq[l].reshape(Q_LORA, N_HEADS, QK_NOPE + QK_ROPE)
    nope, rope = wuq[..., :QK_NOPE], wuq[..., QK_NOPE:]
    zpad = jnp.zeros((Q_LORA, N_HEADS, HEAD_BLOCK - QK_NOPE - QK_ROPE), F32)
    wuq_p = jnp.concatenate([rope, nope, zpad], axis=-1).reshape(Q_LORA, -1).astype(BF16)

    wukv = w_ukv[l].reshape(KV_LORA, N_HEADS, QK_NOPE + V_HEAD)
    kz0 = jnp.zeros((KV_LORA, N_HEADS, QK_ROPE), F32)
    kz1 = jnp.zeros((KV_LORA, N_HEADS, HEAD_BLOCK - QK_NOPE - QK_ROPE), F32)
    wuk = jnp.concatenate([kz0, wukv[..., :QK_NOPE], kz1], axis=-1).reshape(KV_LORA, -1).astype(BF16)
    wuv = wukv[..., QK_NOPE:].reshape(KV_LORA, -1).astype(BF16)
    vz = jnp.zeros((KV_LORA, N_HEADS, HEAD_BLOCK - V_HEAD), F32)
    wuv_ones = jnp.concatenate([wukv[..., QK_NOPE:], vz], axis=-1).reshape(KV_LORA, -1).astype(BF16)

    return {
        "n1g": norm1_g[l][None], "win": win_cat, "qg": q_norm_g[l][None], "wuq": wuq_p,
        "kvg": kv_norm_g[l][None], "wuk": wuk, "wuv": wuv, "wuv_ones": wuv_ones,
        "conv_w": conv_w[l], "conv_b": conv_b[l][None], "ln_g": conv_ln_g[l][None],
        "ln_b": conv_ln_b[l][None], "wout": w_out[l].astype(BF16), "n2g": norm2_g[l][None],
        "wg": w_gate[l].astype(BF16), "wu": w_up[l].astype(BF16), "wd": w_down[l].astype(BF16),
    }


def _rope_tables(n_tokens):
    rows = n_tokens // GRID_W
    row = jnp.repeat(jnp.arange(rows), GRID_W).astype(F32)
    col = jnp.tile(jnp.arange(GRID_W), rows).astype(F32)
    freqs = ROPE_BASE ** (-jnp.arange(ROPE_PAIRS, dtype=F32) / ROPE_PAIRS)
    ar, ac = row[:, None] * freqs, col[:, None] * freqs
    cr, sr, cc, sc = jnp.cos(ar), jnp.sin(ar), jnp.cos(ac), jnp.sin(ac)
    rest = HEAD_BLOCK - QK_ROPE
    cos = jnp.concatenate([cr, cr, cc, cc, jnp.ones((n_tokens, rest), F32)], axis=1)
    sin = jnp.concatenate([-sr, sr, -sc, sc, jnp.zeros((n_tokens, rest), F32)], axis=1)
    return cos, sin


TM_PROMPT = 256
TM_SAMPLE = 512
TQ_SAMPLE = 512


def _fold(a):
    return None if a is None else a.reshape(a.shape[0] // 2, a.shape[1] * 2, a.shape[2])


def _unfold(a):
    return a.reshape(a.shape[0] * 2, a.shape[1] // 2, a.shape[2])


def kernel(x_prompt, x_sample, cache_ckv, cache_krope, c, c_ctx, w_ada, b_ada, norm1_g, w_in,
           q_norm_g, w_uq, kv_norm_g, w_ukv, conv_w, conv_b, conv_ln_g, conv_ln_b, w_out,
           norm2_g, router_w, router_b, w_gate, w_up, w_down, final_g):
    dec_b = x_sample.shape[0]
    cvecs = jnp.concatenate(
        [c_ctx[None], c, jnp.zeros((MOD_ROWS - 1 - dec_b, D_MODEL), F32)], axis=0)
    mod_all = _ada_call(cvecs, w_ada, b_ada).reshape(DEPTH, MOD_ROWS, N_MOD, D_MODEL)

    rwt = router_w.T
    rb = router_b[:, None]
    fg = final_g[None]
    rope_tabs = _rope_tables(x_sample.shape[1])
    cache_kr_pad = jnp.pad(cache_krope, ((0, 0), (0, 0), (0, 0), (0, HEAD_BLOCK - QK_ROPE)))

    xp, xs = x_prompt, x_sample
    moe_p = moe_s = mod_p = mod_s = None
    ckvs, krs = [], []
    for l in range(DEPTH):
        lw = _prep_layer(l, norm1_g, w_in, q_norm_g, w_uq, kv_norm_g, w_ukv, conv_w, conv_b,
                         conv_ln_g, conv_ln_b, w_out, norm2_g, w_gate, w_up, w_down)
        prev_p, prev_s = mod_p, mod_s
        mod_p = mod_all[l, 0:1]
        mod_s = mod_all[l, 1:1 + dec_b]

        outs = _premix_call(_fold(xp), mod_p, lw, TM_SAMPLE, moe=_fold(moe_p), mod_prev=prev_p)
        outs = [_unfold(o) for o in outs]
        if l > 0:
            q, k, v, z, xp, ckv, kr = outs
        else:
            q, k, v, z, ckv, kr = outs
        ckvs.append(ckv)
        krs.append(kr)
        attn = _attn_call(q, k, v, xp.shape[1])
        xp, h2, meta, ids = _postmix_call(xp, attn, z, mod_p, lw, rwt, rb, TM_PROMPT)
        moe_p = _routed_moe(h2, meta, ids, lw)

        outs = _premix_call(xs, mod_s, lw, TM_SAMPLE, rope_tabs, moe=moe_s, mod_prev=prev_s)
        if l > 0:
            q, k, v, z, xs = outs
        else:
            q, k, v, z = outs
        ctx_kv = _ctx_call(cache_ckv, cache_kr_pad, l, lw)
        attn = _attn_call(q, k, v, TQ_SAMPLE, ctx_kv)
        xs, h2, meta, ids = _postmix_call(xs, attn, z, mod_s, lw, rwt, rb, TM_SAMPLE)
        moe_s = _routed_moe(h2, meta, ids, lw)

    y_prompt = _unfold(_final_call(_fold(xp), _fold(moe_p), mod_p, fg, TM_SAMPLE))
    y_sample = _final_call(xs, moe_s, mod_s, fg, TM_SAMPLE)
    return y_prompt, y_sample, jnp.stack(ckvs, axis=1), jnp.stack(krs, axis=1)
```

```python
import functools

import jax
import jax.numpy as jnp
import numpy as np
from jax import lax
from jax.experimental import pallas as pl
from jax.experimental.pallas import tpu as pltpu
from jax.experimental.pallas import tpu_sc as plsc

D_MODEL = 1024
DEPTH = 4
GRID_W = 64
N_HEADS = 8
QK_NOPE = 64
QK_ROPE = 32
V_HEAD = 64
Q_LORA = 384
KV_LORA = 256
C_CONV = 512
CONV_K = 31
CONV_PAD = CONV_K // 2
N_EXPERTS = 16
N_GROUPS = 4
EXPERTS_PER_GROUP = N_EXPERTS // N_GROUPS
D_EXPERT = 256
ROPE_BASE = 10000.0
ROPE_PAIRS = QK_ROPE // 4
EPS = 1e-6
ATTN_SCALE = (QK_NOPE + QK_ROPE) ** -0.5
LOG2E = 1.4426950408889634

LANES = 128
SUBLANES = 8
HEAD_BLOCK = LANES
N_MOD = 6
MOD_ROWS = 16
HALO = 16
VMEM_LIMIT = 48 * 1024 * 1024

BF16 = jnp.bfloat16
F32 = jnp.float32


def _rms(x, g):
    return x * lax.rsqrt(jnp.mean(x * x, axis=-1, keepdims=True) + EPS) * g


def _silu(x):
    return x * jax.nn.sigmoid(x)


def _ones_lane(width):
    lane = lax.broadcasted_iota(jnp.int32, (1, width), 1)
    return jnp.where(lane % HEAD_BLOCK == V_HEAD, 1.0, 0.0)


def _pack_bf16_pairs(x):
    half = x.shape[1] // 2
    xb = x.astype(BF16).astype(F32)
    hi = pltpu.bitcast(xb[:, :half], jnp.uint32)
    lo = pltpu.bitcast(xb[:, half:], jnp.uint32)
    return pltpu.bitcast(hi | (lo >> 16), jnp.int32)


def _unpack_bf16_pairs(w, dtype=BF16):
    u = pltpu.bitcast(w, jnp.uint32)
    hi = pltpu.bitcast(u & jnp.uint32(0xFFFF0000), F32).astype(dtype)
    lo = pltpu.bitcast(u << 16, F32).astype(dtype)
    return jnp.concatenate([hi, lo], axis=1)


def _params(*sem):
    return pltpu.CompilerParams(dimension_semantics=sem, vmem_limit_bytes=VMEM_LIMIT)


def _ada_kernel(c_ref, w_ref, b_ref, o_ref):
    s = _silu(c_ref[...]).astype(BF16)
    o_ref[...] = jnp.dot(s, w_ref[...].astype(BF16), preferred_element_type=F32) + b_ref[...]


def _ada_call(cvecs, w_ada, b_ada):
    return pl.pallas_call(
        _ada_kernel,
        grid=(DEPTH, N_MOD),
        in_specs=[
            pl.BlockSpec((MOD_ROWS, D_MODEL), lambda l, n: (0, 0)),
            pl.BlockSpec((None, D_MODEL, D_MODEL), lambda l, n: (l, 0, n)),
            pl.BlockSpec((None, 1, D_MODEL), lambda l, n: (l, 0, n)),
        ],
        out_specs=pl.BlockSpec((None, MOD_ROWS, D_MODEL), lambda l, n: (l, 0, n)),
        out_shape=jax.ShapeDtypeStruct((DEPTH, MOD_ROWS, N_MOD * D_MODEL), F32),
        compiler_params=_params("parallel", "parallel"),
        name="ada",
    )(cvecs, w_ada, b_ada.reshape(DEPTH, 1, N_MOD * D_MODEL))


W_IN_Q = 0
W_IN_KV = W_IN_Q + Q_LORA
W_IN_A = W_IN_KV + KV_LORA
W_IN_G = W_IN_A + C_CONV
W_IN_KR = W_IN_G + C_CONV
W_IN_COLS = W_IN_KR + HEAD_BLOCK


def _swap_halves(x):
    lane = lax.broadcasted_iota(jnp.int32, x.shape, 1)
    first = lane % (2 * ROPE_PAIRS) < ROPE_PAIRS
    from_right = pltpu.roll(x, HEAD_BLOCK - ROPE_PAIRS, 1)
    from_left = pltpu.roll(x, ROPE_PAIRS, 1)
    return jnp.where(first, from_right, from_left)


def _premix_kernel(rope, resid, x_ref, *rest):
    if resid:
        moe_ref, modprev_ref, *rest = rest
    mod_ref, n1g_ref, win_ref, qg_ref, wuq_ref, kvg_ref, wuk_ref, wuv_ref, *rest = rest
    if rope:
        cos_ref, sin_ref, *rest = rest
    q_ref, k_ref, v_ref, z_ref, *rest = rest
    if resid:
        x2_ref, *rest = rest
    if not rope:
        ckv_ref, kr_ref = rest
    x = x_ref[0]
    if resid:
        x = x + modprev_ref[0][5:6] * _unpack_bf16_pairs(moe_ref[0], F32)
        x2_ref[0] = x
    mod = mod_ref[0]
    sh1, sc1 = mod[0:1], mod[1:2]
    h = (_rms(x, n1g_ref[...]) * (1.0 + sc1) + sh1).astype(BF16)
    proj = jnp.dot(h, win_ref[...], preferred_element_type=F32)
    q_c = proj[:, W_IN_Q:W_IN_KV]
    kv_c = proj[:, W_IN_KV:W_IN_A]
    a = proj[:, W_IN_A:W_IN_G]
    gate = proj[:, W_IN_G:W_IN_KR]
    kr = proj[:, W_IN_KR:W_IN_COLS]

    z_ref[0] = a * jax.nn.sigmoid(gate)

    qn = _rms(q_c, qg_ref[...]).astype(BF16)
    q = jnp.dot(qn, wuq_ref[...], preferred_element_type=F32)
    ckv = _rms(kv_c, kvg_ref[...])
    ckv_b = ckv.astype(BF16)
    kn = jnp.dot(ckv_b, wuk_ref[...], preferred_element_type=F32)
    v = jnp.dot(ckv_b, wuv_ref[...], preferred_element_type=F32)

    if rope:
        v = v + _ones_lane(v.shape[1])
        cos = cos_ref[...]
        sin = sin_ref[...]
        kr = kr * cos + _swap_halves(kr) * sin
    else:
        ckv_ref[0] = ckv
        kr_ref[0] = kr[:, :QK_ROPE]
    v_ref[0] = v.astype(BF16)

    for hd in range(N_HEADS):
        sl = slice(hd * HEAD_BLOCK, (hd + 1) * HEAD_BLOCK)
        qh = q[:, sl]
        if rope:
            qh = qh * cos + _swap_halves(qh) * sin
        q_ref[0, :, sl] = (qh * (ATTN_SCALE * LOG2E)).astype(BF16)
        k_ref[0, :, sl] = (kn[:, sl] + kr).astype(BF16)


def _premix_call(x, mod, lw, tm, rope_tabs=None, moe=None, mod_prev=None):
    b, t, _ = x.shape
    rope = rope_tabs is not None
    resid = moe is not None
    nt = t // tm
    per_b = (lambda i, j: (i, 0, 0)) if mod.shape[0] > 1 else (lambda i, j: (0, 0, 0))
    const2 = lambda i, j: (0, 0)
    tok = lambda i, j: (i, j, 0)
    v_cols = N_HEADS * (HEAD_BLOCK if rope else V_HEAD)
    in_specs = [pl.BlockSpec((1, tm, D_MODEL), tok)]
    args = [x]
    if resid:
        in_specs += [pl.BlockSpec((1, tm, D_MODEL // 2), tok),
                     pl.BlockSpec((1, N_MOD, D_MODEL), per_b)]
        args += [moe, mod_prev]
    in_specs += [
        pl.BlockSpec((1, N_MOD, D_MODEL), per_b),
        pl.BlockSpec((1, D_MODEL), const2),
        pl.BlockSpec((D_MODEL, W_IN_COLS), const2),
        pl.BlockSpec((1, Q_LORA), const2),
        pl.BlockSpec((Q_LORA, N_HEADS * HEAD_BLOCK), const2),
        pl.BlockSpec((1, KV_LORA), const2),
        pl.BlockSpec((KV_LORA, N_HEADS * HEAD_BLOCK), const2),
        pl.BlockSpec((KV_LORA, v_cols), const2),
    ]
    args += [mod, lw["n1g"], lw["win"], lw["qg"], lw["wuq"], lw["kvg"], lw["wuk"],
             lw["wuv_ones"] if rope else lw["wuv"]]
    out_specs = [
        pl.BlockSpec((1, tm, N_HEADS * HEAD_BLOCK), tok),
        pl.BlockSpec((1, tm, N_HEADS * HEAD_BLOCK), tok),
        pl.BlockSpec((1, tm, v_cols), tok),
        pl.BlockSpec((1, tm, C_CONV), tok),
    ]
    out_shape = [
        jax.ShapeDtypeStruct((b, t, N_HEADS * HEAD_BLOCK), BF16),
        jax.ShapeDtypeStruct((b, t, N_HEADS * HEAD_BLOCK), BF16),
        jax.ShapeDtypeStruct((b, t, v_cols), BF16),
        jax.ShapeDtypeStruct((b, t, C_CONV), F32),
    ]
    if resid:
        out_specs.append(pl.BlockSpec((1, tm, D_MODEL), tok))
        out_shape.append(jax.ShapeDtypeStruct((b, t, D_MODEL), F32))
    if rope:
        in_specs += [
            pl.BlockSpec((tm, HEAD_BLOCK), lambda i, j: (j, 0)),
            pl.BlockSpec((tm, HEAD_BLOCK), lambda i, j: (j, 0)),
        ]
        args += [rope_tabs[0], rope_tabs[1]]
    else:
        out_specs += [
            pl.BlockSpec((1, tm, KV_LORA), tok),
            pl.BlockSpec((1, tm, QK_ROPE), tok),
        ]
        out_shape += [
            jax.ShapeDtypeStruct((b, t, KV_LORA), F32),
            jax.ShapeDtypeStruct((b, t, QK_ROPE), F32),
        ]
    return pl.pallas_call(
        functools.partial(_premix_kernel, rope, resid),
        grid=(b, nt),
        in_specs=in_specs,
        out_specs=out_specs,
        out_shape=out_shape,
        compiler_params=_params("parallel", "parallel"),
        name="premix_rope" if rope else "premix",
    )(*args)


def _ctx_kernel(ckv_ref, kr_ref, wuk_ref, wuv_ref, k_ref, v_ref):
    ckv_b = ckv_ref[...].astype(BF16)
    kn = jnp.dot(ckv_b, wuk_ref[...], preferred_element_type=F32)
    v = jnp.dot(ckv_b, wuv_ref[...], preferred_element_type=F32)
    v_ref[0] = (v + _ones_lane(v.shape[1])).astype(BF16)
    kr = kr_ref[...]
    for hd in range(N_HEADS):
        sl = slice(hd * HEAD_BLOCK, (hd + 1) * HEAD_BLOCK)
        k_ref[0, :, sl] = (kn[:, sl] + kr).astype(BF16)


def _ctx_call(cache_ckv, cache_kr_pad, layer, lw):
    b, _, s, _ = cache_ckv.shape
    const2 = lambda i: (0, 0)
    return pl.pallas_call(
        _ctx_kernel,
        grid=(b,),
        in_specs=[
            pl.BlockSpec((None, None, s, KV_LORA), lambda i: (i, layer, 0, 0)),
            pl.BlockSpec((None, None, s, HEAD_BLOCK), lambda i: (i, layer, 0, 0)),
            pl.BlockSpec((KV_LORA, N_HEADS * HEAD_BLOCK), const2),
            pl.BlockSpec((KV_LORA, N_HEADS * HEAD_BLOCK), const2),
        ],
        out_specs=[
            pl.BlockSpec((1, s, N_HEADS * HEAD_BLOCK), lambda i: (i, 0, 0)),
            pl.BlockSpec((1, s, N_HEADS * HEAD_BLOCK), lambda i: (i, 0, 0)),
        ],
        out_shape=[
            jax.ShapeDtypeStruct((b, s, N_HEADS * HEAD_BLOCK), BF16),
            jax.ShapeDtypeStruct((b, s, N_HEADS * HEAD_BLOCK), BF16),
        ],
        compiler_params=_params("parallel"),
        name="ctx_kv",
    )(cache_ckv, cache_kr_pad, lw["wuk"], lw["wuv_ones"])


_NT = (((1,), (1,)), ((), ()))


def _attn_kernel(ctx, q_ref, k_ref, v_ref, *rest):
    if ctx:
        kc_ref, vc_ref, o_ref = rest
    else:
        (o_ref,) = rest
    outs = []
    for hd in range(N_HEADS):
        sl = slice(hd * HEAD_BLOCK, (hd + 1) * HEAD_BLOCK)
        qh = q_ref[0, :, sl]
        s = lax.dot_general(qh, k_ref[0, :, sl], _NT, preferred_element_type=F32)
        m = jnp.max(s, axis=-1, keepdims=True)
        if ctx:
            sc = lax.dot_general(qh, kc_ref[0, :, sl], _NT, preferred_element_type=F32)
            m = jnp.maximum(m, jnp.max(sc, axis=-1, keepdims=True))
        p = jnp.exp2(s - m)
        if ctx:
            pc = jnp.exp2(sc - m)
            o = jnp.dot(p.astype(BF16), v_ref[0, :, sl], preferred_element_type=F32)
            o = o + jnp.dot(pc.astype(BF16), vc_ref[0, :, sl], preferred_element_type=F32)
            outs.append(o[:, :V_HEAD] / o[:, V_HEAD:V_HEAD + 1])
        else:
            vs = slice(hd * V_HEAD, (hd + 1) * V_HEAD)
            o = jnp.dot(p.astype(BF16), v_ref[0, :, vs], preferred_element_type=F32)
            outs.append(o / jnp.sum(p, axis=-1, keepdims=True))
    o_ref[0] = jnp.concatenate(outs, axis=-1).astype(BF16)


def _attn_call(q, k, v, tq, ctx_kv=None):
    b, t, _ = q.shape
    ctx = ctx_kv is not None
    tok = lambda i, j: (i, j, 0)
    seq = lambda i, j: (i, 0, 0)
    in_specs = [
        pl.BlockSpec((1, tq, N_HEADS * HEAD_BLOCK), tok),
        pl.BlockSpec((1, t, N_HEADS * HEAD_BLOCK), seq),
        pl.BlockSpec((1, t, v.shape[2]), seq),
    ]
    args = [q, k, v]
    if ctx:
        s = ctx_kv[0].shape[1]
        in_specs += [
            pl.BlockSpec((1, s, N_HEADS * HEAD_BLOCK), seq),
            pl.BlockSpec((1, s, N_HEADS * HEAD_BLOCK), seq),
        ]
        args += list(ctx_kv)
    return pl.pallas_call(
        functools.partial(_attn_kernel, ctx),
        grid=(b, t // tq),
        in_specs=in_specs,
        out_specs=pl.BlockSpec((1, tq, N_HEADS * V_HEAD), tok),
        out_shape=jax.ShapeDtypeStruct((b, t, N_HEADS * V_HEAD), BF16),
        compiler_params=_params("parallel", "arbitrary"),
        name="attn_ctx" if ctx else "attn",
    )(*args)


CONV_ROWS = 64


def _route(aff, bias):
    sel = aff + bias
    rows = [sel[e:e + 1] for e in range(N_EXPERTS)]
    affr = [aff[e:e + 1] for e in range(N_EXPERTS)]
    scores = []
    for g in range(N_GROUPS):
        a, b, c, d = rows[g * EXPERTS_PER_GROUP:(g + 1) * EXPERTS_PER_GROUP]
        hi1, lo1 = jnp.maximum(a, b), jnp.minimum(a, b)
        hi2, lo2 = jnp.maximum(c, d), jnp.minimum(c, d)
        top1 = jnp.maximum(hi1, hi2)
        top2 = jnp.maximum(jnp.minimum(hi1, hi2), jnp.maximum(lo1, lo2))
        scores.append(top1 + top2)
    best = scores[0]
    gidx = jnp.zeros_like(best, dtype=jnp.int32)
    for g in range(1, N_GROUPS):
        better = scores[g] > best
        best = jnp.where(better, scores[g], best)
        gidx = jnp.where(better, g, gidx)
    gates = []
    e_lo = jnp.full_like(best, float(N_EXPERTS))
    e_hi = jnp.full_like(best, -1.0)
    for g in range(N_GROUPS):
        in_g = gidx == g
        s = rows[g * EXPERTS_PER_GROUP:(g + 1) * EXPERTS_PER_GROUP]
        af = affr[g * EXPERTS_PER_GROUP:(g + 1) * EXPERTS_PER_GROUP]
        picked = []
        for i in range(EXPERTS_PER_GROUP):
            rank = jnp.zeros_like(gidx)
            for j in range(EXPERTS_PER_GROUP):
                if j == i:
                    continue
                ahead = (s[j] >= s[i]) if j < i else (s[j] > s[i])
                rank = rank + ahead.astype(jnp.int32)
            chosen = (rank < 2) & in_g
            eid = float(g * EXPERTS_PER_GROUP + i)
            e_lo = jnp.where(chosen, jnp.minimum(e_lo, eid), e_lo)
            e_hi = jnp.where(chosen, jnp.maximum(e_hi, eid), e_hi)
            picked.append(jnp.where(chosen, af[i], 0.0))
        den = (picked[0] + picked[1]) + (picked[2] + picked[3])
        den = jnp.where(in_g, den, 1.0)
        gates += [pk / den for pk in picked]
    return jnp.concatenate(gates, axis=0), e_lo, e_hi


def _postmix_kernel(nt, x_ref, attn_ref, z_ref, mod_ref, cw_ref, cb_ref, lg_ref, lb_ref, wout_ref,
                    n2g_ref, rwt_ref, rb_ref, x1_ref, h2_ref, meta_ref, ids_ref, zp_ref, zs_ref, cz_ref):
    nseq, tm = x_ref.shape[0], x_ref.shape[1]
    rows = nseq * tm
    j = pl.program_id(1)
    t0 = pl.multiple_of(j * tm, tm)
    zeros = jnp.zeros((HALO, C_CONV), F32)
    cw = cw_ref[...]
    cb = cb_ref[...]
    span = zs_ref.shape[1]

    for sq in range(nseq):
        zp_ref[HALO:HALO + tm, :] = z_ref[sq, pl.ds(t0, tm), :]

        @pl.when(j == 0)
        def _():
            zp_ref[0:HALO, :] = zeros

        @pl.when(j > 0)
        def _(sq=sq):
            zp_ref[0:HALO, :] = z_ref[sq, pl.ds(t0 - HALO, HALO), :]

        @pl.when(j == nt - 1)
        def _():
            zp_ref[HALO + tm:, :] = zeros

        @pl.when(j < nt - 1)
        def _(sq=sq):
            zp_ref[HALO + tm:, :] = z_ref[sq, pl.ds(t0 + tm, HALO), :]

        for o in range(1, SUBLANES):
            zs_ref[o - 1] = zp_ref[o:o + span, :]

        for r0 in range(0, tm, CONV_ROWS):
            acc = jnp.broadcast_to(cb, (CONV_ROWS, C_CONV))
            for kk in range(CONV_K):
                off = HALO - CONV_PAD + kk
                o = off % SUBLANES
                base = r0 + off - o
                if o == 0:
                    tap = zp_ref[base:base + CONV_ROWS, :]
                else:
                    tap = zs_ref[o - 1, base:base + CONV_ROWS, :]
                acc = acc + tap * cw[kk:kk + 1, :]
            cz_ref[sq * tm + r0:sq * tm + r0 + CONV_ROWS, :] = acc

    cz = cz_ref[...]
    mu = jnp.mean(cz, axis=-1, keepdims=True)
    cen = cz - mu
    var = jnp.mean(cen * cen, axis=-1, keepdims=True)
    conv = _silu(cen * lax.rsqrt(var + EPS) * lg_ref[...] + lb_ref[...]).astype(BF16)

    n_attn = N_HEADS * V_HEAD
    y = jnp.dot(attn_ref[...].reshape(rows, n_attn), wout_ref[0:n_attn, :],
                preferred_element_type=F32)
    y = y + jnp.dot(conv, wout_ref[n_attn:, :], preferred_element_type=F32)

    mod = mod_ref[0]
    g1, sh2, sc2 = mod[2:3], mod[3:4], mod[4:5]
    x1 = x_ref[...].reshape(rows, D_MODEL) + g1 * y
    x1_ref[...] = x1.reshape(x1_ref.shape)
    h2 = _rms(x1, n2g_ref[...]) * (1.0 + sc2) + sh2
    h2_ref[...] = _pack_bf16_pairs(h2).reshape(h2_ref.shape)

    logits = lax.dot_general(rwt_ref[...], h2, _NT, precision=lax.Precision.HIGHEST,
                             preferred_element_type=F32)
    gates, e_lo, e_hi = _route(jax.nn.sigmoid(logits), rb_ref[...])
    row = lax.broadcasted_iota(jnp.int32, (SUBLANES, rows), 0)
    ids = jnp.where(row == 0, e_lo, jnp.where(row == 1, e_hi, 0.0))
    ids_ref[...] = ids
    rec = jnp.concatenate([gates, ids, jnp.zeros((LANES - N_EXPERTS - SUBLANES, rows), F32)], axis=0)
    meta_ref[...] = rec.T.reshape(meta_ref.shape)


def _postmix_call(x, attn, z, mod, lw, rwt, rb, tm, nseq=1):
    b, t, _ = x.shape
    nt = t // tm
    assert nseq == 1 or (nt == 1 and mod.shape[0] == 1)
    per_b = (lambda i, j: (i, 0, 0)) if mod.shape[0] > 1 else (lambda i, j: (0, 0, 0))
    const2 = lambda i, j: (0, 0)
    tok = lambda i, j: (i, j, 0)
    return pl.pallas_call(
        functools.partial(_postmix_kernel, nt),
        grid=(b // nseq, nt),
        in_specs=[
            pl.BlockSpec((nseq, tm, D_MODEL), tok),
            pl.BlockSpec((nseq, tm, N_HEADS * V_HEAD), tok),
            pl.BlockSpec((nseq, t, C_CONV), lambda i, j: (i, 0, 0)),
            pl.BlockSpec((1, N_MOD, D_MODEL), per_b),
            pl.BlockSpec((CONV_K, C_CONV), const2),
            pl.BlockSpec((1, C_CONV), const2),
            pl.BlockSpec((1, C_CONV), const2),
            pl.BlockSpec((1, C_CONV), const2),
            pl.BlockSpec((D_MODEL, D_MODEL), const2),
            pl.BlockSpec((1, D_MODEL), const2),
            pl.BlockSpec((N_EXPERTS, D_MODEL), const2),
            pl.BlockSpec((N_EXPERTS, 1), const2),
        ],
        out_specs=[
            pl.BlockSpec((nseq, tm, D_MODEL), tok),
            pl.BlockSpec((nseq, tm, D_MODEL // 2), tok),
            pl.BlockSpec((nseq, tm, LANES), tok),
            pl.BlockSpec((SUBLANES, nseq * tm), lambda i, j: (0, i * nt + j)),
        ],
        out_shape=[
            jax.ShapeDtypeStruct((b, t, D_MODEL), F32),
            jax.ShapeDtypeStruct((b, t, D_MODEL // 2), jnp.int32),
            jax.ShapeDtypeStruct((b, t, LANES), F32),
            jax.ShapeDtypeStruct((SUBLANES, b * t), F32),
        ],
        scratch_shapes=[
            pltpu.VMEM((tm + 2 * HALO, C_CONV), F32),
            pltpu.VMEM((SUBLANES - 1, tm + 2 * HALO - SUBLANES, C_CONV), F32),
            pltpu.VMEM((nseq * tm, C_CONV), F32),
        ],
        compiler_params=_params("parallel", "arbitrary"),
        name="postmix",
    )(x, attn, z, mod, lw["conv_w"], lw["conv_b"], lw["ln_g"], lw["ln_b"], lw["wout"],
      lw["n2g"], rwt, rb)


N_PAIRS = EXPERTS_PER_GROUP * (EXPERTS_PER_GROUP - 1) // 2
N_CLASSES = N_GROUPS * N_PAIRS
_PAIR_LO = np.array([i for i in range(EXPERTS_PER_GROUP) for j in range(i + 1, EXPERTS_PER_GROUP)])
_PAIR_HI = np.array([j for i in range(EXPERTS_PER_GROUP) for j in range(i + 1, EXPERTS_PER_GROUP)])
MOE_ROWS = 256
MOE_STEP_TILES = 4
SC_CORES = 2
SC_SUBCORES = 16
SC_WORKERS = SC_CORES * SC_SUBCORES
SC_WINDOW = 32


def _slot_rows(n):
    return n + N_CLASSES * MOE_ROWS


def _plan(ids, n_slots):
    e_lo = ids[0].astype(jnp.int32)
    e_hi = ids[1].astype(jnp.int32)
    grp = e_lo // EXPERTS_PER_GROUP
    i = e_lo % EXPERTS_PER_GROUP
    j = e_hi % EXPERTS_PER_GROUP
    cls = grp * N_PAIRS + (i * (2 * EXPERTS_PER_GROUP - 1 - i)) // 2 + (j - i - 1)
    onehot = (cls[None, :] == jnp.arange(N_CLASSES, dtype=jnp.int32)[:, None]).astype(jnp.int32)
    csum = jnp.cumsum(onehot, axis=1)
    counts = csum[:, -1]
    padded = (counts + MOE_ROWS - 1) // MOE_ROWS * MOE_ROWS
    pend = jnp.cumsum(padded)
    pstart = pend - padded
    pos = jnp.sum(onehot * (csum - 1 + pstart[:, None]), axis=0).astype(jnp.int32)
    tile_start = jnp.arange(n_slots // MOE_ROWS, dtype=jnp.int32) * MOE_ROWS
    tcls = jnp.minimum(jnp.sum(tile_start[:, None] >= pend[None, :], axis=1), N_CLASSES - 1)
    tgrp, tpair = tcls // N_PAIRS, tcls % N_PAIRS
    t_lo = tgrp * EXPERTS_PER_GROUP + jnp.asarray(_PAIR_LO, jnp.int32)[tpair]
    t_hi = tgrp * EXPERTS_PER_GROUP + jnp.asarray(_PAIR_HI, jnp.int32)[tpair]
    tile_experts = jnp.stack([t_lo, t_hi]).astype(jnp.int32)
    n_used = (pend[-1] // MOE_ROWS).astype(jnp.int32).reshape(1)
    return pos, tile_experts, n_used


def _sc_mesh():
    return plsc.VectorSubcoreMesh(core_axis_name="core", subcore_axis_name="subcore")


def _sc_worker():
    return lax.axis_index("core") * SC_SUBCORES + lax.axis_index("subcore")


def _dispatch_call(h2, meta, pos, n_slots):
    n = h2.shape[0]
    per_worker = n // SC_WORKERS

    @pl.kernel(
        out_type=[jax.ShapeDtypeStruct((n_slots, h2.shape[1]), h2.dtype),
                  jax.ShapeDtypeStruct((n_slots, LANES), F32)],
        mesh=_sc_mesh(),
        scratch_types=[pltpu.VMEM((per_worker,), jnp.int32),
                       pltpu.VMEM((SC_WINDOW, h2.shape[1]), h2.dtype),
                       pltpu.VMEM((SC_WINDOW, LANES), F32)],
    )
    def dispatch(h2_hbm, meta_hbm, pos_hbm, xs_hbm, ms_hbm, slot_v, row_v, rec_v):
        wid = _sc_worker()
        base = wid * per_worker
        pltpu.sync_copy(pos_hbm.at[wid], slot_v)

        @pl.loop(0, per_worker // SC_WINDOW)
        def _(step):
            rows = pl.ds(base + step * SC_WINDOW, SC_WINDOW)
            slots = slot_v.at[pl.ds(step * SC_WINDOW, SC_WINDOW)]
            pltpu.sync_copy(h2_hbm.at[rows], row_v)
            pltpu.sync_copy(row_v, xs_hbm.at[slots])
            pltpu.sync_copy(meta_hbm.at[rows], rec_v)
            pltpu.sync_copy(rec_v, ms_hbm.at[slots])

    return dispatch(h2, meta, pos.reshape(SC_WORKERS, per_worker))


def _combine_call(ys, pos):
    n = pos.shape[0]
    per_worker = n // SC_WORKERS

    @pl.kernel(
        out_type=jax.ShapeDtypeStruct((n, ys.shape[1]), ys.dtype),
        mesh=_sc_mesh(),
        scratch_types=[pltpu.VMEM((per_worker,), jnp.int32),
                       pltpu.VMEM((SC_WINDOW, ys.shape[1]), ys.dtype)],
    )
    def combine(ys_hbm, pos_hbm, out_hbm, slot_v, row_v):
        wid = _sc_worker()
        base = wid * per_worker
        pltpu.sync_copy(pos_hbm.at[wid], slot_v)

        @pl.loop(0, per_worker // SC_WINDOW)
        def _(step):
            slots = slot_v.at[pl.ds(step * SC_WINDOW, SC_WINDOW)]
            pltpu.sync_copy(ys_hbm.at[slots], row_v)
            pltpu.sync_copy(row_v, out_hbm.at[pl.ds(base + step * SC_WINDOW, SC_WINDOW)])

    return combine(ys, pos.reshape(SC_WORKERS, per_worker))


def _last_used_step(nu_ref):
    return jnp.maximum((nu_ref[0] - 1) // MOE_STEP_TILES, 0)


def _experts_kernel(te_ref, nu_ref, xs_ref, ms_ref, wg_ref, wu_ref, wd_ref, ys_ref):
    step = pl.program_id(0)
    live = step <= _last_used_step(nu_ref)
    for sub in range(MOE_STEP_TILES):
        tile = step * MOE_STEP_TILES + sub
        rows = slice(sub * MOE_ROWS, (sub + 1) * MOE_ROWS)

        @pl.when(tile < nu_ref[0])
        def _(tile=tile, rows=rows):
            x = _unpack_bf16_pairs(xs_ref[rows, :])
            rec = ms_ref[rows, :]
            lane = lax.broadcasted_iota(jnp.int32, rec.shape, 1)

            def expert(e):
                gate = jnp.sum(jnp.where(lane == e, rec, 0.0), axis=-1, keepdims=True)
                hg = jnp.dot(x, wg_ref[e], preferred_element_type=F32)
                hu = jnp.dot(x, wu_ref[e], preferred_element_type=F32)
                hid = (_silu(hg) * hu * gate).astype(BF16)
                return jnp.dot(hid, wd_ref[e], preferred_element_type=F32)

            y = expert(te_ref[0, tile]) + expert(te_ref[1, tile])
            ys_ref[rows, :] = _pack_bf16_pairs(y)

        @pl.when((tile >= nu_ref[0]) & live)
        def _(rows=rows):
            ys_ref[rows, :] = jnp.zeros((MOE_ROWS, D_MODEL // 2), jnp.int32)


def _experts_call(xs, ms, tile_experts, n_used, lw):
    n_slots = xs.shape[0]
    step_rows = MOE_ROWS * MOE_STEP_TILES
    rows = lambda i, te, nu: (jnp.minimum(i, _last_used_step(nu)), 0)
    resident = pl.BlockSpec(memory_space=pltpu.VMEM)
    return pl.pallas_call(
        _experts_kernel,
        grid_spec=pltpu.PrefetchScalarGridSpec(
            num_scalar_prefetch=2,
            grid=(n_slots // step_rows,),
            in_specs=[
                pl.BlockSpec((step_rows, D_MODEL // 2), rows),
                pl.BlockSpec((step_rows, LANES), rows),
                resident, resident, resident,
            ],
            out_specs=pl.BlockSpec((step_rows, D_MODEL // 2), rows),
        ),
        out_shape=jax.ShapeDtypeStruct((n_slots, D_MODEL // 2), jnp.int32),
        compiler_params=_params("arbitrary"),
        name="experts",
    )(tile_experts, n_used, xs, ms, lw["wg"], lw["wu"], lw["wd"])


def _routed_moe(h2, meta, ids, lw):
    b, t, _ = h2.shape
    n = b * t
    n_slots = _slot_rows(n)
    pos, tile_experts, n_used = _plan(ids, n_slots)
    xs, ms = _dispatch_call(h2.reshape(n, h2.shape[2]), meta.reshape(n, LANES), pos, n_slots)
    ys = _experts_call(xs, ms, tile_experts, n_used, lw)
    return _combine_call(ys, pos).reshape(b, t, ys.shape[1])


def _final_kernel(x1_ref, moe_ref, mod_ref, fg_ref, o_ref):
    x2 = x1_ref[0] + mod_ref[0][5:6] * _unpack_bf16_pairs(moe_ref[0], F32)
    o_ref[0] = _rms(x2, fg_ref[...])


def _final_call(x1, moe, mod, final_g, tm):
    b, t, _ = x1.shape
    per_b = (lambda i, j: (i, 0, 0)) if mod.shape[0] > 1 else (lambda i, j: (0, 0, 0))
    tok = lambda i, j: (i, j, 0)
    return pl.pallas_call(
        _final_kernel,
        grid=(b, t // tm),
        in_specs=[
            pl.BlockSpec((1, tm, D_MODEL), tok),
            pl.BlockSpec((1, tm, D_MODEL // 2), tok),
            pl.BlockSpec((1, N_MOD, D_MODEL), per_b),
            pl.BlockSpec((1, D_MODEL), lambda i, j: (0, 0)),
        ],
        out_specs=pl.BlockSpec((1, tm, D_MODEL), tok),
        out_shape=jax.ShapeDtypeStruct((b, t, D_MODEL), F32),
        compiler_params=_params("parallel", "parallel"),
        name="final_norm",
    )(x1, moe, mod, final_g)


def _prep_layer(l, norm1_g, w_in, q_norm_g, w_uq, kv_norm_g, w_ukv, conv_w, conv_b, conv_ln_g,
                conv_ln_b, w_out, norm2_g, w_gate, w_up, w_down):
    win = w_in[l]
    o_kv, o_kr, o_a, o_g = Q_LORA, Q_LORA + KV_LORA, Q_LORA + KV_LORA + QK_ROPE, Q_LORA + KV_LORA + QK_ROPE + C_CONV
    wkr = win[:, o_kr:o_a]
    pad_r = ((0, 0), (0, HEAD_BLOCK - QK_ROPE))
    win_cat = jnp.concatenate(
        [win[:, :o_kv], win[:, o_kv:o_kr], win[:, o_a:o_g], win[:, o_g:],
         jnp.pad(wkr, pad_r)], axis=1).astype(BF16)

    wuq = w_uq[l].reshape(Q_LORA, N_HEADS, QK_NOPE + QK_ROPE)
    nope, rope = wuq[..., :QK_NOPE], wuq[..., QK_NOPE:]
    zpad = jnp.zeros((Q_LORA, N_HEADS, HEAD_BLOCK - QK_NOPE - QK_ROPE), F32)
    wuq_p = jnp.concatenate([rope, nope, zpad], axis=-1).reshape(Q_LORA, -1).astype(BF16)

    wukv = w_ukv[l].reshape(KV_LORA, N_HEADS, QK_NOPE + V_HEAD)
    kz0 = jnp.zeros((KV_LORA, N_HEADS, QK_ROPE), F32)
    kz1 = jnp.zeros((KV_LORA, N_HEADS, HEAD_BLOCK - QK_NOPE - QK_ROPE), F32)
    wuk = jnp.concatenate([kz0, wukv[..., :QK_NOPE], kz1], axis=-1).reshape(KV_LORA, -1).astype(BF16)
    wuv = wukv[..., QK_NOPE:].reshape(KV_LORA, -1).astype(BF16)
    vz = jnp.zeros((KV_LORA, N_HEADS, HEAD_BLOCK - V_HEAD), F32)
    wuv_ones = jnp.concatenate([wukv[..., QK_NOPE:], vz], axis=-1).reshape(KV_LORA, -1).astype(BF16)

    return {
        "n1g": norm1_g[l][None], "win": win_cat, "qg": q_norm_g[l][None], "wuq": wuq_p,
        "kvg": kv_norm_g[l][None], "wuk": wuk, "wuv": wuv, "wuv_ones": wuv_ones,
        "conv_w": conv_w[l], "conv_b": conv_b[l][None], "ln_g": conv_ln_g[l][None],
        "ln_b": conv_ln_b[l][None], "wout": w_out[l].astype(BF16), "n2g": norm2_g[l][None],
        "wg": w_gate[l].astype(BF16), "wu": w_up[l].astype(BF16), "wd": w_down[l].astype(BF16),
    }


def _rope_tables(n_tokens):
    rows = n_tokens // GRID_W
    row = jnp.repeat(jnp.arange(rows), GRID_W).astype(F32)
    col = jnp.tile(jnp.arange(GRID_W), rows).astype(F32)
    freqs = ROPE_BASE ** (-jnp.arange(ROPE_PAIRS, dtype=F32) / ROPE_PAIRS)
    ar, ac = row[:, None] * freqs, col[:, None] * freqs
    cr, sr, cc, sc = jnp.cos(ar), jnp.sin(ar), jnp.cos(ac), jnp.sin(ac)
    rest = HEAD_BLOCK - QK_ROPE
    cos = jnp.concatenate([cr, cr, cc, cc, jnp.ones((n_tokens, rest), F32)], axis=1)
    sin = jnp.concatenate([-sr, sr, -sc, sc, jnp.zeros((n_tokens, rest), F32)], axis=1)
    return cos, sin


TM_PROMPT = 256
PROMPT_SEQS_PER_STEP = 2
TM_SAMPLE = 512
TQ_SAMPLE = 512


def _fold(a):
    return None if a is None else a.reshape(a.shape[0] // 2, a.shape[1] * 2, a.shape[2])


def _unfold(a):
    return a.reshape(a.shape[0] * 2, a.shape[1] // 2, a.shape[2])


def kernel(x_prompt, x_sample, cache_ckv, cache_krope, c, c_ctx, w_ada, b_ada, norm1_g, w_in,
           q_norm_g, w_uq, kv_norm_g, w_ukv, conv_w, conv_b, conv_ln_g, conv_ln_b, w_out,
           norm2_g, router_w, router_b, w_gate, w_up, w_down, final_g):
    dec_b = x_sample.shape[0]
    cvecs = jnp.concatenate(
        [c_ctx[None], c, jnp.zeros((MOD_ROWS - 1 - dec_b, D_MODEL), F32)], axis=0)
    mod_all = _ada_call(cvecs, w_ada, b_ada).reshape(DEPTH, MOD_ROWS, N_MOD, D_MODEL)

    rwt = router_w.T
    rb = router_b[:, None]
    fg = final_g[None]
    rope_tabs = _rope_tables(x_sample.shape[1])
    cache_kr_pad = jnp.pad(cache_krope, ((0, 0), (0, 0), (0, 0), (0, HEAD_BLOCK - QK_ROPE)))

    xp, xs = x_prompt, x_sample
    moe_p = moe_s = mod_p = mod_s = None
    ckvs, krs = [], []
    for l in range(DEPTH):
        lw = _prep_layer(l, norm1_g, w_in, q_norm_g, w_uq, kv_norm_g, w_ukv, conv_w, conv_b,
                         conv_ln_g, conv_ln_b, w_out, norm2_g, w_gate, w_up, w_down)
        prev_p, prev_s = mod_p, mod_s
        mod_p = mod_all[l, 0:1]
        mod_s = mod_all[l, 1:1 + dec_b]

        outs = _premix_call(_fold(xp), mod_p, lw, TM_SAMPLE, moe=_fold(moe_p), mod_prev=prev_p)
        outs = [_unfold(o) for o in outs]
        if l > 0:
            q, k, v, z, xp, ckv, kr = outs
        else:
            q, k, v, z, ckv, kr = outs
        ckvs.append(ckv)
        krs.append(kr)
        attn = _attn_call(q, k, v, xp.shape[1])
        xp, h2, meta, ids = _postmix_call(xp, attn, z, mod_p, lw, rwt, rb, TM_PROMPT,
                                          nseq=PROMPT_SEQS_PER_STEP)
        moe_p = _routed_moe(h2, meta, ids, lw)

        outs = _premix_call(xs, mod_s, lw, TM_SAMPLE, rope_tabs, moe=moe_s, mod_prev=prev_s)
        if l > 0:
            q, k, v, z, xs = outs
        else:
            q, k, v, z = outs
        ctx_kv = _ctx_call(cache_ckv, cache_kr_pad, l, lw)
        attn = _attn_call(q, k, v, TQ_SAMPLE, ctx_kv)
        xs, h2, meta, ids = _postmix_call(xs, attn, z, mod_s, lw, rwt, rb, TM_SAMPLE)
        moe_s = _routed_moe(h2, meta, ids, lw)

    y_prompt = _unfold(_final_call(_fold(xp), _fold(moe_p), mod_p, fg, TM_SAMPLE))
    y_sample = _final_call(xs, moe_s, mod_s, fg, TM_SAMPLE)
    return y_prompt, y_sample, jnp.stack(ckvs, axis=1), jnp.stack(krs, axis=1)
```

```python
import functools

import jax
import jax.numpy as jnp
import numpy as np
from jax import lax
from jax.experimental import pallas as pl
from jax.experimental.pallas import tpu as pltpu
from jax.experimental.pallas import tpu_sc as plsc

D_MODEL = 1024
DEPTH = 4
GRID_W = 64
N_HEADS = 8
QK_NOPE = 64
QK_ROPE = 32
V_HEAD = 64
Q_LORA = 384
KV_LORA = 256
C_CONV = 512
CONV_K = 31
CONV_PAD = CONV_K // 2
N_EXPERTS = 16
N_GROUPS = 4
EXPERTS_PER_GROUP = N_EXPERTS // N_GROUPS
D_EXPERT = 256
ROPE_BASE = 10000.0
ROPE_PAIRS = QK_ROPE // 4
EPS = 1e-6
ATTN_SCALE = (QK_NOPE + QK_ROPE) ** -0.5
LOG2E = 1.4426950408889634

LANES = 128
SUBLANES = 8
HEAD_BLOCK = LANES
N_MOD = 6
MOD_ROWS = 16
HALO = 16
VMEM_LIMIT = 48 * 1024 * 1024

BF16 = jnp.bfloat16
F32 = jnp.float32


def _rms(x, g):
    return x * lax.rsqrt(jnp.mean(x * x, axis=-1, keepdims=True) + EPS) * g


def _silu(x):
    return x * jax.nn.sigmoid(x)


def _ones_lane(width):
    lane = lax.broadcasted_iota(jnp.int32, (1, width), 1)
    return jnp.where(lane % HEAD_BLOCK == V_HEAD, 1.0, 0.0)


def _pack_bf16_pairs(x):
    half = x.shape[1] // 2
    xb = x.astype(BF16).astype(F32)
    hi = pltpu.bitcast(xb[:, :half], jnp.uint32)
    lo = pltpu.bitcast(xb[:, half:], jnp.uint32)
    return pltpu.bitcast(hi | (lo >> 16), jnp.int32)


def _unpack_bf16_pairs(w, dtype=BF16):
    u = pltpu.bitcast(w, jnp.uint32)
    hi = pltpu.bitcast(u & jnp.uint32(0xFFFF0000), F32).astype(dtype)
    lo = pltpu.bitcast(u << 16, F32).astype(dtype)
    return jnp.concatenate([hi, lo], axis=1)


def _params(*sem):
    return pltpu.CompilerParams(dimension_semantics=sem, vmem_limit_bytes=VMEM_LIMIT)


def _ada_kernel(c_ref, w_ref, b_ref, o_ref):
    s = _silu(c_ref[...]).astype(BF16)
    o_ref[...] = jnp.dot(s, w_ref[...].astype(BF16), preferred_element_type=F32) + b_ref[...]


def _ada_call(cvecs, w_ada, b_ada):
    return pl.pallas_call(
        _ada_kernel,
        grid=(DEPTH, N_MOD),
        in_specs=[
            pl.BlockSpec((MOD_ROWS, D_MODEL), lambda l, n: (0, 0)),
            pl.BlockSpec((None, D_MODEL, D_MODEL), lambda l, n: (l, 0, n)),
            pl.BlockSpec((None, 1, D_MODEL), lambda l, n: (l, 0, n)),
        ],
        out_specs=pl.BlockSpec((None, MOD_ROWS, D_MODEL), lambda l, n: (l, 0, n)),
        out_shape=jax.ShapeDtypeStruct((DEPTH, MOD_ROWS, N_MOD * D_MODEL), F32),
        compiler_params=_params("parallel", "parallel"),
        name="ada",
    )(cvecs, w_ada, b_ada.reshape(DEPTH, 1, N_MOD * D_MODEL))


W_IN_Q = 0
W_IN_KV = W_IN_Q + Q_LORA
W_IN_A = W_IN_KV + KV_LORA
W_IN_G = W_IN_A + C_CONV
W_IN_KR = W_IN_G + C_CONV
W_IN_COLS = W_IN_KR + HEAD_BLOCK


def _swap_halves(x):
    lane = lax.broadcasted_iota(jnp.int32, x.shape, 1)
    first = lane % (2 * ROPE_PAIRS) < ROPE_PAIRS
    from_right = pltpu.roll(x, HEAD_BLOCK - ROPE_PAIRS, 1)
    from_left = pltpu.roll(x, ROPE_PAIRS, 1)
    return jnp.where(first, from_right, from_left)


def _premix_kernel(rope, resid, x_ref, *rest):
    if resid:
        moe_ref, modprev_ref, *rest = rest
    mod_ref, n1g_ref, win_ref, qg_ref, wuq_ref, kvg_ref, wuk_ref, wuv_ref, *rest = rest
    if rope:
        cos_ref, sin_ref, *rest = rest
    q_ref, k_ref, v_ref, z_ref, *rest = rest
    if resid:
        x2_ref, *rest = rest
    if not rope:
        ckv_ref, kr_ref = rest
    x = x_ref[0]
    if resid:
        x = x + modprev_ref[0][5:6] * _unpack_bf16_pairs(moe_ref[0], F32)
        x2_ref[0] = x
    mod = mod_ref[0]
    sh1, sc1 = mod[0:1], mod[1:2]
    h = (_rms(x, n1g_ref[...]) * (1.0 + sc1) + sh1).astype(BF16)
    proj = jnp.dot(h, win_ref[...], preferred_element_type=F32)
    q_c = proj[:, W_IN_Q:W_IN_KV]
    kv_c = proj[:, W_IN_KV:W_IN_A]
    a = proj[:, W_IN_A:W_IN_G]
    gate = proj[:, W_IN_G:W_IN_KR]
    kr = proj[:, W_IN_KR:W_IN_COLS]

    z_ref[0] = a * jax.nn.sigmoid(gate)

    qn = _rms(q_c, qg_ref[...]).astype(BF16)
    q = jnp.dot(qn, wuq_ref[...], preferred_element_type=F32)
    ckv = _rms(kv_c, kvg_ref[...])
    ckv_b = ckv.astype(BF16)
    kn = jnp.dot(ckv_b, wuk_ref[...], preferred_element_type=F32)
    v = jnp.dot(ckv_b, wuv_ref[...], preferred_element_type=F32)

    if rope:
        v = v + _ones_lane(v.shape[1])
        cos = cos_ref[...]
        sin = sin_ref[...]
        kr = kr * cos + _swap_halves(kr) * sin
    else:
        ckv_ref[0] = ckv
        kr_ref[0] = kr[:, :QK_ROPE]
    v_ref[0] = v.astype(BF16)

    for hd in range(N_HEADS):
        sl = slice(hd * HEAD_BLOCK, (hd + 1) * HEAD_BLOCK)
        qh = q[:, sl]
        if rope:
            qh = qh * cos + _swap_halves(qh) * sin
        q_ref[0, :, sl] = (qh * (ATTN_SCALE * LOG2E)).astype(BF16)
        k_ref[0, :, sl] = (kn[:, sl] + kr).astype(BF16)


def _premix_call(x, mod, lw, tm, rope_tabs=None, moe=None, mod_prev=None):
    b, t, _ = x.shape
    rope = rope_tabs is not None
    resid = moe is not None
    nt = t // tm
    per_b = (lambda i, j: (i, 0, 0)) if mod.shape[0] > 1 else (lambda i, j: (0, 0, 0))
    const2 = lambda i, j: (0, 0)
    tok = lambda i, j: (i, j, 0)
    v_cols = N_HEADS * (HEAD_BLOCK if rope else V_HEAD)
    in_specs = [pl.BlockSpec((1, tm, D_MODEL), tok)]
    args = [x]
    if resid:
        in_specs += [pl.BlockSpec((1, tm, D_MODEL // 2), tok),
                     pl.BlockSpec((1, N_MOD, D_MODEL), per_b)]
        args += [moe, mod_prev]
    in_specs += [
        pl.BlockSpec((1, N_MOD, D_MODEL), per_b),
        pl.BlockSpec((1, D_MODEL), const2),
        pl.BlockSpec((D_MODEL, W_IN_COLS), const2),
        pl.BlockSpec((1, Q_LORA), const2),
        pl.BlockSpec((Q_LORA, N_HEADS * HEAD_BLOCK), const2),
        pl.BlockSpec((1, KV_LORA), const2),
        pl.BlockSpec((KV_LORA, N_HEADS * HEAD_BLOCK), const2),
        pl.BlockSpec((KV_LORA, v_cols), const2),
    ]
    args += [mod, lw["n1g"], lw["win"], lw["qg"], lw["wuq"], lw["kvg"], lw["wuk"],
             lw["wuv_ones"] if rope else lw["wuv"]]
    out_specs = [
        pl.BlockSpec((1, tm, N_HEADS * HEAD_BLOCK), tok),
        pl.BlockSpec((1, tm, N_HEADS * HEAD_BLOCK), tok),
        pl.BlockSpec((1, tm, v_cols), tok),
        pl.BlockSpec((1, tm, C_CONV), tok),
    ]
    out_shape = [
        jax.ShapeDtypeStruct((b, t, N_HEADS * HEAD_BLOCK), BF16),
        jax.ShapeDtypeStruct((b, t, N_HEADS * HEAD_BLOCK), BF16),
        jax.ShapeDtypeStruct((b, t, v_cols), BF16),
        jax.ShapeDtypeStruct((b, t, C_CONV), F32),
    ]
    if resid:
        out_specs.append(pl.BlockSpec((1, tm, D_MODEL), tok))
        out_shape.append(jax.ShapeDtypeStruct((b, t, D_MODEL), F32))
    if rope:
        in_specs += [
            pl.BlockSpec((tm, HEAD_BLOCK), lambda i, j: (j, 0)),
            pl.BlockSpec((tm, HEAD_BLOCK), lambda i, j: (j, 0)),
        ]
        args += [rope_tabs[0], rope_tabs[1]]
    else:
        out_specs += [
            pl.BlockSpec((1, tm, KV_LORA), tok),
            pl.BlockSpec((1, tm, QK_ROPE), tok),
        ]
        out_shape += [
            jax.ShapeDtypeStruct((b, t, KV_LORA), F32),
            jax.ShapeDtypeStruct((b, t, QK_ROPE), F32),
        ]
    return pl.pallas_call(
        functools.partial(_premix_kernel, rope, resid),
        grid=(b, nt),
        in_specs=in_specs,
        out_specs=out_specs,
        out_shape=out_shape,
        compiler_params=_params("parallel", "parallel"),
        name="premix_rope" if rope else "premix",
    )(*args)


def _ctx_kernel(ckv_ref, kr_ref, wuk_ref, wuv_ref, k_ref, v_ref):
    ckv_b = ckv_ref[...].astype(BF16)
    kn = jnp.dot(ckv_b, wuk_ref[...], preferred_element_type=F32)
    v = jnp.dot(ckv_b, wuv_ref[...], preferred_element_type=F32)
    v_ref[0] = (v + _ones_lane(v.shape[1])).astype(BF16)
    kr = kr_ref[...]
    for hd in range(N_HEADS):
        sl = slice(hd * HEAD_BLOCK, (hd + 1) * HEAD_BLOCK)
        k_ref[0, :, sl] = (kn[:, sl] + kr).astype(BF16)


def _ctx_call(cache_ckv, cache_kr_pad, layer, lw):
    b, _, s, _ = cache_ckv.shape
    const2 = lambda i: (0, 0)
    return pl.pallas_call(
        _ctx_kernel,
        grid=(b,),
        in_specs=[
            pl.BlockSpec((None, None, s, KV_LORA), lambda i: (i, layer, 0, 0)),
            pl.BlockSpec((None, None, s, HEAD_BLOCK), lambda i: (i, layer, 0, 0)),
            pl.BlockSpec((KV_LORA, N_HEADS * HEAD_BLOCK), const2),
            pl.BlockSpec((KV_LORA, N_HEADS * HEAD_BLOCK), const2),
        ],
        out_specs=[
            pl.BlockSpec((1, s, N_HEADS * HEAD_BLOCK), lambda i: (i, 0, 0)),
            pl.BlockSpec((1, s, N_HEADS * HEAD_BLOCK), lambda i: (i, 0, 0)),
        ],
        out_shape=[
            jax.ShapeDtypeStruct((b, s, N_HEADS * HEAD_BLOCK), BF16),
            jax.ShapeDtypeStruct((b, s, N_HEADS * HEAD_BLOCK), BF16),
        ],
        compiler_params=_params("parallel"),
        name="ctx_kv",
    )(cache_ckv, cache_kr_pad, lw["wuk"], lw["wuv_ones"])


_NT = (((1,), (1,)), ((), ()))


def _attn_kernel(ctx, q_ref, k_ref, v_ref, *rest):
    if ctx:
        kc_ref, vc_ref, o_ref = rest
    else:
        (o_ref,) = rest
    outs = []
    for hd in range(N_HEADS):
        sl = slice(hd * HEAD_BLOCK, (hd + 1) * HEAD_BLOCK)
        qh = q_ref[0, :, sl]
        s = lax.dot_general(qh, k_ref[0, :, sl], _NT, preferred_element_type=F32)
        m = jnp.max(s, axis=-1, keepdims=True)
        if ctx:
            sc = lax.dot_general(qh, kc_ref[0, :, sl], _NT, preferred_element_type=F32)
            m = jnp.maximum(m, jnp.max(sc, axis=-1, keepdims=True))
        p = jnp.exp2(s - m)
        if ctx:
            pc = jnp.exp2(sc - m)
            o = jnp.dot(p.astype(BF16), v_ref[0, :, sl], preferred_element_type=F32)
            o = o + jnp.dot(pc.astype(BF16), vc_ref[0, :, sl], preferred_element_type=F32)
            outs.append(o[:, :V_HEAD] / o[:, V_HEAD:V_HEAD + 1])
        else:
            vs = slice(hd * V_HEAD, (hd + 1) * V_HEAD)
            o = jnp.dot(p.astype(BF16), v_ref[0, :, vs], preferred_element_type=F32)
            outs.append(o / jnp.sum(p, axis=-1, keepdims=True))
    o_ref[0] = jnp.concatenate(outs, axis=-1).astype(BF16)


def _attn_call(q, k, v, tq, ctx_kv=None):
    b, t, _ = q.shape
    ctx = ctx_kv is not None
    tok = lambda i, j: (i, j, 0)
    seq = lambda i, j: (i, 0, 0)
    in_specs = [
        pl.BlockSpec((1, tq, N_HEADS * HEAD_BLOCK), tok),
        pl.BlockSpec((1, t, N_HEADS * HEAD_BLOCK), seq),
        pl.BlockSpec((1, t, v.shape[2]), seq),
    ]
    args = [q, k, v]
    if ctx:
        s = ctx_kv[0].shape[1]
        in_specs += [
            pl.BlockSpec((1, s, N_HEADS * HEAD_BLOCK), seq),
            pl.BlockSpec((1, s, N_HEADS * HEAD_BLOCK), seq),
        ]
        args += list(ctx_kv)
    return pl.pallas_call(
        functools.partial(_attn_kernel, ctx),
        grid=(b, t // tq),
        in_specs=in_specs,
        out_specs=pl.BlockSpec((1, tq, N_HEADS * V_HEAD), tok),
        out_shape=jax.ShapeDtypeStruct((b, t, N_HEADS * V_HEAD), BF16),
        compiler_params=_params("parallel", "arbitrary"),
        name="attn_ctx" if ctx else "attn",
    )(*args)


CONV_ROWS = 64


def _route(aff, bias):
    sel = aff + bias
    rows = [sel[e:e + 1] for e in range(N_EXPERTS)]
    affr = [aff[e:e + 1] for e in range(N_EXPERTS)]
    scores = []
    for g in range(N_GROUPS):
        a, b, c, d = rows[g * EXPERTS_PER_GROUP:(g + 1) * EXPERTS_PER_GROUP]
        hi1, lo1 = jnp.maximum(a, b), jnp.minimum(a, b)
        hi2, lo2 = jnp.maximum(c, d), jnp.minimum(c, d)
        top1 = jnp.maximum(hi1, hi2)
        top2 = jnp.maximum(jnp.minimum(hi1, hi2), jnp.maximum(lo1, lo2))
        scores.append(top1 + top2)
    best = scores[0]
    gidx = jnp.zeros_like(best, dtype=jnp.int32)
    for g in range(1, N_GROUPS):
        better = scores[g] > best
        best = jnp.where(better, scores[g], best)
        gidx = jnp.where(better, g, gidx)
    gates = []
    e_lo = jnp.full_like(best, float(N_EXPERTS))
    e_hi = jnp.full_like(best, -1.0)
    for g in range(N_GROUPS):
        in_g = gidx == g
        s = rows[g * EXPERTS_PER_GROUP:(g + 1) * EXPERTS_PER_GROUP]
        af = affr[g * EXPERTS_PER_GROUP:(g + 1) * EXPERTS_PER_GROUP]
        picked = []
        for i in range(EXPERTS_PER_GROUP):
            rank = jnp.zeros_like(gidx)
            for j in range(EXPERTS_PER_GROUP):
                if j == i:
                    continue
                ahead = (s[j] >= s[i]) if j < i else (s[j] > s[i])
                rank = rank + ahead.astype(jnp.int32)
            chosen = (rank < 2) & in_g
            eid = float(g * EXPERTS_PER_GROUP + i)
            e_lo = jnp.where(chosen, jnp.minimum(e_lo, eid), e_lo)
            e_hi = jnp.where(chosen, jnp.maximum(e_hi, eid), e_hi)
            picked.append(jnp.where(chosen, af[i], 0.0))
        den = (picked[0] + picked[1]) + (picked[2] + picked[3])
        den = jnp.where(in_g, den, 1.0)
        gates += [pk / den for pk in picked]
    return jnp.concatenate(gates, axis=0), e_lo, e_hi


def _postmix_kernel(nt, x_ref, attn_ref, z_ref, mod_ref, cw_ref, cb_ref, lg_ref, lb_ref, wout_ref,
                    n2g_ref, rwt_ref, rb_ref, x1_ref, h2_ref, meta_ref, ids_ref, zp_ref, zs_ref, cz_ref):
    nseq, tm = x_ref.shape[0], x_ref.shape[1]
    rows = nseq * tm
    j = pl.program_id(1)
    t0 = pl.multiple_of(j * tm, tm)
    zeros = jnp.zeros((HALO, C_CONV), F32)
    cw = cw_ref[...]
    cb = cb_ref[...]
    span = zs_ref.shape[1]

    for sq in range(nseq):
        zp_ref[HALO:HALO + tm, :] = z_ref[sq, pl.ds(t0, tm), :]

        @pl.when(j == 0)
        def _():
            zp_ref[0:HALO, :] = zeros

        @pl.when(j > 0)
        def _(sq=sq):
            zp_ref[0:HALO, :] = z_ref[sq, pl.ds(t0 - HALO, HALO), :]

        @pl.when(j == nt - 1)
        def _():
            zp_ref[HALO + tm:, :] = zeros

        @pl.when(j < nt - 1)
        def _(sq=sq):
            zp_ref[HALO + tm:, :] = z_ref[sq, pl.ds(t0 + tm, HALO), :]

        for o in range(1, SUBLANES):
            zs_ref[o - 1] = zp_ref[o:o + span, :]

        for r0 in range(0, tm, CONV_ROWS):
            acc = jnp.broadcast_to(cb, (CONV_ROWS, C_CONV))
            for kk in range(CONV_K):
                off = HALO - CONV_PAD + kk
                o = off % SUBLANES
                base = r0 + off - o
                if o == 0:
                    tap = zp_ref[base:base + CONV_ROWS, :]
                else:
                    tap = zs_ref[o - 1, base:base + CONV_ROWS, :]
                acc = acc + tap * cw[kk:kk + 1, :]
            cz_ref[sq * tm + r0:sq * tm + r0 + CONV_ROWS, :] = acc

    cz = cz_ref[...]
    mu = jnp.mean(cz, axis=-1, keepdims=True)
    cen = cz - mu
    var = jnp.mean(cen * cen, axis=-1, keepdims=True)
    conv = _silu(cen * lax.rsqrt(var + EPS) * lg_ref[...] + lb_ref[...]).astype(BF16)

    n_attn = N_HEADS * V_HEAD
    y = jnp.dot(attn_ref[...].reshape(rows, n_attn), wout_ref[0:n_attn, :],
                preferred_element_type=F32)
    y = y + jnp.dot(conv, wout_ref[n_attn:, :], preferred_element_type=F32)

    mod = mod_ref[0]
    g1, sh2, sc2 = mod[2:3], mod[3:4], mod[4:5]
    x1 = x_ref[...].reshape(rows, D_MODEL) + g1 * y
    x1_ref[...] = x1.reshape(x1_ref.shape)
    h2 = _rms(x1, n2g_ref[...]) * (1.0 + sc2) + sh2
    h2_ref[...] = _pack_bf16_pairs(h2).reshape(h2_ref.shape)

    logits = lax.dot_general(rwt_ref[...], h2, _NT, precision=lax.Precision.HIGHEST,
                             preferred_element_type=F32)
    gates, e_lo, e_hi = _route(jax.nn.sigmoid(logits), rb_ref[...])
    row = lax.broadcasted_iota(jnp.int32, (SUBLANES, rows), 0)
    ids = jnp.where(row == 0, e_lo, jnp.where(row == 1, e_hi, 0.0))
    ids_ref[...] = ids
    rec = jnp.concatenate([gates, ids, jnp.zeros((LANES - N_EXPERTS - SUBLANES, rows), F32)], axis=0)
    meta_ref[...] = rec.T.reshape(meta_ref.shape)


def _postmix_call(x, attn, z, mod, lw, rwt, rb, tm, nseq=1):
    b, t, _ = x.shape
    nt = t // tm
    assert nseq == 1 or (nt == 1 and mod.shape[0] == 1)
    per_b = (lambda i, j: (i, 0, 0)) if mod.shape[0] > 1 else (lambda i, j: (0, 0, 0))
    const2 = lambda i, j: (0, 0)
    tok = lambda i, j: (i, j, 0)
    return pl.pallas_call(
        functools.partial(_postmix_kernel, nt),
        grid=(b // nseq, nt),
        in_specs=[
            pl.BlockSpec((nseq, tm, D_MODEL), tok),
            pl.BlockSpec((nseq, tm, N_HEADS * V_HEAD), tok),
            pl.BlockSpec((nseq, t, C_CONV), lambda i, j: (i, 0, 0)),
            pl.BlockSpec((1, N_MOD, D_MODEL), per_b),
            pl.BlockSpec((CONV_K, C_CONV), const2),
            pl.BlockSpec((1, C_CONV), const2),
            pl.BlockSpec((1, C_CONV), const2),
            pl.BlockSpec((1, C_CONV), const2),
            pl.BlockSpec((D_MODEL, D_MODEL), const2),
            pl.BlockSpec((1, D_MODEL), const2),
            pl.BlockSpec((N_EXPERTS, D_MODEL), const2),
            pl.BlockSpec((N_EXPERTS, 1), const2),
        ],
        out_specs=[
            pl.BlockSpec((nseq, tm, D_MODEL), tok),
            pl.BlockSpec((nseq, tm, D_MODEL // 2), tok),
            pl.BlockSpec((nseq, tm, LANES), tok),
            pl.BlockSpec((SUBLANES, nseq * tm), lambda i, j: (0, i * nt + j)),
        ],
        out_shape=[
            jax.ShapeDtypeStruct((b, t, D_MODEL), F32),
            jax.ShapeDtypeStruct((b, t, D_MODEL // 2), jnp.int32),
            jax.ShapeDtypeStruct((b, t, LANES), F32),
            jax.ShapeDtypeStruct((SUBLANES, b * t), F32),
        ],
        scratch_shapes=[
            pltpu.VMEM((tm + 2 * HALO, C_CONV), F32),
            pltpu.VMEM((SUBLANES - 1, tm + 2 * HALO - SUBLANES, C_CONV), F32),
            pltpu.VMEM((nseq * tm, C_CONV), F32),
        ],
        compiler_params=_params("parallel", "arbitrary"),
        name="postmix",
    )(x, attn, z, mod, lw["conv_w"], lw["conv_b"], lw["ln_g"], lw["ln_b"], lw["wout"],
      lw["n2g"], rwt, rb)


N_PAIRS = EXPERTS_PER_GROUP * (EXPERTS_PER_GROUP - 1) // 2
N_CLASSES = N_GROUPS * N_PAIRS
_PAIR_LO = np.array([i for i in range(EXPERTS_PER_GROUP) for j in range(i + 1, EXPERTS_PER_GROUP)])
_PAIR_HI = np.array([j for i in range(EXPERTS_PER_GROUP) for j in range(i + 1, EXPERTS_PER_GROUP)])
MOE_ROWS = 256
MOE_STEP_TILES = 4
SC_CORES = 2
SC_SUBCORES = 16
SC_WORKERS = SC_CORES * SC_SUBCORES
SC_WINDOW = 32


def _slot_rows(n):
    return n + N_CLASSES * MOE_ROWS


def _plan(ids, n_slots):
    e_lo = ids[0].astype(jnp.int32)
    e_hi = ids[1].astype(jnp.int32)
    grp = e_lo // EXPERTS_PER_GROUP
    i = e_lo % EXPERTS_PER_GROUP
    j = e_hi % EXPERTS_PER_GROUP
    cls = grp * N_PAIRS + (i * (2 * EXPERTS_PER_GROUP - 1 - i)) // 2 + (j - i - 1)
    onehot = (cls[None, :] == jnp.arange(N_CLASSES, dtype=jnp.int32)[:, None]).astype(jnp.int32)
    csum = jnp.cumsum(onehot, axis=1)
    counts = csum[:, -1]
    padded = (counts + MOE_ROWS - 1) // MOE_ROWS * MOE_ROWS
    pend = jnp.cumsum(padded)
    pstart = pend - padded
    pos = jnp.sum(onehot * (csum - 1 + pstart[:, None]), axis=0).astype(jnp.int32)
    tile_start = jnp.arange(n_slots // MOE_ROWS, dtype=jnp.int32) * MOE_ROWS
    tcls = jnp.minimum(jnp.sum(tile_start[:, None] >= pend[None, :], axis=1), N_CLASSES - 1)
    tgrp, tpair = tcls // N_PAIRS, tcls % N_PAIRS
    t_lo = tgrp * EXPERTS_PER_GROUP + jnp.asarray(_PAIR_LO, jnp.int32)[tpair]
    t_hi = tgrp * EXPERTS_PER_GROUP + jnp.asarray(_PAIR_HI, jnp.int32)[tpair]
    tile_experts = jnp.stack([t_lo, t_hi]).astype(jnp.int32)
    n_used = (pend[-1] // MOE_ROWS).astype(jnp.int32).reshape(1)
    return pos, tile_experts, n_used


def _sc_mesh():
    return plsc.VectorSubcoreMesh(core_axis_name="core", subcore_axis_name="subcore")


def _sc_worker():
    return lax.axis_index("core") * SC_SUBCORES + lax.axis_index("subcore")


def _dispatch_call(h2, meta, pos, n_slots):
    n = h2.shape[0]
    per_worker = n // SC_WORKERS

    @pl.kernel(
        out_type=[jax.ShapeDtypeStruct((n_slots, h2.shape[1]), h2.dtype),
                  jax.ShapeDtypeStruct((n_slots, LANES), F32)],
        mesh=_sc_mesh(),
        scratch_types=[pltpu.VMEM((per_worker,), jnp.int32),
                       pltpu.VMEM((SC_WINDOW, h2.shape[1]), h2.dtype),
                       pltpu.VMEM((SC_WINDOW, LANES), F32)],
    )
    def dispatch(h2_hbm, meta_hbm, pos_hbm, xs_hbm, ms_hbm, slot_v, row_v, rec_v):
        wid = _sc_worker()
        base = wid * per_worker
        pltpu.sync_copy(pos_hbm.at[wid], slot_v)

        @pl.loop(0, per_worker // SC_WINDOW)
        def _(step):
            rows = pl.ds(base + step * SC_WINDOW, SC_WINDOW)
            slots = slot_v.at[pl.ds(step * SC_WINDOW, SC_WINDOW)]
            pltpu.sync_copy(h2_hbm.at[rows], row_v)
            pltpu.sync_copy(row_v, xs_hbm.at[slots])
            pltpu.sync_copy(meta_hbm.at[rows], rec_v)
            pltpu.sync_copy(rec_v, ms_hbm.at[slots])

    return dispatch(h2, meta, pos.reshape(SC_WORKERS, per_worker))


def _combine_call(ys, pos):
    n = pos.shape[0]
    per_worker = n // SC_WORKERS

    @pl.kernel(
        out_type=jax.ShapeDtypeStruct((n, ys.shape[1]), ys.dtype),
        mesh=_sc_mesh(),
        scratch_types=[pltpu.VMEM((per_worker,), jnp.int32),
                       pltpu.VMEM((SC_WINDOW, ys.shape[1]), ys.dtype)],
    )
    def combine(ys_hbm, pos_hbm, out_hbm, slot_v, row_v):
        wid = _sc_worker()
        base = wid * per_worker
        pltpu.sync_copy(pos_hbm.at[wid], slot_v)

        @pl.loop(0, per_worker // SC_WINDOW)
        def _(step):
            slots = slot_v.at[pl.ds(step * SC_WINDOW, SC_WINDOW)]
            pltpu.sync_copy(ys_hbm.at[slots], row_v)
            pltpu.sync_copy(row_v, out_hbm.at[pl.ds(base + step * SC_WINDOW, SC_WINDOW)])

    return combine(ys, pos.reshape(SC_WORKERS, per_worker))


def _last_used_step(nu_ref):
    return jnp.maximum((nu_ref[0] - 1) // MOE_STEP_TILES, 0)


def _experts_kernel(te_ref, nu_ref, xs_ref, ms_ref, wg_ref, wu_ref, wd_ref, ys_ref):
    step = pl.program_id(0)
    live = step <= _last_used_step(nu_ref)
    for sub in range(MOE_STEP_TILES):
        tile = step * MOE_STEP_TILES + sub
        rows = slice(sub * MOE_ROWS, (sub + 1) * MOE_ROWS)

        @pl.when(tile < nu_ref[0])
        def _(tile=tile, rows=rows):
            x = _unpack_bf16_pairs(xs_ref[rows, :])
            rec = ms_ref[rows, :]
            lane = lax.broadcasted_iota(jnp.int32, rec.shape, 1)

            def expert(e):
                gate = jnp.sum(jnp.where(lane == e, rec, 0.0), axis=-1, keepdims=True)
                hg = jnp.dot(x, wg_ref[e], preferred_element_type=F32)
                hu = jnp.dot(x, wu_ref[e], preferred_element_type=F32)
                hid = (_silu(hg) * hu * gate).astype(BF16)
                return jnp.dot(hid, wd_ref[e], preferred_element_type=F32)

            y = expert(te_ref[0, tile]) + expert(te_ref[1, tile])
            ys_ref[rows, :] = _pack_bf16_pairs(y)

        @pl.when((tile >= nu_ref[0]) & live)
        def _(rows=rows):
            ys_ref[rows, :] = jnp.zeros((MOE_ROWS, D_MODEL // 2), jnp.int32)


def _experts_call(xs, ms, tile_experts, n_used, layer, moe_w):
    n_slots = xs.shape[0]
    step_rows = MOE_ROWS * MOE_STEP_TILES
    rows = lambda i, te, nu: (jnp.minimum(i, _last_used_step(nu)), 0)

    def resident(w):
        return pl.BlockSpec((None,) + w.shape[1:], lambda i, te, nu: (layer, 0, 0, 0),
                            pipeline_mode=pl.Buffered(1))

    wg, wu, wd = moe_w
    return pl.pallas_call(
        _experts_kernel,
        grid_spec=pltpu.PrefetchScalarGridSpec(
            num_scalar_prefetch=2,
            grid=(n_slots // step_rows,),
            in_specs=[
                pl.BlockSpec((step_rows, D_MODEL // 2), rows),
                pl.BlockSpec((step_rows, LANES), rows),
                resident(wg), resident(wu), resident(wd),
            ],
            out_specs=pl.BlockSpec((step_rows, D_MODEL // 2), rows),
        ),
        out_shape=jax.ShapeDtypeStruct((n_slots, D_MODEL // 2), jnp.int32),
        compiler_params=_params("arbitrary"),
        name="experts",
    )(tile_experts, n_used, xs, ms, wg, wu, wd)


def _routed_moe(h2, meta, ids, layer, moe_w):
    b, t, _ = h2.shape
    n = b * t
    n_slots = _slot_rows(n)
    pos, tile_experts, n_used = _plan(ids, n_slots)
    xs, ms = _dispatch_call(h2.reshape(n, h2.shape[2]), meta.reshape(n, LANES), pos, n_slots)
    ys = _experts_call(xs, ms, tile_experts, n_used, layer, moe_w)
    return _combine_call(ys, pos).reshape(b, t, ys.shape[1])


def _final_kernel(x1_ref, moe_ref, mod_ref, fg_ref, o_ref):
    x2 = x1_ref[0] + mod_ref[0][5:6] * _unpack_bf16_pairs(moe_ref[0], F32)
    o_ref[0] = _rms(x2, fg_ref[...])


def _final_call(x1, moe, mod, final_g, tm):
    b, t, _ = x1.shape
    per_b = (lambda i, j: (i, 0, 0)) if mod.shape[0] > 1 else (lambda i, j: (0, 0, 0))
    tok = lambda i, j: (i, j, 0)
    return pl.pallas_call(
        _final_kernel,
        grid=(b, t // tm),
        in_specs=[
            pl.BlockSpec((1, tm, D_MODEL), tok),
            pl.BlockSpec((1, tm, D_MODEL // 2), tok),
            pl.BlockSpec((1, N_MOD, D_MODEL), per_b),
            pl.BlockSpec((1, D_MODEL), lambda i, j: (0, 0)),
        ],
        out_specs=pl.BlockSpec((1, tm, D_MODEL), tok),
        out_shape=jax.ShapeDtypeStruct((b, t, D_MODEL), F32),
        compiler_params=_params("parallel", "parallel"),
        name="final_norm",
    )(x1, moe, mod, final_g)


def _prep_layer(l, norm1_g, w_in, q_norm_g, w_uq, kv_norm_g, w_ukv, conv_w, conv_b, conv_ln_g,
                conv_ln_b, w_out, norm2_g):
    win = w_in[l]
    o_kv, o_kr, o_a, o_g = Q_LORA, Q_LORA + KV_LORA, Q_LORA + KV_LORA + QK_ROPE, Q_LORA + KV_LORA + QK_ROPE + C_CONV
    wkr = win[:, o_kr:o_a]
    pad_r = ((0, 0), (0, HEAD_BLOCK - QK_ROPE))
    win_cat = jnp.concatenate(
        [win[:, :o_kv], win[:, o_kv:o_kr], win[:, o_a:o_g], win[:, o_g:],
         jnp.pad(wkr, pad_r)], axis=1).astype(BF16)

    wuq = w_uq[l].reshape(Q_LORA, N_HEADS, QK_NOPE + QK_ROPE)
    nope, rope = wuq[..., :QK_NOPE], wuq[..., QK_NOPE:]
    zpad = jnp.zeros((Q_LORA, N_HEADS, HEAD_BLOCK - QK_NOPE - QK_ROPE), F32)
    wuq_p = jnp.concatenate([rope, nope, zpad], axis=-1).reshape(Q_LORA, -1).astype(BF16)

    wukv = w_ukv[l].reshape(KV_LORA, N_HEADS, QK_NOPE + V_HEAD)
    kz0 = jnp.zeros((KV_LORA, N_HEADS, QK_ROPE), F32)
    kz1 = jnp.zeros((KV_LORA, N_HEADS, HEAD_BLOCK - QK_NOPE - QK_ROPE), F32)
    wuk = jnp.concatenate([kz0, wukv[..., :QK_NOPE], kz1], axis=-1).reshape(KV_LORA, -1).astype(BF16)
    wuv = wukv[..., QK_NOPE:].reshape(KV_LORA, -1).astype(BF16)
    vz = jnp.zeros((KV_LORA, N_HEADS, HEAD_BLOCK - V_HEAD), F32)
    wuv_ones = jnp.concatenate([wukv[..., QK_NOPE:], vz], axis=-1).reshape(KV_LORA, -1).astype(BF16)

    return {
        "n1g": norm1_g[l][None], "win": win_cat, "qg": q_norm_g[l][None], "wuq": wuq_p,
        "kvg": kv_norm_g[l][None], "wuk": wuk, "wuv": wuv, "wuv_ones": wuv_ones,
        "conv_w": conv_w[l], "conv_b": conv_b[l][None], "ln_g": conv_ln_g[l][None],
        "ln_b": conv_ln_b[l][None], "wout": w_out[l].astype(BF16), "n2g": norm2_g[l][None],
    }


def _rope_tables(n_tokens):
    rows = n_tokens // GRID_W
    row = jnp.repeat(jnp.arange(rows), GRID_W).astype(F32)
    col = jnp.tile(jnp.arange(GRID_W), rows).astype(F32)
    freqs = ROPE_BASE ** (-jnp.arange(ROPE_PAIRS, dtype=F32) / ROPE_PAIRS)
    ar, ac = row[:, None] * freqs, col[:, None] * freqs
    cr, sr, cc, sc = jnp.cos(ar), jnp.sin(ar), jnp.cos(ac), jnp.sin(ac)
    rest = HEAD_BLOCK - QK_ROPE
    cos = jnp.concatenate([cr, cr, cc, cc, jnp.ones((n_tokens, rest), F32)], axis=1)
    sin = jnp.concatenate([-sr, sr, -sc, sc, jnp.zeros((n_tokens, rest), F32)], axis=1)
    return cos, sin


TM_PROMPT = 256
PROMPT_SEQS_PER_STEP = 2
TM_SAMPLE = 512
TQ_SAMPLE = 512


def _fold(a):
    return None if a is None else a.reshape(a.shape[0] // 2, a.shape[1] * 2, a.shape[2])


def _unfold(a):
    return a.reshape(a.shape[0] * 2, a.shape[1] // 2, a.shape[2])


def kernel(x_prompt, x_sample, cache_ckv, cache_krope, c, c_ctx, w_ada, b_ada, norm1_g, w_in,
           q_norm_g, w_uq, kv_norm_g, w_ukv, conv_w, conv_b, conv_ln_g, conv_ln_b, w_out,
           norm2_g, router_w, router_b, w_gate, w_up, w_down, final_g):
    dec_b = x_sample.shape[0]
    cvecs = jnp.concatenate(
        [c_ctx[None], c, jnp.zeros((MOD_ROWS - 1 - dec_b, D_MODEL), F32)], axis=0)
    mod_all = _ada_call(cvecs, w_ada, b_ada).reshape(DEPTH, MOD_ROWS, N_MOD, D_MODEL)

    moe_w = (w_gate.astype(BF16), w_up.astype(BF16), w_down.astype(BF16))
    rwt = router_w.T
    rb = router_b[:, None]
    fg = final_g[None]
    rope_tabs = _rope_tables(x_sample.shape[1])
    cache_kr_pad = jnp.pad(cache_krope, ((0, 0), (0, 0), (0, 0), (0, HEAD_BLOCK - QK_ROPE)))

    xp, xs = x_prompt, x_sample
    moe_p = moe_s = mod_p = mod_s = None
    ckvs, krs = [], []
    for l in range(DEPTH):
        lw = _prep_layer(l, norm1_g, w_in, q_norm_g, w_uq, kv_norm_g, w_ukv, conv_w, conv_b,
                         conv_ln_g, conv_ln_b, w_out, norm2_g)
        prev_p, prev_s = mod_p, mod_s
        mod_p = mod_all[l, 0:1]
        mod_s = mod_all[l, 1:1 + dec_b]

        outs = _premix_call(_fold(xp), mod_p, lw, TM_SAMPLE, moe=_fold(moe_p), mod_prev=prev_p)
        outs = [_unfold(o) for o in outs]
        if l > 0:
            q, k, v, z, xp, ckv, kr = outs
        else:
            q, k, v, z, ckv, kr = outs
        ckvs.append(ckv)
        krs.append(kr)
        attn = _attn_call(q, k, v, xp.shape[1])
        xp, h2, meta, ids = _postmix_call(xp, attn, z, mod_p, lw, rwt, rb, TM_PROMPT,
                                          nseq=PROMPT_SEQS_PER_STEP)
        moe_p = _routed_moe(h2, meta, ids, l, moe_w)

        outs = _premix_call(xs, mod_s, lw, TM_SAMPLE, rope_tabs, moe=moe_s, mod_prev=prev_s)
        if l > 0:
            q, k, v, z, xs = outs
        else:
            q, k, v, z = outs
        ctx_kv = _ctx_call(cache_ckv, cache_kr_pad, l, lw)
        attn = _attn_call(q, k, v, TQ_SAMPLE, ctx_kv)
        xs, h2, meta, ids = _postmix_call(xs, attn, z, mod_s, lw, rwt, rb, TM_SAMPLE)
        moe_s = _routed_moe(h2, meta, ids, l, moe_w)

    y_prompt = _unfold(_final_call(_fold(xp), _fold(moe_p), mod_p, fg, TM_SAMPLE))
    y_sample = _final_call(xs, moe_s, mod_s, fg, TM_SAMPLE)
    return y_prompt, y_sample, jnp.stack(ckvs, axis=1), jnp.stack(krs, axis=1)
```

```python
import functools

import jax
import jax.numpy as jnp
import numpy as np
from jax import lax
from jax.experimental import pallas as pl
from jax.experimental.pallas import tpu as pltpu
from jax.experimental.pallas import tpu_sc as plsc

D_MODEL = 1024
DEPTH = 4
GRID_W = 64
N_HEADS = 8
QK_NOPE = 64
QK_ROPE = 32
V_HEAD = 64
Q_LORA = 384
KV_LORA = 256
C_CONV = 512
CONV_K = 31
CONV_PAD = CONV_K // 2
N_EXPERTS = 16
N_GROUPS = 4
EXPERTS_PER_GROUP = N_EXPERTS // N_GROUPS
D_EXPERT = 256
ROPE_BASE = 10000.0
ROPE_PAIRS = QK_ROPE // 4
EPS = 1e-6
ATTN_SCALE = (QK_NOPE + QK_ROPE) ** -0.5
LOG2E = 1.4426950408889634

LANES = 128
SUBLANES = 8
HEAD_BLOCK = LANES
N_MOD = 6
MOD_ROWS = 16
HALO = 16
VMEM_LIMIT = 48 * 1024 * 1024

BF16 = jnp.bfloat16
F32 = jnp.float32


def _rms(x, g):
    return x * lax.rsqrt(jnp.mean(x * x, axis=-1, keepdims=True) + EPS) * g


def _silu(x):
    return x * jax.nn.sigmoid(x)


def _ones_lane(width):
    lane = lax.broadcasted_iota(jnp.int32, (1, width), 1)
    return jnp.where(lane % HEAD_BLOCK == V_HEAD, 1.0, 0.0)


def _pack_bf16_pairs(x):
    half = x.shape[1] // 2
    xb = x.astype(BF16).astype(F32)
    hi = pltpu.bitcast(xb[:, :half], jnp.uint32)
    lo = pltpu.bitcast(xb[:, half:], jnp.uint32)
    return pltpu.bitcast(hi | (lo >> 16), jnp.int32)


def _unpack_bf16_pairs(w, dtype=BF16):
    u = pltpu.bitcast(w, jnp.uint32)
    hi = pltpu.bitcast(u & jnp.uint32(0xFFFF0000), F32).astype(dtype)
    lo = pltpu.bitcast(u << 16, F32).astype(dtype)
    return jnp.concatenate([hi, lo], axis=1)


def _params(*sem):
    return pltpu.CompilerParams(dimension_semantics=sem, vmem_limit_bytes=VMEM_LIMIT)


def _ada_kernel(c_ref, w_ref, b_ref, o_ref):
    s = _silu(c_ref[...]).astype(BF16)
    o_ref[...] = jnp.dot(s, w_ref[...].astype(BF16), preferred_element_type=F32) + b_ref[...]


def _ada_call(cvecs, w_ada, b_ada):
    return pl.pallas_call(
        _ada_kernel,
        grid=(DEPTH, N_MOD),
        in_specs=[
            pl.BlockSpec((MOD_ROWS, D_MODEL), lambda l, n: (0, 0)),
            pl.BlockSpec((None, D_MODEL, D_MODEL), lambda l, n: (l, 0, n)),
            pl.BlockSpec((None, 1, D_MODEL), lambda l, n: (l, 0, n)),
        ],
        out_specs=pl.BlockSpec((None, MOD_ROWS, D_MODEL), lambda l, n: (l, 0, n)),
        out_shape=jax.ShapeDtypeStruct((DEPTH, MOD_ROWS, N_MOD * D_MODEL), F32),
        compiler_params=_params("parallel", "parallel"),
        name="ada",
    )(cvecs, w_ada, b_ada.reshape(DEPTH, 1, N_MOD * D_MODEL))


W_IN_Q = 0
W_IN_KV = W_IN_Q + Q_LORA
W_IN_A = W_IN_KV + KV_LORA
W_IN_G = W_IN_A + C_CONV
W_IN_KR = W_IN_G + C_CONV
W_IN_COLS = W_IN_KR + HEAD_BLOCK


def _swap_halves(x):
    lane = lax.broadcasted_iota(jnp.int32, x.shape, 1)
    first = lane % (2 * ROPE_PAIRS) < ROPE_PAIRS
    from_right = pltpu.roll(x, HEAD_BLOCK - ROPE_PAIRS, 1)
    from_left = pltpu.roll(x, ROPE_PAIRS, 1)
    return jnp.where(first, from_right, from_left)


def _premix_kernel(rope, resid, x_ref, *rest):
    if resid:
        moe_ref, modprev_ref, *rest = rest
    mod_ref, n1g_ref, win_ref, qg_ref, wuq_ref, kvg_ref, wuk_ref, wuv_ref, *rest = rest
    if rope:
        cos_ref, sin_ref, *rest = rest
    q_ref, k_ref, v_ref, z_ref, *rest = rest
    if resid:
        x2_ref, *rest = rest
    if not rope:
        ckv_ref, kr_ref = rest
    x = x_ref[0]
    if resid:
        x = x + modprev_ref[0][5:6] * _unpack_bf16_pairs(moe_ref[0], F32)
        x2_ref[0] = x
    mod = mod_ref[0]
    sh1, sc1 = mod[0:1], mod[1:2]
    h = (_rms(x, n1g_ref[...]) * (1.0 + sc1) + sh1).astype(BF16)
    proj = jnp.dot(h, win_ref[...], preferred_element_type=F32)
    q_c = proj[:, W_IN_Q:W_IN_KV]
    kv_c = proj[:, W_IN_KV:W_IN_A]
    a = proj[:, W_IN_A:W_IN_G]
    gate = proj[:, W_IN_G:W_IN_KR]
    kr = proj[:, W_IN_KR:W_IN_COLS]

    z_ref[0] = a * jax.nn.sigmoid(gate)

    qn = _rms(q_c, qg_ref[...]).astype(BF16)
    q = jnp.dot(qn, wuq_ref[...], preferred_element_type=F32)
    ckv = _rms(kv_c, kvg_ref[...])
    ckv_b = ckv.astype(BF16)
    kn = jnp.dot(ckv_b, wuk_ref[...], preferred_element_type=F32)
    v = jnp.dot(ckv_b, wuv_ref[...], preferred_element_type=F32)

    if rope:
        v = v + _ones_lane(v.shape[1])
        cos = cos_ref[...]
        sin = sin_ref[...]
        kr = kr * cos + _swap_halves(kr) * sin
    else:
        ckv_ref[0] = ckv
        kr_ref[0] = kr[:, :QK_ROPE]
    v_ref[0] = v.astype(BF16)

    for hd in range(N_HEADS):
        sl = slice(hd * HEAD_BLOCK, (hd + 1) * HEAD_BLOCK)
        qh = q[:, sl]
        if rope:
            qh = qh * cos + _swap_halves(qh) * sin
        q_ref[0, :, sl] = (qh * (ATTN_SCALE * LOG2E)).astype(BF16)
        k_ref[0, :, sl] = (kn[:, sl] + kr).astype(BF16)


def _premix_call(x, mod, lw, layer, tm, rope_tabs=None, moe=None, mod_prev=None):
    b, t, _ = x.shape
    rope = rope_tabs is not None
    resid = moe is not None
    nt = t // tm
    per_b = (lambda i, j: (i, 0, 0)) if mod.shape[0] > 1 else (lambda i, j: (0, 0, 0))
    const2 = lambda i, j: (0, 0)
    tok = lambda i, j: (i, j, 0)
    v_cols = N_HEADS * (HEAD_BLOCK if rope else V_HEAD)
    in_specs = [pl.BlockSpec((1, tm, D_MODEL), tok)]
    args = [x]
    if resid:
        in_specs += [pl.BlockSpec((1, tm, D_MODEL // 2), tok),
                     pl.BlockSpec((1, N_MOD, D_MODEL), per_b)]
        args += [moe, mod_prev]
    weights = [lw["n1g"], lw["win"], lw["qg"], lw["wuq"], lw["kvg"], lw["wuk"],
               lw["wuv_ones"] if rope else lw["wuv"]]
    in_specs += [pl.BlockSpec((1, N_MOD, D_MODEL), per_b)]
    in_specs += [_layer_spec(w, layer) for w in weights]
    args += [mod] + weights
    out_specs = [
        pl.BlockSpec((1, tm, N_HEADS * HEAD_BLOCK), tok),
        pl.BlockSpec((1, tm, N_HEADS * HEAD_BLOCK), tok),
        pl.BlockSpec((1, tm, v_cols), tok),
        pl.BlockSpec((1, tm, C_CONV), tok),
    ]
    out_shape = [
        jax.ShapeDtypeStruct((b, t, N_HEADS * HEAD_BLOCK), BF16),
        jax.ShapeDtypeStruct((b, t, N_HEADS * HEAD_BLOCK), BF16),
        jax.ShapeDtypeStruct((b, t, v_cols), BF16),
        jax.ShapeDtypeStruct((b, t, C_CONV), F32),
    ]
    if resid:
        out_specs.append(pl.BlockSpec((1, tm, D_MODEL), tok))
        out_shape.append(jax.ShapeDtypeStruct((b, t, D_MODEL), F32))
    if rope:
        in_specs += [
            pl.BlockSpec((tm, HEAD_BLOCK), lambda i, j: (j, 0)),
            pl.BlockSpec((tm, HEAD_BLOCK), lambda i, j: (j, 0)),
        ]
        args += [rope_tabs[0], rope_tabs[1]]
    else:
        out_specs += [
            pl.BlockSpec((1, tm, KV_LORA), tok),
            pl.BlockSpec((1, tm, QK_ROPE), tok),
        ]
        out_shape += [
            jax.ShapeDtypeStruct((b, t, KV_LORA), F32),
            jax.ShapeDtypeStruct((b, t, QK_ROPE), F32),
        ]
    return pl.pallas_call(
        functools.partial(_premix_kernel, rope, resid),
        grid=(b, nt),
        in_specs=in_specs,
        out_specs=out_specs,
        out_shape=out_shape,
        compiler_params=_params("parallel", "parallel"),
        name="premix_rope" if rope else "premix",
    )(*args)


def _ctx_kernel(ckv_ref, kr_ref, wuk_ref, wuv_ref, k_ref, v_ref):
    ckv_b = ckv_ref[...].astype(BF16)
    kn = jnp.dot(ckv_b, wuk_ref[...], preferred_element_type=F32)
    v = jnp.dot(ckv_b, wuv_ref[...], preferred_element_type=F32)
    v_ref[0] = (v + _ones_lane(v.shape[1])).astype(BF16)
    kr = kr_ref[...]
    for hd in range(N_HEADS):
        sl = slice(hd * HEAD_BLOCK, (hd + 1) * HEAD_BLOCK)
        k_ref[0, :, sl] = (kn[:, sl] + kr).astype(BF16)


def _ctx_call(cache_ckv, cache_kr_pad, layer, lw):
    b, _, s, _ = cache_ckv.shape
    return pl.pallas_call(
        _ctx_kernel,
        grid=(b,),
        in_specs=[
            pl.BlockSpec((None, None, s, KV_LORA), lambda i: (i, layer, 0, 0)),
            pl.BlockSpec((None, None, s, HEAD_BLOCK), lambda i: (i, layer, 0, 0)),
            _layer_spec(lw["wuk"], layer),
            _layer_spec(lw["wuv_ones"], layer),
        ],
        out_specs=[
            pl.BlockSpec((1, s, N_HEADS * HEAD_BLOCK), lambda i: (i, 0, 0)),
            pl.BlockSpec((1, s, N_HEADS * HEAD_BLOCK), lambda i: (i, 0, 0)),
        ],
        out_shape=[
            jax.ShapeDtypeStruct((b, s, N_HEADS * HEAD_BLOCK), BF16),
            jax.ShapeDtypeStruct((b, s, N_HEADS * HEAD_BLOCK), BF16),
        ],
        compiler_params=_params("parallel"),
        name="ctx_kv",
    )(cache_ckv, cache_kr_pad, lw["wuk"], lw["wuv_ones"])


_NT = (((1,), (1,)), ((), ()))


def _attn_kernel(ctx, q_ref, k_ref, v_ref, *rest):
    if ctx:
        kc_ref, vc_ref, o_ref = rest
    else:
        (o_ref,) = rest
    outs = []
    for hd in range(N_HEADS):
        sl = slice(hd * HEAD_BLOCK, (hd + 1) * HEAD_BLOCK)
        qh = q_ref[0, :, sl]
        s = lax.dot_general(qh, k_ref[0, :, sl], _NT, preferred_element_type=F32)
        m = jnp.max(s, axis=-1, keepdims=True)
        if ctx:
            sc = lax.dot_general(qh, kc_ref[0, :, sl], _NT, preferred_element_type=F32)
            m = jnp.maximum(m, jnp.max(sc, axis=-1, keepdims=True))
        p = jnp.exp2(s - m)
        if ctx:
            pc = jnp.exp2(sc - m)
            o = jnp.dot(p.astype(BF16), v_ref[0, :, sl], preferred_element_type=F32)
            o = o + jnp.dot(pc.astype(BF16), vc_ref[0, :, sl], preferred_element_type=F32)
            outs.append(o[:, :V_HEAD] / o[:, V_HEAD:V_HEAD + 1])
        else:
            vs = slice(hd * V_HEAD, (hd + 1) * V_HEAD)
            o = jnp.dot(p.astype(BF16), v_ref[0, :, vs], preferred_element_type=F32)
            outs.append(o / jnp.sum(p, axis=-1, keepdims=True))
    o_ref[0] = jnp.concatenate(outs, axis=-1).astype(BF16)


def _attn_call(q, k, v, tq, ctx_kv=None):
    b, t, _ = q.shape
    ctx = ctx_kv is not None
    tok = lambda i, j: (i, j, 0)
    seq = lambda i, j: (i, 0, 0)
    in_specs = [
        pl.BlockSpec((1, tq, N_HEADS * HEAD_BLOCK), tok),
        pl.BlockSpec((1, t, N_HEADS * HEAD_BLOCK), seq),
        pl.BlockSpec((1, t, v.shape[2]), seq),
    ]
    args = [q, k, v]
    if ctx:
        s = ctx_kv[0].shape[1]
        in_specs += [
            pl.BlockSpec((1, s, N_HEADS * HEAD_BLOCK), seq),
            pl.BlockSpec((1, s, N_HEADS * HEAD_BLOCK), seq),
        ]
        args += list(ctx_kv)
    return pl.pallas_call(
        functools.partial(_attn_kernel, ctx),
        grid=(b, t // tq),
        in_specs=in_specs,
        out_specs=pl.BlockSpec((1, tq, N_HEADS * V_HEAD), tok),
        out_shape=jax.ShapeDtypeStruct((b, t, N_HEADS * V_HEAD), BF16),
        compiler_params=_params("parallel", "arbitrary"),
        name="attn_ctx" if ctx else "attn",
    )(*args)


CONV_ROWS = 64


def _route(aff, bias):
    sel = aff + bias
    rows = [sel[e:e + 1] for e in range(N_EXPERTS)]
    affr = [aff[e:e + 1] for e in range(N_EXPERTS)]
    scores = []
    for g in range(N_GROUPS):
        a, b, c, d = rows[g * EXPERTS_PER_GROUP:(g + 1) * EXPERTS_PER_GROUP]
        hi1, lo1 = jnp.maximum(a, b), jnp.minimum(a, b)
        hi2, lo2 = jnp.maximum(c, d), jnp.minimum(c, d)
        top1 = jnp.maximum(hi1, hi2)
        top2 = jnp.maximum(jnp.minimum(hi1, hi2), jnp.maximum(lo1, lo2))
        scores.append(top1 + top2)
    best = scores[0]
    gidx = jnp.zeros_like(best, dtype=jnp.int32)
    for g in range(1, N_GROUPS):
        better = scores[g] > best
        best = jnp.where(better, scores[g], best)
        gidx = jnp.where(better, g, gidx)
    gates = []
    e_lo = jnp.full_like(best, float(N_EXPERTS))
    e_hi = jnp.full_like(best, -1.0)
    for g in range(N_GROUPS):
        in_g = gidx == g
        s = rows[g * EXPERTS_PER_GROUP:(g + 1) * EXPERTS_PER_GROUP]
        af = affr[g * EXPERTS_PER_GROUP:(g + 1) * EXPERTS_PER_GROUP]
        picked = []
        for i in range(EXPERTS_PER_GROUP):
            rank = jnp.zeros_like(gidx)
            for j in range(EXPERTS_PER_GROUP):
                if j == i:
                    continue
                ahead = (s[j] >= s[i]) if j < i else (s[j] > s[i])
                rank = rank + ahead.astype(jnp.int32)
            chosen = (rank < 2) & in_g
            eid = float(g * EXPERTS_PER_GROUP + i)
            e_lo = jnp.where(chosen, jnp.minimum(e_lo, eid), e_lo)
            e_hi = jnp.where(chosen, jnp.maximum(e_hi, eid), e_hi)
            picked.append(jnp.where(chosen, af[i], 0.0))
        den = (picked[0] + picked[1]) + (picked[2] + picked[3])
        den = jnp.where(in_g, den, 1.0)
        gates += [pk / den for pk in picked]
    return jnp.concatenate(gates, axis=0), e_lo, e_hi


def _postmix_kernel(nt, x_ref, attn_ref, z_ref, mod_ref, cw_ref, cb_ref, lg_ref, lb_ref, wout_ref,
                    n2g_ref, rwt_ref, rb_ref, x1_ref, h2_ref, meta_ref, ids_ref, zp_ref, zs_ref, cz_ref):
    nseq, tm = x_ref.shape[0], x_ref.shape[1]
    rows = nseq * tm
    j = pl.program_id(1)
    t0 = pl.multiple_of(j * tm, tm)
    zeros = jnp.zeros((HALO, C_CONV), F32)
    cw = cw_ref[...]
    cb = cb_ref[...]
    span = zs_ref.shape[1]

    for sq in range(nseq):
        zp_ref[HALO:HALO + tm, :] = z_ref[sq, pl.ds(t0, tm), :]

        @pl.when(j == 0)
        def _():
            zp_ref[0:HALO, :] = zeros

        @pl.when(j > 0)
        def _(sq=sq):
            zp_ref[0:HALO, :] = z_ref[sq, pl.ds(t0 - HALO, HALO), :]

        @pl.when(j == nt - 1)
        def _():
            zp_ref[HALO + tm:, :] = zeros

        @pl.when(j < nt - 1)
        def _(sq=sq):
            zp_ref[HALO + tm:, :] = z_ref[sq, pl.ds(t0 + tm, HALO), :]

        for o in range(1, SUBLANES):
            zs_ref[o - 1] = zp_ref[o:o + span, :]

        for r0 in range(0, tm, CONV_ROWS):
            acc = jnp.broadcast_to(cb, (CONV_ROWS, C_CONV))
            for kk in range(CONV_K):
                off = HALO - CONV_PAD + kk
                o = off % SUBLANES
                base = r0 + off - o
                if o == 0:
                    tap = zp_ref[base:base + CONV_ROWS, :]
                else:
                    tap = zs_ref[o - 1, base:base + CONV_ROWS, :]
                acc = acc + tap * cw[kk:kk + 1, :]
            cz_ref[sq * tm + r0:sq * tm + r0 + CONV_ROWS, :] = acc

    cz = cz_ref[...]
    mu = jnp.mean(cz, axis=-1, keepdims=True)
    cen = cz - mu
    var = jnp.mean(cen * cen, axis=-1, keepdims=True)
    conv = _silu(cen * lax.rsqrt(var + EPS) * lg_ref[...] + lb_ref[...]).astype(BF16)

    n_attn = N_HEADS * V_HEAD
    y = jnp.dot(attn_ref[...].reshape(rows, n_attn), wout_ref[0:n_attn, :],
                preferred_element_type=F32)
    y = y + jnp.dot(conv, wout_ref[n_attn:, :], preferred_element_type=F32)

    mod = mod_ref[0]
    g1, sh2, sc2 = mod[2:3], mod[3:4], mod[4:5]
    x1 = x_ref[...].reshape(rows, D_MODEL) + g1 * y
    x1_ref[...] = x1.reshape(x1_ref.shape)
    h2 = _rms(x1, n2g_ref[...]) * (1.0 + sc2) + sh2
    h2_ref[...] = _pack_bf16_pairs(h2).reshape(h2_ref.shape)

    logits = lax.dot_general(rwt_ref[...], h2, _NT, precision=lax.Precision.HIGHEST,
                             preferred_element_type=F32)
    gates, e_lo, e_hi = _route(jax.nn.sigmoid(logits), rb_ref[...])
    row = lax.broadcasted_iota(jnp.int32, (SUBLANES, rows), 0)
    ids = jnp.where(row == 0, e_lo, jnp.where(row == 1, e_hi, 0.0))
    ids_ref[...] = ids
    rec = jnp.concatenate([gates, ids, jnp.zeros((LANES - N_EXPERTS - SUBLANES, rows), F32)], axis=0)
    meta_ref[...] = rec.T.reshape(meta_ref.shape)


def _postmix_call(x, attn, z, mod, lw, layer, rwt, rb, tm, nseq=1):
    b, t, _ = x.shape
    nt = t // tm
    assert nseq == 1 or (nt == 1 and mod.shape[0] == 1)
    per_b = (lambda i, j: (i, 0, 0)) if mod.shape[0] > 1 else (lambda i, j: (0, 0, 0))
    const2 = lambda i, j: (0, 0)
    tok = lambda i, j: (i, j, 0)
    weights = [lw["conv_w"], lw["conv_b"], lw["ln_g"], lw["ln_b"], lw["wout"], lw["n2g"]]
    return pl.pallas_call(
        functools.partial(_postmix_kernel, nt),
        grid=(b // nseq, nt),
        in_specs=[
            pl.BlockSpec((nseq, tm, D_MODEL), tok),
            pl.BlockSpec((nseq, tm, N_HEADS * V_HEAD), tok),
            pl.BlockSpec((nseq, t, C_CONV), lambda i, j: (i, 0, 0)),
            pl.BlockSpec((1, N_MOD, D_MODEL), per_b),
        ] + [_layer_spec(w, layer) for w in weights] + [
            pl.BlockSpec((N_EXPERTS, D_MODEL), const2),
            pl.BlockSpec((N_EXPERTS, 1), const2),
        ],
        out_specs=[
            pl.BlockSpec((nseq, tm, D_MODEL), tok),
            pl.BlockSpec((nseq, tm, D_MODEL // 2), tok),
            pl.BlockSpec((nseq, tm, LANES), tok),
            pl.BlockSpec((SUBLANES, nseq * tm), lambda i, j: (0, i * nt + j)),
        ],
        out_shape=[
            jax.ShapeDtypeStruct((b, t, D_MODEL), F32),
            jax.ShapeDtypeStruct((b, t, D_MODEL // 2), jnp.int32),
            jax.ShapeDtypeStruct((b, t, LANES), F32),
            jax.ShapeDtypeStruct((SUBLANES, b * t), F32),
        ],
        scratch_shapes=[
            pltpu.VMEM((tm + 2 * HALO, C_CONV), F32),
            pltpu.VMEM((SUBLANES - 1, tm + 2 * HALO - SUBLANES, C_CONV), F32),
            pltpu.VMEM((nseq * tm, C_CONV), F32),
        ],
        compiler_params=_params("parallel", "arbitrary"),
        name="postmix",
    )(x, attn, z, mod, *weights, rwt, rb)


N_PAIRS = EXPERTS_PER_GROUP * (EXPERTS_PER_GROUP - 1) // 2
N_CLASSES = N_GROUPS * N_PAIRS
_PAIR_LO = np.array([i for i in range(EXPERTS_PER_GROUP) for j in range(i + 1, EXPERTS_PER_GROUP)])
_PAIR_HI = np.array([j for i in range(EXPERTS_PER_GROUP) for j in range(i + 1, EXPERTS_PER_GROUP)])
MOE_ROWS = 256
MOE_STEP_TILES = 4
SC_CORES = 2
SC_SUBCORES = 16
SC_WORKERS = SC_CORES * SC_SUBCORES
SC_WINDOW = 32


def _slot_rows(n):
    return n + N_CLASSES * MOE_ROWS


def _plan(ids, n_slots):
    e_lo = ids[0].astype(jnp.int32)
    e_hi = ids[1].astype(jnp.int32)
    grp = e_lo // EXPERTS_PER_GROUP
    i = e_lo % EXPERTS_PER_GROUP
    j = e_hi % EXPERTS_PER_GROUP
    cls = grp * N_PAIRS + (i * (2 * EXPERTS_PER_GROUP - 1 - i)) // 2 + (j - i - 1)
    onehot = (cls[None, :] == jnp.arange(N_CLASSES, dtype=jnp.int32)[:, None]).astype(jnp.int32)
    csum = jnp.cumsum(onehot, axis=1)
    counts = csum[:, -1]
    padded = (counts + MOE_ROWS - 1) // MOE_ROWS * MOE_ROWS
    pend = jnp.cumsum(padded)
    pstart = pend - padded
    pos = jnp.sum(onehot * (csum - 1 + pstart[:, None]), axis=0).astype(jnp.int32)
    tile_start = jnp.arange(n_slots // MOE_ROWS, dtype=jnp.int32) * MOE_ROWS
    tcls = jnp.minimum(jnp.sum(tile_start[:, None] >= pend[None, :], axis=1), N_CLASSES - 1)
    tgrp, tpair = tcls // N_PAIRS, tcls % N_PAIRS
    t_lo = tgrp * EXPERTS_PER_GROUP + jnp.asarray(_PAIR_LO, jnp.int32)[tpair]
    t_hi = tgrp * EXPERTS_PER_GROUP + jnp.asarray(_PAIR_HI, jnp.int32)[tpair]
    tile_experts = jnp.stack([t_lo, t_hi]).astype(jnp.int32)
    n_used = (pend[-1] // MOE_ROWS).astype(jnp.int32).reshape(1)
    return pos, tile_experts, n_used


def _sc_mesh():
    return plsc.VectorSubcoreMesh(core_axis_name="core", subcore_axis_name="subcore")


def _sc_worker():
    return lax.axis_index("core") * SC_SUBCORES + lax.axis_index("subcore")


def _dispatch_call(h2, meta, pos, n_slots):
    n = h2.shape[0]
    per_worker = n // SC_WORKERS

    @pl.kernel(
        out_type=[jax.ShapeDtypeStruct((n_slots, h2.shape[1]), h2.dtype),
                  jax.ShapeDtypeStruct((n_slots, LANES), F32)],
        mesh=_sc_mesh(),
        scratch_types=[pltpu.VMEM((per_worker,), jnp.int32),
                       pltpu.VMEM((SC_WINDOW, h2.shape[1]), h2.dtype),
                       pltpu.VMEM((SC_WINDOW, LANES), F32)],
    )
    def dispatch(h2_hbm, meta_hbm, pos_hbm, xs_hbm, ms_hbm, slot_v, row_v, rec_v):
        wid = _sc_worker()
        base = wid * per_worker
        pltpu.sync_copy(pos_hbm.at[wid], slot_v)

        @pl.loop(0, per_worker // SC_WINDOW)
        def _(step):
            rows = pl.ds(base + step * SC_WINDOW, SC_WINDOW)
            slots = slot_v.at[pl.ds(step * SC_WINDOW, SC_WINDOW)]
            pltpu.sync_copy(h2_hbm.at[rows], row_v)
            pltpu.sync_copy(row_v, xs_hbm.at[slots])
            pltpu.sync_copy(meta_hbm.at[rows], rec_v)
            pltpu.sync_copy(rec_v, ms_hbm.at[slots])

    return dispatch(h2, meta, pos.reshape(SC_WORKERS, per_worker))


def _combine_call(ys, pos):
    n = pos.shape[0]
    per_worker = n // SC_WORKERS

    @pl.kernel(
        out_type=jax.ShapeDtypeStruct((n, ys.shape[1]), ys.dtype),
        mesh=_sc_mesh(),
        scratch_types=[pltpu.VMEM((per_worker,), jnp.int32),
                       pltpu.VMEM((SC_WINDOW, ys.shape[1]), ys.dtype)],
    )
    def combine(ys_hbm, pos_hbm, out_hbm, slot_v, row_v):
        wid = _sc_worker()
        base = wid * per_worker
        pltpu.sync_copy(pos_hbm.at[wid], slot_v)

        @pl.loop(0, per_worker // SC_WINDOW)
        def _(step):
            slots = slot_v.at[pl.ds(step * SC_WINDOW, SC_WINDOW)]
            pltpu.sync_copy(ys_hbm.at[slots], row_v)
            pltpu.sync_copy(row_v, out_hbm.at[pl.ds(base + step * SC_WINDOW, SC_WINDOW)])

    return combine(ys, pos.reshape(SC_WORKERS, per_worker))


def _last_used_step(nu_ref):
    return jnp.maximum((nu_ref[0] - 1) // MOE_STEP_TILES, 0)


def _experts_kernel(te_ref, nu_ref, xs_ref, ms_ref, wg_ref, wu_ref, wd_ref, ys_ref):
    step = pl.program_id(0)
    live = step <= _last_used_step(nu_ref)
    for sub in range(MOE_STEP_TILES):
        tile = step * MOE_STEP_TILES + sub
        rows = slice(sub * MOE_ROWS, (sub + 1) * MOE_ROWS)

        @pl.when(tile < nu_ref[0])
        def _(tile=tile, rows=rows):
            x = _unpack_bf16_pairs(xs_ref[rows, :])
            rec = ms_ref[rows, :]
            lane = lax.broadcasted_iota(jnp.int32, rec.shape, 1)

            def expert(e):
                gate = jnp.sum(jnp.where(lane == e, rec, 0.0), axis=-1, keepdims=True)
                hg = jnp.dot(x, wg_ref[e], preferred_element_type=F32)
                hu = jnp.dot(x, wu_ref[e], preferred_element_type=F32)
                hid = (_silu(hg) * hu * gate).astype(BF16)
                return jnp.dot(hid, wd_ref[e], preferred_element_type=F32)

            y = expert(te_ref[0, tile]) + expert(te_ref[1, tile])
            ys_ref[rows, :] = _pack_bf16_pairs(y)

        @pl.when((tile >= nu_ref[0]) & live)
        def _(rows=rows):
            ys_ref[rows, :] = jnp.zeros((MOE_ROWS, D_MODEL // 2), jnp.int32)


def _experts_call(xs, ms, tile_experts, n_used, layer, moe_w):
    n_slots = xs.shape[0]
    step_rows = MOE_ROWS * MOE_STEP_TILES
    rows = lambda i, te, nu: (jnp.minimum(i, _last_used_step(nu)), 0)

    def resident(w):
        return pl.BlockSpec((None,) + w.shape[1:], lambda i, te, nu: (layer, 0, 0, 0),
                            pipeline_mode=pl.Buffered(1))

    wg, wu, wd = moe_w
    return pl.pallas_call(
        _experts_kernel,
        grid_spec=pltpu.PrefetchScalarGridSpec(
            num_scalar_prefetch=2,
            grid=(n_slots // step_rows,),
            in_specs=[
                pl.BlockSpec((step_rows, D_MODEL // 2), rows),
                pl.BlockSpec((step_rows, LANES), rows),
                resident(wg), resident(wu), resident(wd),
            ],
            out_specs=pl.BlockSpec((step_rows, D_MODEL // 2), rows),
        ),
        out_shape=jax.ShapeDtypeStruct((n_slots, D_MODEL // 2), jnp.int32),
        compiler_params=_params("arbitrary"),
        name="experts",
    )(tile_experts, n_used, xs, ms, wg, wu, wd)


def _routed_moe(h2, meta, ids, layer, moe_w):
    b, t, _ = h2.shape
    n = b * t
    n_slots = _slot_rows(n)
    pos, tile_experts, n_used = _plan(ids, n_slots)
    xs, ms = _dispatch_call(h2.reshape(n, h2.shape[2]), meta.reshape(n, LANES), pos, n_slots)
    ys = _experts_call(xs, ms, tile_experts, n_used, layer, moe_w)
    return _combine_call(ys, pos).reshape(b, t, ys.shape[1])


def _final_kernel(x1_ref, moe_ref, mod_ref, fg_ref, o_ref):
    x2 = x1_ref[0] + mod_ref[0][5:6] * _unpack_bf16_pairs(moe_ref[0], F32)
    o_ref[0] = _rms(x2, fg_ref[...])


def _final_call(x1, moe, mod, final_g, tm):
    b, t, _ = x1.shape
    per_b = (lambda i, j: (i, 0, 0)) if mod.shape[0] > 1 else (lambda i, j: (0, 0, 0))
    tok = lambda i, j: (i, j, 0)
    return pl.pallas_call(
        _final_kernel,
        grid=(b, t // tm),
        in_specs=[
            pl.BlockSpec((1, tm, D_MODEL), tok),
            pl.BlockSpec((1, tm, D_MODEL // 2), tok),
            pl.BlockSpec((1, N_MOD, D_MODEL), per_b),
            pl.BlockSpec((1, D_MODEL), lambda i, j: (0, 0)),
        ],
        out_specs=pl.BlockSpec((1, tm, D_MODEL), tok),
        out_shape=jax.ShapeDtypeStruct((b, t, D_MODEL), F32),
        compiler_params=_params("parallel", "parallel"),
        name="final_norm",
    )(x1, moe, mod, final_g)


def _prep_weights(norm1_g, w_in, q_norm_g, w_uq, kv_norm_g, w_ukv, conv_w, conv_b, conv_ln_g,
                  conv_ln_b, w_out, norm2_g):
    o_kv, o_kr, o_a, o_g = Q_LORA, Q_LORA + KV_LORA, Q_LORA + KV_LORA + QK_ROPE, Q_LORA + KV_LORA + QK_ROPE + C_CONV
    pad_r = ((0, 0), (0, 0), (0, HEAD_BLOCK - QK_ROPE))
    win = jnp.concatenate(
        [w_in[..., :o_kv], w_in[..., o_kv:o_kr], w_in[..., o_a:o_g], w_in[..., o_g:],
         jnp.pad(w_in[..., o_kr:o_a], pad_r)], axis=-1).astype(BF16)

    wuq = w_uq.reshape(DEPTH, Q_LORA, N_HEADS, QK_NOPE + QK_ROPE)
    nope, rope = wuq[..., :QK_NOPE], wuq[..., QK_NOPE:]
    zpad = jnp.zeros((DEPTH, Q_LORA, N_HEADS, HEAD_BLOCK - QK_NOPE - QK_ROPE), F32)
    wuq_p = jnp.concatenate([rope, nope, zpad], axis=-1).reshape(DEPTH, Q_LORA, -1).astype(BF16)

    wukv = w_ukv.reshape(DEPTH, KV_LORA, N_HEADS, QK_NOPE + V_HEAD)
    kz0 = jnp.zeros((DEPTH, KV_LORA, N_HEADS, QK_ROPE), F32)
    kz1 = jnp.zeros((DEPTH, KV_LORA, N_HEADS, HEAD_BLOCK - QK_NOPE - QK_ROPE), F32)
    wuk = jnp.concatenate([kz0, wukv[..., :QK_NOPE], kz1], axis=-1)
    wuk = wuk.reshape(DEPTH, KV_LORA, -1).astype(BF16)
    wuv = wukv[..., QK_NOPE:].reshape(DEPTH, KV_LORA, -1).astype(BF16)
    vz = jnp.zeros((DEPTH, KV_LORA, N_HEADS, HEAD_BLOCK - V_HEAD), F32)
    wuv_ones = jnp.concatenate([wukv[..., QK_NOPE:], vz], axis=-1)
    wuv_ones = wuv_ones.reshape(DEPTH, KV_LORA, -1).astype(BF16)

    row = lambda a: a[:, None, :]
    return {
        "n1g": row(norm1_g), "win": win, "qg": row(q_norm_g), "wuq": wuq_p,
        "kvg": row(kv_norm_g), "wuk": wuk, "wuv": wuv, "wuv_ones": wuv_ones,
        "conv_w": conv_w, "conv_b": row(conv_b), "ln_g": row(conv_ln_g),
        "ln_b": row(conv_ln_b), "wout": w_out.astype(BF16), "n2g": row(norm2_g),
    }


def _layer_spec(w, layer):
    return pl.BlockSpec((None,) + w.shape[1:], lambda *_: (layer,) + (0,) * (w.ndim - 1))


def _rope_tables(n_tokens):
    rows = n_tokens // GRID_W
    row = jnp.repeat(jnp.arange(rows), GRID_W).astype(F32)
    col = jnp.tile(jnp.arange(GRID_W), rows).astype(F32)
    freqs = ROPE_BASE ** (-jnp.arange(ROPE_PAIRS, dtype=F32) / ROPE_PAIRS)
    ar, ac = row[:, None] * freqs, col[:, None] * freqs
    cr, sr, cc, sc = jnp.cos(ar), jnp.sin(ar), jnp.cos(ac), jnp.sin(ac)
    rest = HEAD_BLOCK - QK_ROPE
    cos = jnp.concatenate([cr, cr, cc, cc, jnp.ones((n_tokens, rest), F32)], axis=1)
    sin = jnp.concatenate([-sr, sr, -sc, sc, jnp.zeros((n_tokens, rest), F32)], axis=1)
    return cos, sin


TM_PROMPT = 256
PROMPT_SEQS_PER_STEP = 2
TM_SAMPLE = 512
TQ_SAMPLE = 512


def _fold(a):
    return None if a is None else a.reshape(a.shape[0] // 2, a.shape[1] * 2, a.shape[2])


def _unfold(a):
    return a.reshape(a.shape[0] * 2, a.shape[1] // 2, a.shape[2])


def kernel(x_prompt, x_sample, cache_ckv, cache_krope, c, c_ctx, w_ada, b_ada, norm1_g, w_in,
           q_norm_g, w_uq, kv_norm_g, w_ukv, conv_w, conv_b, conv_ln_g, conv_ln_b, w_out,
           norm2_g, router_w, router_b, w_gate, w_up, w_down, final_g):
    dec_b = x_sample.shape[0]
    cvecs = jnp.concatenate(
        [c_ctx[None], c, jnp.zeros((MOD_ROWS - 1 - dec_b, D_MODEL), F32)], axis=0)
    mod_all = _ada_call(cvecs, w_ada, b_ada).reshape(DEPTH, MOD_ROWS, N_MOD, D_MODEL)

    moe_w = (w_gate.astype(BF16), w_up.astype(BF16), w_down.astype(BF16))
    rwt = router_w.T
    rb = router_b[:, None]
    fg = final_g[None]
    rope_tabs = _rope_tables(x_sample.shape[1])
    cache_kr_pad = jnp.pad(cache_krope, ((0, 0), (0, 0), (0, 0), (0, HEAD_BLOCK - QK_ROPE)))

    xp, xs = x_prompt, x_sample
    moe_p = moe_s = mod_p = mod_s = None
    ckvs, krs = [], []
    lw = _prep_weights(norm1_g, w_in, q_norm_g, w_uq, kv_norm_g, w_ukv, conv_w, conv_b,
                       conv_ln_g, conv_ln_b, w_out, norm2_g)
    for l in range(DEPTH):
        prev_p, prev_s = mod_p, mod_s
        mod_p = mod_all[l, 0:1]
        mod_s = mod_all[l, 1:1 + dec_b]

        outs = _premix_call(_fold(xp), mod_p, lw, l, TM_SAMPLE, moe=_fold(moe_p),
                            mod_prev=prev_p)
        outs = [_unfold(o) for o in outs]
        if l > 0:
            q, k, v, z, xp, ckv, kr = outs
        else:
            q, k, v, z, ckv, kr = outs
        ckvs.append(ckv)
        krs.append(kr)
        attn = _attn_call(q, k, v, xp.shape[1])
        xp, h2, meta, ids = _postmix_call(xp, attn, z, mod_p, lw, l, rwt, rb, TM_PROMPT,
                                          nseq=PROMPT_SEQS_PER_STEP)
        moe_p = _routed_moe(h2, meta, ids, l, moe_w)

        outs = _premix_call(xs, mod_s, lw, l, TM_SAMPLE, rope_tabs, moe=moe_s, mod_prev=prev_s)
        if l > 0:
            q, k, v, z, xs = outs
        else:
            q, k, v, z = outs
        ctx_kv = _ctx_call(cache_ckv, cache_kr_pad, l, lw)
        attn = _attn_call(q, k, v, TQ_SAMPLE, ctx_kv)
        xs, h2, meta, ids = _postmix_call(xs, attn, z, mod_s, lw, l, rwt, rb, TM_SAMPLE)
        moe_s = _routed_moe(h2, meta, ids, l, moe_w)

    y_prompt = _unfold(_final_call(_fold(xp), _fold(moe_p), mod_p, fg, TM_SAMPLE))
    y_sample = _final_call(xs, moe_s, mod_s, fg, TM_SAMPLE)
    return y_prompt, y_sample, jnp.stack(ckvs, axis=1), jnp.stack(krs, axis=1)
```

```python
import functools

import jax
import jax.numpy as jnp
import numpy as np
from jax import lax
from jax.experimental import pallas as pl
from jax.experimental.pallas import tpu as pltpu
from jax.experimental.pallas import tpu_sc as plsc

D_MODEL = 1024
DEPTH = 4
GRID_W = 64
N_HEADS = 8
QK_NOPE = 64
QK_ROPE = 32
V_HEAD = 64
Q_LORA = 384
KV_LORA = 256
C_CONV = 512
CONV_K = 31
CONV_PAD = CONV_K // 2
N_EXPERTS = 16
N_GROUPS = 4
EXPERTS_PER_GROUP = N_EXPERTS // N_GROUPS
D_EXPERT = 256
ROPE_BASE = 10000.0
ROPE_PAIRS = QK_ROPE // 4
EPS = 1e-6
ATTN_SCALE = (QK_NOPE + QK_ROPE) ** -0.5
LOG2E = 1.4426950408889634

LANES = 128
SUBLANES = 8
HEAD_BLOCK = LANES
N_MOD = 6
MOD_ROWS = 16
HALO = 16
VMEM_LIMIT = 48 * 1024 * 1024

BF16 = jnp.bfloat16
F32 = jnp.float32


def _rms(x, g):
    return x * lax.rsqrt(jnp.mean(x * x, axis=-1, keepdims=True) + EPS) * g


def _silu(x):
    return x * jax.nn.sigmoid(x)


def _ones_lane(width):
    lane = lax.broadcasted_iota(jnp.int32, (1, width), 1)
    return jnp.where(lane % HEAD_BLOCK == V_HEAD, 1.0, 0.0)


def _pack_bf16_pairs(x):
    half = x.shape[1] // 2
    xb = x.astype(BF16).astype(F32)
    hi = pltpu.bitcast(xb[:, :half], jnp.uint32)
    lo = pltpu.bitcast(xb[:, half:], jnp.uint32)
    return pltpu.bitcast(hi | (lo >> 16), jnp.int32)


def _unpack_bf16_pairs(w, dtype=BF16):
    u = pltpu.bitcast(w, jnp.uint32)
    hi = pltpu.bitcast(u & jnp.uint32(0xFFFF0000), F32).astype(dtype)
    lo = pltpu.bitcast(u << 16, F32).astype(dtype)
    return jnp.concatenate([hi, lo], axis=1)


def _params(*sem):
    return pltpu.CompilerParams(dimension_semantics=sem, vmem_limit_bytes=VMEM_LIMIT)


def _ada_kernel(c_ref, w_ref, b_ref, o_ref):
    s = _silu(c_ref[...]).astype(BF16)
    o_ref[...] = jnp.dot(s, w_ref[...].astype(BF16), preferred_element_type=F32) + b_ref[...]


def _ada_call(cvecs, w_ada, b_ada):
    return pl.pallas_call(
        _ada_kernel,
        grid=(DEPTH, N_MOD),
        in_specs=[
            pl.BlockSpec((MOD_ROWS, D_MODEL), lambda l, n: (0, 0)),
            pl.BlockSpec((None, D_MODEL, D_MODEL), lambda l, n: (l, 0, n)),
            pl.BlockSpec((None, 1, D_MODEL), lambda l, n: (l, 0, n)),
        ],
        out_specs=pl.BlockSpec((None, MOD_ROWS, D_MODEL), lambda l, n: (l, 0, n)),
        out_shape=jax.ShapeDtypeStruct((DEPTH, MOD_ROWS, N_MOD * D_MODEL), F32),
        compiler_params=_params("parallel", "parallel"),
        name="ada",
    )(cvecs, w_ada, b_ada.reshape(DEPTH, 1, N_MOD * D_MODEL))


W_IN_Q = 0
W_IN_KV = W_IN_Q + Q_LORA
W_IN_A = W_IN_KV + KV_LORA
W_IN_G = W_IN_A + C_CONV
W_IN_KR = W_IN_G + C_CONV
W_IN_COLS = W_IN_KR + HEAD_BLOCK


def _swap_halves(x):
    lane = lax.broadcasted_iota(jnp.int32, x.shape, 1)
    first = lane % (2 * ROPE_PAIRS) < ROPE_PAIRS
    from_right = pltpu.roll(x, HEAD_BLOCK - ROPE_PAIRS, 1)
    from_left = pltpu.roll(x, ROPE_PAIRS, 1)
    return jnp.where(first, from_right, from_left)


def _premix_kernel(rope, resid, x_ref, *rest):
    if resid:
        moe_ref, modprev_ref, *rest = rest
    mod_ref, n1g_ref, win_ref, qg_ref, wuq_ref, kvg_ref, wuk_ref, wuv_ref, *rest = rest
    if rope:
        cos_ref, sin_ref, *rest = rest
    q_ref, k_ref, v_ref, z_ref, *rest = rest
    if resid:
        x2_ref, *rest = rest
    if not rope:
        ckv_ref, kr_ref = rest
    x = x_ref[0]
    if resid:
        x = x + modprev_ref[0][5:6] * _unpack_bf16_pairs(moe_ref[0], F32)
        x2_ref[0] = x
    mod = mod_ref[0]
    sh1, sc1 = mod[0:1], mod[1:2]
    h = (_rms(x, n1g_ref[...]) * (1.0 + sc1) + sh1).astype(BF16)
    proj = jnp.dot(h, win_ref[...], preferred_element_type=F32)
    q_c = proj[:, W_IN_Q:W_IN_KV]
    kv_c = proj[:, W_IN_KV:W_IN_A]
    a = proj[:, W_IN_A:W_IN_G]
    gate = proj[:, W_IN_G:W_IN_KR]
    kr = proj[:, W_IN_KR:W_IN_COLS]

    z_ref[0] = a * jax.nn.sigmoid(gate)

    qn = _rms(q_c, qg_ref[...]).astype(BF16)
    q = jnp.dot(qn, wuq_ref[...], preferred_element_type=F32)
    ckv = _rms(kv_c, kvg_ref[...])
    ckv_b = ckv.astype(BF16)
    kn = jnp.dot(ckv_b, wuk_ref[...], preferred_element_type=F32)
    v = jnp.dot(ckv_b, wuv_ref[...], preferred_element_type=F32)

    if rope:
        v = v + _ones_lane(v.shape[1])
        cos = cos_ref[...]
        sin = sin_ref[...]
        kr = kr * cos + _swap_halves(kr) * sin
    else:
        ckv_ref[0] = ckv
        kr_ref[0] = kr[:, :QK_ROPE]
    v_ref[0] = v.astype(BF16)

    for hd in range(N_HEADS):
        sl = slice(hd * HEAD_BLOCK, (hd + 1) * HEAD_BLOCK)
        qh = q[:, sl]
        if rope:
            qh = qh * cos + _swap_halves(qh) * sin
        q_ref[0, :, sl] = (qh * (ATTN_SCALE * LOG2E)).astype(BF16)
        k_ref[0, :, sl] = (kn[:, sl] + kr).astype(BF16)


def _premix_call(x, mod, lw, layer, tm, rope_tabs=None, moe=None, mod_prev=None):
    b, t, _ = x.shape
    rope = rope_tabs is not None
    resid = moe is not None
    nt = t // tm
    per_b = (lambda i, j: (i, 0, 0)) if mod.shape[0] > 1 else (lambda i, j: (0, 0, 0))
    const2 = lambda i, j: (0, 0)
    tok = lambda i, j: (i, j, 0)
    v_cols = N_HEADS * (HEAD_BLOCK if rope else V_HEAD)
    in_specs = [pl.BlockSpec((1, tm, D_MODEL), tok)]
    args = [x]
    if resid:
        in_specs += [pl.BlockSpec((1, tm, D_MODEL // 2), tok),
                     pl.BlockSpec((1, N_MOD, D_MODEL), per_b)]
        args += [moe, mod_prev]
    weights = [lw["n1g"], lw["win"], lw["qg"], lw["wuq"], lw["kvg"], lw["wuk"],
               lw["wuv_ones"] if rope else lw["wuv"]]
    in_specs += [pl.BlockSpec((1, N_MOD, D_MODEL), per_b)]
    in_specs += [_layer_spec(w, layer) for w in weights]
    args += [mod] + weights
    out_specs = [
        pl.BlockSpec((1, tm, N_HEADS * HEAD_BLOCK), tok),
        pl.BlockSpec((1, tm, N_HEADS * HEAD_BLOCK), tok),
        pl.BlockSpec((1, tm, v_cols), tok),
        pl.BlockSpec((1, tm, C_CONV), tok),
    ]
    out_shape = [
        jax.ShapeDtypeStruct((b, t, N_HEADS * HEAD_BLOCK), BF16),
        jax.ShapeDtypeStruct((b, t, N_HEADS * HEAD_BLOCK), BF16),
        jax.ShapeDtypeStruct((b, t, v_cols), BF16),
        jax.ShapeDtypeStruct((b, t, C_CONV), F32),
    ]
    if resid:
        out_specs.append(pl.BlockSpec((1, tm, D_MODEL), tok))
        out_shape.append(jax.ShapeDtypeStruct((b, t, D_MODEL), F32))
    if rope:
        in_specs += [
            pl.BlockSpec((tm, HEAD_BLOCK), lambda i, j: (j, 0)),
            pl.BlockSpec((tm, HEAD_BLOCK), lambda i, j: (j, 0)),
        ]
        args += [rope_tabs[0], rope_tabs[1]]
    else:
        out_specs += [
            pl.BlockSpec((1, tm, KV_LORA), tok),
            pl.BlockSpec((1, tm, QK_ROPE), tok),
        ]
        out_shape += [
            jax.ShapeDtypeStruct((b, t, KV_LORA), F32),
            jax.ShapeDtypeStruct((b, t, QK_ROPE), F32),
        ]
    return pl.pallas_call(
        functools.partial(_premix_kernel, rope, resid),
        grid=(b, nt),
        in_specs=in_specs,
        out_specs=out_specs,
        out_shape=out_shape,
        compiler_params=_params("parallel", "parallel"),
        name="premix_rope" if rope else "premix",
    )(*args)


def _ctx_kernel(ckv_ref, kr_ref, wuk_ref, wuv_ref, k_ref, v_ref):
    ckv_b = ckv_ref[...].astype(BF16)
    kn = jnp.dot(ckv_b, wuk_ref[...], preferred_element_type=F32)
    v = jnp.dot(ckv_b, wuv_ref[...], preferred_element_type=F32)
    v_ref[0] = (v + _ones_lane(v.shape[1])).astype(BF16)
    kr = kr_ref[...]
    for hd in range(N_HEADS):
        sl = slice(hd * HEAD_BLOCK, (hd + 1) * HEAD_BLOCK)
        k_ref[0, :, sl] = (kn[:, sl] + kr).astype(BF16)


def _ctx_call(cache_ckv, cache_kr_pad, layer, lw):
    b, _, s, _ = cache_ckv.shape
    return pl.pallas_call(
        _ctx_kernel,
        grid=(b,),
        in_specs=[
            pl.BlockSpec((None, None, s, KV_LORA), lambda i: (i, layer, 0, 0)),
            pl.BlockSpec((None, None, s, HEAD_BLOCK), lambda i: (i, layer, 0, 0)),
            _layer_spec(lw["wuk"], layer),
            _layer_spec(lw["wuv_ones"], layer),
        ],
        out_specs=[
            pl.BlockSpec((1, s, N_HEADS * HEAD_BLOCK), lambda i: (i, 0, 0)),
            pl.BlockSpec((1, s, N_HEADS * HEAD_BLOCK), lambda i: (i, 0, 0)),
        ],
        out_shape=[
            jax.ShapeDtypeStruct((b, s, N_HEADS * HEAD_BLOCK), BF16),
            jax.ShapeDtypeStruct((b, s, N_HEADS * HEAD_BLOCK), BF16),
        ],
        compiler_params=_params("parallel"),
        name="ctx_kv",
    )(cache_ckv, cache_kr_pad, lw["wuk"], lw["wuv_ones"])


_NT = (((1,), (1,)), ((), ()))


def _attn_kernel(ctx, q_ref, k_ref, v_ref, *rest):
    if ctx:
        kc_ref, vc_ref, o_ref = rest
    else:
        (o_ref,) = rest
    outs = []
    for hd in range(N_HEADS):
        sl = slice(hd * HEAD_BLOCK, (hd + 1) * HEAD_BLOCK)
        qh = q_ref[0, :, sl]
        s = lax.dot_general(qh, k_ref[0, :, sl], _NT, preferred_element_type=F32)
        m = jnp.max(s, axis=-1, keepdims=True)
        if ctx:
            sc = lax.dot_general(qh, kc_ref[0, :, sl], _NT, preferred_element_type=F32)
            m = jnp.maximum(m, jnp.max(sc, axis=-1, keepdims=True))
        p = jnp.exp2(s - m)
        if ctx:
            pc = jnp.exp2(sc - m)
            o = jnp.dot(p.astype(BF16), v_ref[0, :, sl], preferred_element_type=F32)
            o = o + jnp.dot(pc.astype(BF16), vc_ref[0, :, sl], preferred_element_type=F32)
            outs.append(o[:, :V_HEAD] / o[:, V_HEAD:V_HEAD + 1])
        else:
            vs = slice(hd * V_HEAD, (hd + 1) * V_HEAD)
            o = jnp.dot(p.astype(BF16), v_ref[0, :, vs], preferred_element_type=F32)
            outs.append(o / jnp.sum(p, axis=-1, keepdims=True))
    o_ref[0] = jnp.concatenate(outs, axis=-1).astype(BF16)


def _attn_call(q, k, v, tq, ctx_kv=None):
    b, t, _ = q.shape
    ctx = ctx_kv is not None
    tok = lambda i, j: (i, j, 0)
    seq = lambda i, j: (i, 0, 0)
    in_specs = [
        pl.BlockSpec((1, tq, N_HEADS * HEAD_BLOCK), tok),
        pl.BlockSpec((1, t, N_HEADS * HEAD_BLOCK), seq),
        pl.BlockSpec((1, t, v.shape[2]), seq),
    ]
    args = [q, k, v]
    if ctx:
        s = ctx_kv[0].shape[1]
        in_specs += [
            pl.BlockSpec((1, s, N_HEADS * HEAD_BLOCK), seq),
            pl.BlockSpec((1, s, N_HEADS * HEAD_BLOCK), seq),
        ]
        args += list(ctx_kv)
    return pl.pallas_call(
        functools.partial(_attn_kernel, ctx),
        grid=(b, t // tq),
        in_specs=in_specs,
        out_specs=pl.BlockSpec((1, tq, N_HEADS * V_HEAD), tok),
        out_shape=jax.ShapeDtypeStruct((b, t, N_HEADS * V_HEAD), BF16),
        compiler_params=_params("parallel", "arbitrary"),
        name="attn_ctx" if ctx else "attn",
    )(*args)


CONV_ROWS = 64


def _route(aff, bias):
    sel = aff + bias
    rows = [sel[e:e + 1] for e in range(N_EXPERTS)]
    affr = [aff[e:e + 1] for e in range(N_EXPERTS)]
    scores = []
    for g in range(N_GROUPS):
        a, b, c, d = rows[g * EXPERTS_PER_GROUP:(g + 1) * EXPERTS_PER_GROUP]
        hi1, lo1 = jnp.maximum(a, b), jnp.minimum(a, b)
        hi2, lo2 = jnp.maximum(c, d), jnp.minimum(c, d)
        top1 = jnp.maximum(hi1, hi2)
        top2 = jnp.maximum(jnp.minimum(hi1, hi2), jnp.maximum(lo1, lo2))
        scores.append(top1 + top2)
    best = scores[0]
    gidx = jnp.zeros_like(best, dtype=jnp.int32)
    for g in range(1, N_GROUPS):
        better = scores[g] > best
        best = jnp.where(better, scores[g], best)
        gidx = jnp.where(better, g, gidx)
    gates = []
    e_lo = jnp.full_like(best, float(N_EXPERTS))
    e_hi = jnp.full_like(best, -1.0)
    for g in range(N_GROUPS):
        in_g = gidx == g
        s = rows[g * EXPERTS_PER_GROUP:(g + 1) * EXPERTS_PER_GROUP]
        af = affr[g * EXPERTS_PER_GROUP:(g + 1) * EXPERTS_PER_GROUP]
        picked = []
        for i in range(EXPERTS_PER_GROUP):
            rank = jnp.zeros_like(gidx)
            for j in range(EXPERTS_PER_GROUP):
                if j == i:
                    continue
                ahead = (s[j] >= s[i]) if j < i else (s[j] > s[i])
                rank = rank + ahead.astype(jnp.int32)
            chosen = (rank < 2) & in_g
            eid = float(g * EXPERTS_PER_GROUP + i)
            e_lo = jnp.where(chosen, jnp.minimum(e_lo, eid), e_lo)
            e_hi = jnp.where(chosen, jnp.maximum(e_hi, eid), e_hi)
            picked.append(jnp.where(chosen, af[i], 0.0))
        den = (picked[0] + picked[1]) + (picked[2] + picked[3])
        den = jnp.where(in_g, den, 1.0)
        gates += [pk / den for pk in picked]
    return jnp.concatenate(gates, axis=0), e_lo, e_hi


def _postmix_kernel(nt, x_ref, attn_ref, z_ref, mod_ref, cw_ref, cb_ref, lg_ref, lb_ref, wout_ref,
                    n2g_ref, rwt_ref, rb_ref, x1_ref, h2_ref, meta_ref, ids_ref, zp_ref, zs_ref, cz_ref):
    nseq, tm = x_ref.shape[0], x_ref.shape[1]
    rows = nseq * tm
    j = pl.program_id(1)
    t0 = pl.multiple_of(j * tm, tm)
    zeros = jnp.zeros((HALO, C_CONV), F32)
    cw = cw_ref[...]
    cb = cb_ref[...]
    span = zs_ref.shape[1]

    for sq in range(nseq):
        zp_ref[HALO:HALO + tm, :] = z_ref[sq, pl.ds(t0, tm), :]

        @pl.when(j == 0)
        def _():
            zp_ref[0:HALO, :] = zeros

        @pl.when(j > 0)
        def _(sq=sq):
            zp_ref[0:HALO, :] = z_ref[sq, pl.ds(t0 - HALO, HALO), :]

        @pl.when(j == nt - 1)
        def _():
            zp_ref[HALO + tm:, :] = zeros

        @pl.when(j < nt - 1)
        def _(sq=sq):
            zp_ref[HALO + tm:, :] = z_ref[sq, pl.ds(t0 + tm, HALO), :]

        for o in range(1, SUBLANES):
            zs_ref[o - 1] = zp_ref[o:o + span, :]

        for r0 in range(0, tm, CONV_ROWS):
            acc = jnp.broadcast_to(cb, (CONV_ROWS, C_CONV))
            for kk in range(CONV_K):
                off = HALO - CONV_PAD + kk
                o = off % SUBLANES
                base = r0 + off - o
                if o == 0:
                    tap = zp_ref[base:base + CONV_ROWS, :]
                else:
                    tap = zs_ref[o - 1, base:base + CONV_ROWS, :]
                acc = acc + tap * cw[kk:kk + 1, :]
            cz_ref[sq * tm + r0:sq * tm + r0 + CONV_ROWS, :] = acc

    cz = cz_ref[...]
    mu = jnp.mean(cz, axis=-1, keepdims=True)
    cen = cz - mu
    var = jnp.mean(cen * cen, axis=-1, keepdims=True)
    conv = _silu(cen * lax.rsqrt(var + EPS) * lg_ref[...] + lb_ref[...]).astype(BF16)

    n_attn = N_HEADS * V_HEAD
    y = jnp.dot(attn_ref[...].reshape(rows, n_attn), wout_ref[0:n_attn, :],
                preferred_element_type=F32)
    y = y + jnp.dot(conv, wout_ref[n_attn:, :], preferred_element_type=F32)

    mod = mod_ref[0]
    g1, sh2, sc2 = mod[2:3], mod[3:4], mod[4:5]
    x1 = x_ref[...].reshape(rows, D_MODEL) + g1 * y
    x1_ref[...] = x1.reshape(x1_ref.shape)
    h2 = _rms(x1, n2g_ref[...]) * (1.0 + sc2) + sh2
    h2_ref[...] = _pack_bf16_pairs(h2).reshape(h2_ref.shape)

    logits = lax.dot_general(rwt_ref[...], h2, _NT, precision=lax.Precision.HIGHEST,
                             preferred_element_type=F32)
    gates, e_lo, e_hi = _route(jax.nn.sigmoid(logits), rb_ref[...])
    row = lax.broadcasted_iota(jnp.int32, (SUBLANES, rows), 0)
    ids = jnp.where(row == 0, e_lo, jnp.where(row == 1, e_hi, 0.0))
    ids_ref[...] = ids
    rec = jnp.concatenate([gates, ids, jnp.zeros((LANES - N_EXPERTS - SUBLANES, rows), F32)], axis=0)
    meta_ref[...] = rec.T.reshape(meta_ref.shape)


def _postmix_call(x, attn, z, mod, lw, layer, rwt, rb, tm, nseq=1):
    b, t, _ = x.shape
    nt = t // tm
    assert nseq == 1 or (nt == 1 and mod.shape[0] == 1)
    per_b = (lambda i, j: (i, 0, 0)) if mod.shape[0] > 1 else (lambda i, j: (0, 0, 0))
    const2 = lambda i, j: (0, 0)
    tok = lambda i, j: (i, j, 0)
    weights = [lw["conv_w"], lw["conv_b"], lw["ln_g"], lw["ln_b"], lw["wout"], lw["n2g"]]
    return pl.pallas_call(
        functools.partial(_postmix_kernel, nt),
        grid=(b // nseq, nt),
        in_specs=[
            pl.BlockSpec((nseq, tm, D_MODEL), tok),
            pl.BlockSpec((nseq, tm, N_HEADS * V_HEAD), tok),
            pl.BlockSpec((nseq, t, C_CONV), lambda i, j: (i, 0, 0)),
            pl.BlockSpec((1, N_MOD, D_MODEL), per_b),
        ] + [_layer_spec(w, layer) for w in weights] + [
            pl.BlockSpec((N_EXPERTS, D_MODEL), const2),
            pl.BlockSpec((N_EXPERTS, 1), const2),
        ],
        out_specs=[
            pl.BlockSpec((nseq, tm, D_MODEL), tok),
            pl.BlockSpec((nseq, tm, D_MODEL // 2), tok),
            pl.BlockSpec((nseq, tm, LANES), tok),
            pl.BlockSpec((SUBLANES, nseq * tm), lambda i, j: (0, i * nt + j)),
        ],
        out_shape=[
            jax.ShapeDtypeStruct((b, t, D_MODEL), F32),
            jax.ShapeDtypeStruct((b, t, D_MODEL // 2), jnp.int32),
            jax.ShapeDtypeStruct((b, t, LANES), F32),
            jax.ShapeDtypeStruct((SUBLANES, b * t), F32),
        ],
        scratch_shapes=[
            pltpu.VMEM((tm + 2 * HALO, C_CONV), F32),
            pltpu.VMEM((SUBLANES - 1, tm + 2 * HALO - SUBLANES, C_CONV), F32),
            pltpu.VMEM((nseq * tm, C_CONV), F32),
        ],
        compiler_params=_params("parallel", "arbitrary"),
        name="postmix",
    )(x, attn, z, mod, *weights, rwt, rb)


N_PAIRS = EXPERTS_PER_GROUP * (EXPERTS_PER_GROUP - 1) // 2
N_CLASSES = N_GROUPS * N_PAIRS
_PAIR_LO = np.array([i for i in range(EXPERTS_PER_GROUP) for j in range(i + 1, EXPERTS_PER_GROUP)])
_PAIR_HI = np.array([j for i in range(EXPERTS_PER_GROUP) for j in range(i + 1, EXPERTS_PER_GROUP)])
MOE_ROWS = 256
MOE_STEP_TILES = 4
SC_CORES = 2
SC_SUBCORES = 16
SC_WORKERS = SC_CORES * SC_SUBCORES
SC_WINDOW = 32


def _slot_rows(n):
    return n + N_CLASSES * MOE_ROWS


def _plan(ids, n_slots):
    e_lo = ids[0].astype(jnp.int32)
    e_hi = ids[1].astype(jnp.int32)
    grp = e_lo // EXPERTS_PER_GROUP
    i = e_lo % EXPERTS_PER_GROUP
    j = e_hi % EXPERTS_PER_GROUP
    cls = grp * N_PAIRS + (i * (2 * EXPERTS_PER_GROUP - 1 - i)) // 2 + (j - i - 1)
    onehot = (cls[None, :] == jnp.arange(N_CLASSES, dtype=jnp.int32)[:, None]).astype(jnp.int32)
    csum = jnp.cumsum(onehot, axis=1)
    counts = csum[:, -1]
    padded = (counts + MOE_ROWS - 1) // MOE_ROWS * MOE_ROWS
    pend = jnp.cumsum(padded)
    pstart = pend - padded
    pos = jnp.sum(onehot * (csum - 1 + pstart[:, None]), axis=0).astype(jnp.int32)
    tile_start = jnp.arange(n_slots // MOE_ROWS, dtype=jnp.int32) * MOE_ROWS
    tcls = jnp.minimum(jnp.sum(tile_start[:, None] >= pend[None, :], axis=1), N_CLASSES - 1)
    tgrp, tpair = tcls // N_PAIRS, tcls % N_PAIRS
    t_lo = tgrp * EXPERTS_PER_GROUP + jnp.asarray(_PAIR_LO, jnp.int32)[tpair]
    t_hi = tgrp * EXPERTS_PER_GROUP + jnp.asarray(_PAIR_HI, jnp.int32)[tpair]
    tile_experts = jnp.stack([t_lo, t_hi]).astype(jnp.int32)
    n_used = (pend[-1] // MOE_ROWS).astype(jnp.int32).reshape(1)
    return pos, tile_experts, n_used


def _sc_mesh():
    return plsc.VectorSubcoreMesh(core_axis_name="core", subcore_axis_name="subcore")


def _sc_worker():
    return lax.axis_index("core") * SC_SUBCORES + lax.axis_index("subcore")


def _dispatch_call(h2, meta, pos, n_slots):
    n = h2.shape[0]
    per_worker = n // SC_WORKERS

    @pl.kernel(
        out_type=[jax.ShapeDtypeStruct((n_slots, h2.shape[1]), h2.dtype),
                  jax.ShapeDtypeStruct((n_slots, LANES), F32)],
        mesh=_sc_mesh(),
        scratch_types=[pltpu.VMEM((per_worker,), jnp.int32),
                       pltpu.VMEM((SC_WINDOW, h2.shape[1]), h2.dtype),
                       pltpu.VMEM((SC_WINDOW, LANES), F32)],
    )
    def dispatch(h2_hbm, meta_hbm, pos_hbm, xs_hbm, ms_hbm, slot_v, row_v, rec_v):
        wid = _sc_worker()
        base = wid * per_worker
        pltpu.sync_copy(pos_hbm.at[wid], slot_v)

        @pl.loop(0, per_worker // SC_WINDOW)
        def _(step):
            rows = pl.ds(base + step * SC_WINDOW, SC_WINDOW)
            slots = slot_v.at[pl.ds(step * SC_WINDOW, SC_WINDOW)]
            pltpu.sync_copy(h2_hbm.at[rows], row_v)
            pltpu.sync_copy(row_v, xs_hbm.at[slots])
            pltpu.sync_copy(meta_hbm.at[rows], rec_v)
            pltpu.sync_copy(rec_v, ms_hbm.at[slots])

    return dispatch(h2, meta, pos.reshape(SC_WORKERS, per_worker))


def _combine_call(ys, pos):
    n = pos.shape[0]
    per_worker = n // SC_WORKERS

    @pl.kernel(
        out_type=jax.ShapeDtypeStruct((n, ys.shape[1]), ys.dtype),
        mesh=_sc_mesh(),
        scratch_types=[pltpu.VMEM((per_worker,), jnp.int32),
                       pltpu.VMEM((SC_WINDOW, ys.shape[1]), ys.dtype)],
    )
    def combine(ys_hbm, pos_hbm, out_hbm, slot_v, row_v):
        wid = _sc_worker()
        base = wid * per_worker
        pltpu.sync_copy(pos_hbm.at[wid], slot_v)

        @pl.loop(0, per_worker // SC_WINDOW)
        def _(step):
            slots = slot_v.at[pl.ds(step * SC_WINDOW, SC_WINDOW)]
            pltpu.sync_copy(ys_hbm.at[slots], row_v)
            pltpu.sync_copy(row_v, out_hbm.at[pl.ds(base + step * SC_WINDOW, SC_WINDOW)])

    return combine(ys, pos.reshape(SC_WORKERS, per_worker))


def _last_used_step(nu_ref):
    return jnp.maximum((nu_ref[0] - 1) // MOE_STEP_TILES, 0)


def _experts_kernel(te_ref, nu_ref, xs_ref, ms_ref, wg_ref, wu_ref, wd_ref, ys_ref):
    step = pl.program_id(0)
    live = step <= _last_used_step(nu_ref)
    for sub in range(MOE_STEP_TILES):
        tile = step * MOE_STEP_TILES + sub
        rows = slice(sub * MOE_ROWS, (sub + 1) * MOE_ROWS)

        @pl.when(tile < nu_ref[0])
        def _(tile=tile, rows=rows):
            x = _unpack_bf16_pairs(xs_ref[rows, :])
            rec = ms_ref[rows, :]
            lane = lax.broadcasted_iota(jnp.int32, rec.shape, 1)

            def expert(e):
                gate = jnp.sum(jnp.where(lane == e, rec, 0.0), axis=-1, keepdims=True)
                hg = jnp.dot(x, wg_ref[e], preferred_element_type=F32)
                hu = jnp.dot(x, wu_ref[e], preferred_element_type=F32)
                hid = (_silu(hg) * hu * gate).astype(BF16)
                return jnp.dot(hid, wd_ref[e], preferred_element_type=F32)

            y = expert(te_ref[0, tile]) + expert(te_ref[1, tile])
            ys_ref[rows, :] = _pack_bf16_pairs(y)

        @pl.when((tile >= nu_ref[0]) & live)
        def _(rows=rows):
            ys_ref[rows, :] = jnp.zeros((MOE_ROWS, D_MODEL // 2), jnp.int32)


def _experts_call(xs, ms, tile_experts, n_used, layer, moe_w):
    n_slots = xs.shape[0]
    step_rows = MOE_ROWS * MOE_STEP_TILES
    rows = lambda i, te, nu: (jnp.minimum(i, _last_used_step(nu)), 0)

    def resident(w):
        return pl.BlockSpec((None,) + w.shape[1:], lambda i, te, nu: (layer, 0, 0, 0),
                            pipeline_mode=pl.Buffered(1))

    wg, wu, wd = moe_w
    return pl.pallas_call(
        _experts_kernel,
        grid_spec=pltpu.PrefetchScalarGridSpec(
            num_scalar_prefetch=2,
            grid=(n_slots // step_rows,),
            in_specs=[
                pl.BlockSpec((step_rows, D_MODEL // 2), rows),
                pl.BlockSpec((step_rows, LANES), rows),
                resident(wg), resident(wu), resident(wd),
            ],
            out_specs=pl.BlockSpec((step_rows, D_MODEL // 2), rows),
        ),
        out_shape=jax.ShapeDtypeStruct((n_slots, D_MODEL // 2), jnp.int32),
        compiler_params=_params("arbitrary"),
        name="experts",
    )(tile_experts, n_used, xs, ms, wg, wu, wd)


def _routed_moe(h2, meta, ids, layer, moe_w):
    b, t, _ = h2.shape
    n = b * t
    n_slots = _slot_rows(n)
    pos, tile_experts, n_used = _plan(ids, n_slots)
    xs, ms = _dispatch_call(h2.reshape(n, h2.shape[2]), meta.reshape(n, LANES), pos, n_slots)
    ys = _experts_call(xs, ms, tile_experts, n_used, layer, moe_w)
    return _combine_call(ys, pos).reshape(b, t, ys.shape[1])


def _final_kernel(x1_ref, moe_ref, mod_ref, fg_ref, o_ref):
    x2 = x1_ref[0] + mod_ref[0][5:6] * _unpack_bf16_pairs(moe_ref[0], F32)
    o_ref[0] = _rms(x2, fg_ref[...])


def _final_call(x1, moe, mod, final_g, tm):
    b, t, _ = x1.shape
    per_b = (lambda i, j: (i, 0, 0)) if mod.shape[0] > 1 else (lambda i, j: (0, 0, 0))
    tok = lambda i, j: (i, j, 0)
    return pl.pallas_call(
        _final_kernel,
        grid=(b, t // tm),
        in_specs=[
            pl.BlockSpec((1, tm, D_MODEL), tok),
            pl.BlockSpec((1, tm, D_MODEL // 2), tok),
            pl.BlockSpec((1, N_MOD, D_MODEL), per_b),
            pl.BlockSpec((1, D_MODEL), lambda i, j: (0, 0)),
        ],
        out_specs=pl.BlockSpec((1, tm, D_MODEL), tok),
        out_shape=jax.ShapeDtypeStruct((b, t, D_MODEL), F32),
        compiler_params=_params("parallel", "parallel"),
        name="final_norm",
    )(x1, moe, mod, final_g)


def _prep_weights(norm1_g, w_in, q_norm_g, w_uq, kv_norm_g, w_ukv, conv_w, conv_b, conv_ln_g,
                  conv_ln_b, w_out, norm2_g):
    o_kv, o_kr, o_a, o_g = Q_LORA, Q_LORA + KV_LORA, Q_LORA + KV_LORA + QK_ROPE, Q_LORA + KV_LORA + QK_ROPE + C_CONV
    pad_r = ((0, 0), (0, 0), (0, HEAD_BLOCK - QK_ROPE))
    win = jnp.concatenate(
        [w_in[..., :o_kv], w_in[..., o_kv:o_kr], w_in[..., o_a:o_g], w_in[..., o_g:],
         jnp.pad(w_in[..., o_kr:o_a], pad_r)], axis=-1).astype(BF16)

    wuq = w_uq.reshape(DEPTH, Q_LORA, N_HEADS, QK_NOPE + QK_ROPE)
    nope, rope = wuq[..., :QK_NOPE], wuq[..., QK_NOPE:]
    zpad = jnp.zeros((DEPTH, Q_LORA, N_HEADS, HEAD_BLOCK - QK_NOPE - QK_ROPE), F32)
    wuq_p = jnp.concatenate([rope, nope, zpad], axis=-1).reshape(DEPTH, Q_LORA, -1).astype(BF16)

    wukv = w_ukv.reshape(DEPTH, KV_LORA, N_HEADS, QK_NOPE + V_HEAD)
    kz0 = jnp.zeros((DEPTH, KV_LORA, N_HEADS, QK_ROPE), F32)
    kz1 = jnp.zeros((DEPTH, KV_LORA, N_HEADS, HEAD_BLOCK - QK_NOPE - QK_ROPE), F32)
    wuk = jnp.concatenate([kz0, wukv[..., :QK_NOPE], kz1], axis=-1)
    wuk = wuk.reshape(DEPTH, KV_LORA, -1).astype(BF16)
    wuv = wukv[..., QK_NOPE:].reshape(DEPTH, KV_LORA, -1).astype(BF16)
    vz = jnp.zeros((DEPTH, KV_LORA, N_HEADS, HEAD_BLOCK - V_HEAD), F32)
    wuv_ones = jnp.concatenate([wukv[..., QK_NOPE:], vz], axis=-1)
    wuv_ones = wuv_ones.reshape(DEPTH, KV_LORA, -1).astype(BF16)

    row = lambda a: a[:, None, :]
    return {
        "n1g": row(norm1_g), "win": win, "qg": row(q_norm_g), "wuq": wuq_p,
        "kvg": row(kv_norm_g), "wuk": wuk, "wuv": wuv, "wuv_ones": wuv_ones,
        "conv_w": conv_w, "conv_b": row(conv_b), "ln_g": row(conv_ln_g),
        "ln_b": row(conv_ln_b), "wout": w_out.astype(BF16), "n2g": row(norm2_g),
    }


def _layer_spec(w, layer):
    return pl.BlockSpec((None,) + w.shape[1:], lambda *_: (layer,) + (0,) * (w.ndim - 1))


def _rope_tables(n_tokens):
    rows = n_tokens // GRID_W
    row = jnp.repeat(jnp.arange(rows), GRID_W).astype(F32)
    col = jnp.tile(jnp.arange(GRID_W), rows).astype(F32)
    freqs = ROPE_BASE ** (-jnp.arange(ROPE_PAIRS, dtype=F32) / ROPE_PAIRS)
    ar, ac = row[:, None] * freqs, col[:, None] * freqs
    cr, sr, cc, sc = jnp.cos(ar), jnp.sin(ar), jnp.cos(ac), jnp.sin(ac)
    rest = HEAD_BLOCK - QK_ROPE
    cos = jnp.concatenate([cr, cr, cc, cc, jnp.ones((n_tokens, rest), F32)], axis=1)
    sin = jnp.concatenate([-sr, sr, -sc, sc, jnp.zeros((n_tokens, rest), F32)], axis=1)
    return cos, sin


TM_PROMPT = 256
PROMPT_SEQS_PER_STEP = 2
TM_SAMPLE = 512
TQ_SAMPLE = 512


def _fold(a):
    return None if a is None else a.reshape(a.shape[0] // 2, a.shape[1] * 2, a.shape[2])


def _unfold(a):
    return a.reshape(a.shape[0] * 2, a.shape[1] // 2, a.shape[2])


def kernel(x_prompt, x_sample, cache_ckv, cache_krope, c, c_ctx, w_ada, b_ada, norm1_g, w_in,
           q_norm_g, w_uq, kv_norm_g, w_ukv, conv_w, conv_b, conv_ln_g, conv_ln_b, w_out,
           norm2_g, router_w, router_b, w_gate, w_up, w_down, final_g):
    dec_b = x_sample.shape[0]
    cvecs = jnp.concatenate(
        [c_ctx[None], c, jnp.zeros((MOD_ROWS - 1 - dec_b, D_MODEL), F32)], axis=0)
    mod_all = _ada_call(cvecs, w_ada, b_ada).reshape(DEPTH, MOD_ROWS, N_MOD, D_MODEL)

    moe_w = (w_gate.astype(BF16), w_up.astype(BF16), w_down.astype(BF16))
    rwt = router_w.T
    rb = router_b[:, None]
    fg = final_g[None]
    rope_tabs = _rope_tables(x_sample.shape[1])
    cache_kr_pad = jnp.pad(cache_krope, ((0, 0), (0, 0), (0, 0), (0, HEAD_BLOCK - QK_ROPE)))

    xp, xs = x_prompt, x_sample
    moe_p = moe_s = mod_p = mod_s = None
    ckvs, krs = [], []
    lw = _prep_weights(norm1_g, w_in, q_norm_g, w_uq, kv_norm_g, w_ukv, conv_w, conv_b,
                       conv_ln_g, conv_ln_b, w_out, norm2_g)
    for l in range(DEPTH):
        prev_p, prev_s = mod_p, mod_s
        mod_p = mod_all[l, 0:1]
        mod_s = mod_all[l, 1:1 + dec_b]

        outs = _premix_call(xs, mod_s, lw, l, TM_SAMPLE, rope_tabs, moe=moe_s, mod_prev=prev_s)
        if l > 0:
            q, k, v, z, xs = outs
        else:
            q, k, v, z = outs
        ctx_kv = _ctx_call(cache_ckv, cache_kr_pad, l, lw)
        attn = _attn_call(q, k, v, TQ_SAMPLE, ctx_kv)
        xs, h2, meta, ids = _postmix_call(xs, attn, z, mod_s, lw, l, rwt, rb, TM_SAMPLE)
        moe_s = _routed_moe(h2, meta, ids, l, moe_w)

        outs = _premix_call(_fold(xp), mod_p, lw, l, TM_SAMPLE, moe=_fold(moe_p),
                            mod_prev=prev_p)
        outs = [_unfold(o) for o in outs]
        if l > 0:
            q, k, v, z, xp, ckv, kr = outs
        else:
            q, k, v, z, ckv, kr = outs
        ckvs.append(ckv)
        krs.append(kr)
        attn = _attn_call(q, k, v, xp.shape[1])
        xp, h2, meta, ids = _postmix_call(xp, attn, z, mod_p, lw, l, rwt, rb, TM_PROMPT,
                                          nseq=PROMPT_SEQS_PER_STEP)
        moe_p = _routed_moe(h2, meta, ids, l, moe_w)

    y_sample = _final_call(xs, moe_s, mod_s, fg, TM_SAMPLE)
    y_prompt = _unfold(_final_call(_fold(xp), _fold(moe_p), mod_p, fg, TM_SAMPLE))
    return y_prompt, y_sample, jnp.stack(ckvs, axis=1), jnp.stack(krs, axis=1)
```

```python
import functools

import jax
import jax.numpy as jnp
import numpy as np
from jax import lax
from jax.experimental import pallas as pl
from jax.experimental.pallas import tpu as pltpu
from jax.experimental.pallas import tpu_sc as plsc

D_MODEL = 1024
DEPTH = 4
GRID_W = 64
N_HEADS = 8
QK_NOPE = 64
QK_ROPE = 32
V_HEAD = 64
Q_LORA = 384
KV_LORA = 256
C_CONV = 512
CONV_K = 31
CONV_PAD = CONV_K // 2
N_EXPERTS = 16
N_GROUPS = 4
EXPERTS_PER_GROUP = N_EXPERTS // N_GROUPS
D_EXPERT = 256
ROPE_BASE = 10000.0
ROPE_PAIRS = QK_ROPE // 4
EPS = 1e-6
ATTN_SCALE = (QK_NOPE + QK_ROPE) ** -0.5
LOG2E = 1.4426950408889634

LANES = 128
SUBLANES = 8
HEAD_BLOCK = LANES
N_MOD = 6
MOD_ROWS = 16
HALO = 16
VMEM_LIMIT = 48 * 1024 * 1024

BF16 = jnp.bfloat16
F32 = jnp.float32


def _rms(x, g):
    return x * lax.rsqrt(jnp.mean(x * x, axis=-1, keepdims=True) + EPS) * g


def _silu(x):
    return x * jax.nn.sigmoid(x)


def _ones_lane(width):
    lane = lax.broadcasted_iota(jnp.int32, (1, width), 1)
    return jnp.where(lane % HEAD_BLOCK == V_HEAD, 1.0, 0.0)


def _pack_bf16_pairs(x):
    half = x.shape[1] // 2
    xb = x.astype(BF16).astype(F32)
    hi = pltpu.bitcast(xb[:, :half], jnp.uint32)
    lo = pltpu.bitcast(xb[:, half:], jnp.uint32)
    return pltpu.bitcast(hi | (lo >> 16), jnp.int32)


def _unpack_bf16_pairs(w, dtype=BF16):
    u = pltpu.bitcast(w, jnp.uint32)
    hi = pltpu.bitcast(u & jnp.uint32(0xFFFF0000), F32).astype(dtype)
    lo = pltpu.bitcast(u << 16, F32).astype(dtype)
    return jnp.concatenate([hi, lo], axis=1)


def _params(*sem):
    return pltpu.CompilerParams(dimension_semantics=sem, vmem_limit_bytes=VMEM_LIMIT)


def _ada_kernel(c_ref, w_ref, b_ref, o_ref):
    s = _silu(c_ref[...]).astype(BF16)
    o_ref[...] = jnp.dot(s, w_ref[...].astype(BF16), preferred_element_type=F32) + b_ref[...]


def _ada_call(cvecs, w_ada, b_ada):
    return pl.pallas_call(
        _ada_kernel,
        grid=(DEPTH, N_MOD),
        in_specs=[
            pl.BlockSpec((MOD_ROWS, D_MODEL), lambda l, n: (0, 0)),
            pl.BlockSpec((None, D_MODEL, D_MODEL), lambda l, n: (l, 0, n)),
            pl.BlockSpec((None, 1, D_MODEL), lambda l, n: (l, 0, n)),
        ],
        out_specs=pl.BlockSpec((None, MOD_ROWS, D_MODEL), lambda l, n: (l, 0, n)),
        out_shape=jax.ShapeDtypeStruct((DEPTH, MOD_ROWS, N_MOD * D_MODEL), F32),
        compiler_params=_params("parallel", "parallel"),
        name="ada",
    )(cvecs, w_ada, b_ada.reshape(DEPTH, 1, N_MOD * D_MODEL))


W_IN_Q = 0
W_IN_KV = W_IN_Q + Q_LORA
W_IN_A = W_IN_KV + KV_LORA
W_IN_G = W_IN_A + C_CONV
W_IN_KR = W_IN_G + C_CONV
W_IN_COLS = W_IN_KR + HEAD_BLOCK


def _swap_halves(x):
    lane = lax.broadcasted_iota(jnp.int32, x.shape, 1)
    first = lane % (2 * ROPE_PAIRS) < ROPE_PAIRS
    from_right = pltpu.roll(x, HEAD_BLOCK - ROPE_PAIRS, 1)
    from_left = pltpu.roll(x, ROPE_PAIRS, 1)
    return jnp.where(first, from_right, from_left)


def _premix_kernel(rope, resid, x_ref, *rest):
    if resid:
        moe_ref, modprev_ref, *rest = rest
    mod_ref, n1g_ref, win_ref, qg_ref, wuq_ref, kvg_ref, wuk_ref, wuv_ref, *rest = rest
    if rope:
        cos_ref, sin_ref, *rest = rest
    q_ref, k_ref, v_ref, z_ref, *rest = rest
    if resid:
        x2_ref, *rest = rest
    if not rope:
        ckv_ref, kr_ref = rest
    x = x_ref[0]
    if resid:
        x = x + modprev_ref[0][5:6] * _unpack_bf16_pairs(moe_ref[0], F32)
        x2_ref[0] = x
    mod = mod_ref[0]
    sh1, sc1 = mod[0:1], mod[1:2]
    h = (_rms(x, n1g_ref[...]) * (1.0 + sc1) + sh1).astype(BF16)
    proj = jnp.dot(h, win_ref[...], preferred_element_type=F32)
    q_c = proj[:, W_IN_Q:W_IN_KV]
    kv_c = proj[:, W_IN_KV:W_IN_A]
    a = proj[:, W_IN_A:W_IN_G]
    gate = proj[:, W_IN_G:W_IN_KR]
    kr = proj[:, W_IN_KR:W_IN_COLS]

    z_ref[0] = a * jax.nn.sigmoid(gate)

    qn = _rms(q_c, qg_ref[...]).astype(BF16)
    q = jnp.dot(qn, wuq_ref[...], preferred_element_type=F32)
    ckv = _rms(kv_c, kvg_ref[...])
    ckv_b = ckv.astype(BF16)
    kn = jnp.dot(ckv_b, wuk_ref[...], preferred_element_type=F32)
    v = jnp.dot(ckv_b, wuv_ref[...], preferred_element_type=F32)

    if rope:
        v = v + _ones_lane(v.shape[1])
        cos = cos_ref[...]
        sin = sin_ref[...]
        kr = kr * cos + _swap_halves(kr) * sin
    else:
        ckv_ref[0] = ckv
        kr_ref[0] = kr[:, :QK_ROPE]
    v_ref[0] = v.astype(BF16)

    for hd in range(N_HEADS):
        sl = slice(hd * HEAD_BLOCK, (hd + 1) * HEAD_BLOCK)
        qh = q[:, sl]
        if rope:
            qh = qh * cos + _swap_halves(qh) * sin
        q_ref[0, :, sl] = (qh * (ATTN_SCALE * LOG2E)).astype(BF16)
        k_ref[0, :, sl] = (kn[:, sl] + kr).astype(BF16)


def _premix_call(x, mod, lw, tm, rope_tabs=None, moe=None, mod_prev=None):
    b, t, _ = x.shape
    rope = rope_tabs is not None
    resid = moe is not None
    nt = t // tm
    per_b = (lambda i, j: (i, 0, 0)) if mod.shape[0] > 1 else (lambda i, j: (0, 0, 0))
    const2 = lambda i, j: (0, 0)
    tok = lambda i, j: (i, j, 0)
    v_cols = N_HEADS * (HEAD_BLOCK if rope else V_HEAD)
    in_specs = [pl.BlockSpec((1, tm, D_MODEL), tok)]
    args = [x]
    if resid:
        in_specs += [pl.BlockSpec((1, tm, D_MODEL // 2), tok),
                     pl.BlockSpec((1, N_MOD, D_MODEL), per_b)]
        args += [moe, mod_prev]
    in_specs += [
        pl.BlockSpec((1, N_MOD, D_MODEL), per_b),
        pl.BlockSpec((1, D_MODEL), const2),
        pl.BlockSpec((D_MODEL, W_IN_COLS), const2),
        pl.BlockSpec((1, Q_LORA), const2),
        pl.BlockSpec((Q_LORA, N_HEADS * HEAD_BLOCK), const2),
        pl.BlockSpec((1, KV_LORA), const2),
        pl.BlockSpec((KV_LORA, N_HEADS * HEAD_BLOCK), const2),
        pl.BlockSpec((KV_LORA, v_cols), const2),
    ]
    args += [mod, lw["n1g"], lw["win"], lw["qg"], lw["wuq"], lw["kvg"], lw["wuk"],
             lw["wuv_ones"] if rope else lw["wuv"]]
    out_specs = [
        pl.BlockSpec((1, tm, N_HEADS * HEAD_BLOCK), tok),
        pl.BlockSpec((1, tm, N_HEADS * HEAD_BLOCK), tok),
        pl.BlockSpec((1, tm, v_cols), tok),
        pl.BlockSpec((1, tm, C_CONV), tok),
    ]
    out_shape = [
        jax.ShapeDtypeStruct((b, t, N_HEADS * HEAD_BLOCK), BF16),
        jax.ShapeDtypeStruct((b, t, N_HEADS * HEAD_BLOCK), BF16),
        jax.ShapeDtypeStruct((b, t, v_cols), BF16),
        jax.ShapeDtypeStruct((b, t, C_CONV), F32),
    ]
    if resid:
        out_specs.append(pl.BlockSpec((1, tm, D_MODEL), tok))
        out_shape.append(jax.ShapeDtypeStruct((b, t, D_MODEL), F32))
    if rope:
        in_specs += [
            pl.BlockSpec((tm, HEAD_BLOCK), lambda i, j: (j, 0)),
            pl.BlockSpec((tm, HEAD_BLOCK), lambda i, j: (j, 0)),
        ]
        args += [rope_tabs[0], rope_tabs[1]]
    else:
        out_specs += [
            pl.BlockSpec((1, tm, KV_LORA), tok),
            pl.BlockSpec((1, tm, QK_ROPE), tok),
        ]
        out_shape += [
            jax.ShapeDtypeStruct((b, t, KV_LORA), F32),
            jax.ShapeDtypeStruct((b, t, QK_ROPE), F32),
        ]
    return pl.pallas_call(
        functools.partial(_premix_kernel, rope, resid),
        grid=(b, nt),
        in_specs=in_specs,
        out_specs=out_specs,
        out_shape=out_shape,
        compiler_params=_params("parallel", "parallel"),
        name="premix_rope" if rope else "premix",
    )(*args)


def _ctx_kernel(ckv_ref, kr_ref, wuk_ref, wuv_ref, k_ref, v_ref):
    ckv_b = ckv_ref[...].astype(BF16)
    kn = jnp.dot(ckv_b, wuk_ref[...], preferred_element_type=F32)
    v = jnp.dot(ckv_b, wuv_ref[...], preferred_element_type=F32)
    v_ref[0] = (v + _ones_lane(v.shape[1])).astype(BF16)
    kr = kr_ref[...]
    for hd in range(N_HEADS):
        sl = slice(hd * HEAD_BLOCK, (hd + 1) * HEAD_BLOCK)
        k_ref[0, :, sl] = (kn[:, sl] + kr).astype(BF16)


def _ctx_call(cache_ckv, cache_kr_pad, layer, lw):
    b, _, s, _ = cache_ckv.shape
    const2 = lambda i: (0, 0)
    return pl.pallas_call(
        _ctx_kernel,
        grid=(b,),
        in_specs=[
            pl.BlockSpec((None, None, s, KV_LORA), lambda i: (i, layer, 0, 0)),
            pl.BlockSpec((None, None, s, HEAD_BLOCK), lambda i: (i, layer, 0, 0)),
            pl.BlockSpec((KV_LORA, N_HEADS * HEAD_BLOCK), const2),
            pl.BlockSpec((KV_LORA, N_HEADS * HEAD_BLOCK), const2),
        ],
        out_specs=[
            pl.BlockSpec((1, s, N_HEADS * HEAD_BLOCK), lambda i: (i, 0, 0)),
            pl.BlockSpec((1, s, N_HEADS * HEAD_BLOCK), lambda i: (i, 0, 0)),
        ],
        out_shape=[
            jax.ShapeDtypeStruct((b, s, N_HEADS * HEAD_BLOCK), BF16),
            jax.ShapeDtypeStruct((b, s, N_HEADS * HEAD_BLOCK), BF16),
        ],
        compiler_params=_params("parallel"),
        name="ctx_kv",
    )(cache_ckv, cache_kr_pad, lw["wuk"], lw["wuv_ones"])


_NT = (((1,), (1,)), ((), ()))


def _attn_kernel(ctx, q_ref, k_ref, v_ref, *rest):
    if ctx:
        kc_ref, vc_ref, o_ref = rest
    else:
        (o_ref,) = rest
    outs = []
    for hd in range(N_HEADS):
        sl = slice(hd * HEAD_BLOCK, (hd + 1) * HEAD_BLOCK)
        qh = q_ref[0, :, sl]
        s = lax.dot_general(qh, k_ref[0, :, sl], _NT, preferred_element_type=F32)
        m = jnp.max(s, axis=-1, keepdims=True)
        if ctx:
            sc = lax.dot_general(qh, kc_ref[0, :, sl], _NT, preferred_element_type=F32)
            m = jnp.maximum(m, jnp.max(sc, axis=-1, keepdims=True))
        p = jnp.exp2(s - m)
        if ctx:
            pc = jnp.exp2(sc - m)
            o = jnp.dot(p.astype(BF16), v_ref[0, :, sl], preferred_element_type=F32)
            o = o + jnp.dot(pc.astype(BF16), vc_ref[0, :, sl], preferred_element_type=F32)
            outs.append(o[:, :V_HEAD] / o[:, V_HEAD:V_HEAD + 1])
        else:
            vs = slice(hd * V_HEAD, (hd + 1) * V_HEAD)
            o = jnp.dot(p.astype(BF16), v_ref[0, :, vs], preferred_element_type=F32)
            outs.append(o / jnp.sum(p, axis=-1, keepdims=True))
    o_ref[0] = jnp.concatenate(outs, axis=-1).astype(BF16)


def _attn_call(q, k, v, tq, ctx_kv=None):
    b, t, _ = q.shape
    ctx = ctx_kv is not None
    tok = lambda i, j: (i, j, 0)
    seq = lambda i, j: (i, 0, 0)
    in_specs = [
        pl.BlockSpec((1, tq, N_HEADS * HEAD_BLOCK), tok),
        pl.BlockSpec((1, t, N_HEADS * HEAD_BLOCK), seq),
        pl.BlockSpec((1, t, v.shape[2]), seq),
    ]
    args = [q, k, v]
    if ctx:
        s = ctx_kv[0].shape[1]
        in_specs += [
            pl.BlockSpec((1, s, N_HEADS * HEAD_BLOCK), seq),
            pl.BlockSpec((1, s, N_HEADS * HEAD_BLOCK), seq),
        ]
        args += list(ctx_kv)
    return pl.pallas_call(
        functools.partial(_attn_kernel, ctx),
        grid=(b, t // tq),
        in_specs=in_specs,
        out_specs=pl.BlockSpec((1, tq, N_HEADS * V_HEAD), tok),
        out_shape=jax.ShapeDtypeStruct((b, t, N_HEADS * V_HEAD), BF16),
        compiler_params=_params("parallel", "arbitrary"),
        name="attn_ctx" if ctx else "attn",
    )(*args)


CONV_ROWS = 64


def _split_bf16(x):
    hi = x.astype(BF16)
    return hi, (x - hi.astype(F32)).astype(BF16)


def _dot_nt_bf16x3(a, b):
    ah, al = _split_bf16(a)
    bh, bl = _split_bf16(b)
    dot = lambda x, y: lax.dot_general(x, y, _NT, preferred_element_type=F32)
    return dot(ah, bh) + (dot(ah, bl) + dot(al, bh))


def _route(aff, bias):
    sel = aff + bias
    rows = [sel[e:e + 1] for e in range(N_EXPERTS)]
    affr = [aff[e:e + 1] for e in range(N_EXPERTS)]
    scores = []
    for g in range(N_GROUPS):
        a, b, c, d = rows[g * EXPERTS_PER_GROUP:(g + 1) * EXPERTS_PER_GROUP]
        hi1, lo1 = jnp.maximum(a, b), jnp.minimum(a, b)
        hi2, lo2 = jnp.maximum(c, d), jnp.minimum(c, d)
        top1 = jnp.maximum(hi1, hi2)
        top2 = jnp.maximum(jnp.minimum(hi1, hi2), jnp.maximum(lo1, lo2))
        scores.append(top1 + top2)
    best = scores[0]
    gidx = jnp.zeros_like(best, dtype=jnp.int32)
    for g in range(1, N_GROUPS):
        better = scores[g] > best
        best = jnp.where(better, scores[g], best)
        gidx = jnp.where(better, g, gidx)
    gates = []
    e_lo = jnp.full_like(best, float(N_EXPERTS))
    e_hi = jnp.full_like(best, -1.0)
    for g in range(N_GROUPS):
        in_g = gidx == g
        s = rows[g * EXPERTS_PER_GROUP:(g + 1) * EXPERTS_PER_GROUP]
        af = affr[g * EXPERTS_PER_GROUP:(g + 1) * EXPERTS_PER_GROUP]
        picked = []
        for i in range(EXPERTS_PER_GROUP):
            rank = jnp.zeros_like(gidx)
            for j in range(EXPERTS_PER_GROUP):
                if j == i:
                    continue
                ahead = (s[j] >= s[i]) if j < i else (s[j] > s[i])
                rank = rank + ahead.astype(jnp.int32)
            chosen = (rank < 2) & in_g
            eid = float(g * EXPERTS_PER_GROUP + i)
            e_lo = jnp.where(chosen, jnp.minimum(e_lo, eid), e_lo)
            e_hi = jnp.where(chosen, jnp.maximum(e_hi, eid), e_hi)
            picked.append(jnp.where(chosen, af[i], 0.0))
        den = (picked[0] + picked[1]) + (picked[2] + picked[3])
        den = jnp.where(in_g, den, 1.0)
        gates += [pk / den for pk in picked]
    return jnp.concatenate(gates, axis=0), e_lo, e_hi


def _postmix_kernel(nt, x_ref, attn_ref, z_ref, mod_ref, cw_ref, cb_ref, lg_ref, lb_ref, wout_ref,
                    n2g_ref, rwt_ref, rb_ref, x1_ref, h2_ref, meta_ref, ids_ref, zp_ref, zs_ref, cz_ref):
    nseq, tm = x_ref.shape[0], x_ref.shape[1]
    rows = nseq * tm
    j = pl.program_id(1)
    t0 = pl.multiple_of(j * tm, tm)
    zeros = jnp.zeros((HALO, C_CONV), F32)
    cw = cw_ref[...]
    cb = cb_ref[...]
    span = zs_ref.shape[1]

    for sq in range(nseq):
        zp_ref[HALO:HALO + tm, :] = z_ref[sq, pl.ds(t0, tm), :]

        @pl.when(j == 0)
        def _():
            zp_ref[0:HALO, :] = zeros

        @pl.when(j > 0)
        def _(sq=sq):
            zp_ref[0:HALO, :] = z_ref[sq, pl.ds(t0 - HALO, HALO), :]

        @pl.when(j == nt - 1)
        def _():
            zp_ref[HALO + tm:, :] = zeros

        @pl.when(j < nt - 1)
        def _(sq=sq):
            zp_ref[HALO + tm:, :] = z_ref[sq, pl.ds(t0 + tm, HALO), :]

        for o in range(1, SUBLANES):
            zs_ref[o - 1] = zp_ref[o:o + span, :]

        for r0 in range(0, tm, CONV_ROWS):
            acc = jnp.broadcast_to(cb, (CONV_ROWS, C_CONV))
            for kk in range(CONV_K):
                off = HALO - CONV_PAD + kk
                o = off % SUBLANES
                base = r0 + off - o
                if o == 0:
                    tap = zp_ref[base:base + CONV_ROWS, :]
                else:
                    tap = zs_ref[o - 1, base:base + CONV_ROWS, :]
                acc = acc + tap * cw[kk:kk + 1, :]
            cz_ref[sq * tm + r0:sq * tm + r0 + CONV_ROWS, :] = acc

    cz = cz_ref[...]
    mu = jnp.mean(cz, axis=-1, keepdims=True)
    cen = cz - mu
    var = jnp.mean(cen * cen, axis=-1, keepdims=True)
    conv = _silu(cen * lax.rsqrt(var + EPS) * lg_ref[...] + lb_ref[...]).astype(BF16)

    n_attn = N_HEADS * V_HEAD
    y = jnp.dot(attn_ref[...].reshape(rows, n_attn), wout_ref[0:n_attn, :],
                preferred_element_type=F32)
    y = y + jnp.dot(conv, wout_ref[n_attn:, :], preferred_element_type=F32)

    mod = mod_ref[0]
    g1, sh2, sc2 = mod[2:3], mod[3:4], mod[4:5]
    x1 = x_ref[...].reshape(rows, D_MODEL) + g1 * y
    x1_ref[...] = x1.reshape(x1_ref.shape)
    h2 = _rms(x1, n2g_ref[...]) * (1.0 + sc2) + sh2
    h2_ref[...] = _pack_bf16_pairs(h2).reshape(h2_ref.shape)

    logits = _dot_nt_bf16x3(rwt_ref[...], h2)
    gates, e_lo, e_hi = _route(jax.nn.sigmoid(logits), rb_ref[...])
    row = lax.broadcasted_iota(jnp.int32, (SUBLANES, rows), 0)
    ids = jnp.where(row == 0, e_lo, jnp.where(row == 1, e_hi, 0.0))
    ids_ref[...] = ids
    rec = jnp.concatenate([gates, ids, jnp.zeros((LANES - N_EXPERTS - SUBLANES, rows), F32)], axis=0)
    meta_ref[...] = rec.T.reshape(meta_ref.shape)


def _postmix_call(x, attn, z, mod, lw, rwt, rb, tm, nseq=1):
    b, t, _ = x.shape
    nt = t // tm
    assert nseq == 1 or (nt == 1 and mod.shape[0] == 1)
    per_b = (lambda i, j: (i, 0, 0)) if mod.shape[0] > 1 else (lambda i, j: (0, 0, 0))
    const2 = lambda i, j: (0, 0)
    tok = lambda i, j: (i, j, 0)
    return pl.pallas_call(
        functools.partial(_postmix_kernel, nt),
        grid=(b // nseq, nt),
        in_specs=[
            pl.BlockSpec((nseq, tm, D_MODEL), tok),
            pl.BlockSpec((nseq, tm, N_HEADS * V_HEAD), tok),
            pl.BlockSpec((nseq, t, C_CONV), lambda i, j: (i, 0, 0)),
            pl.BlockSpec((1, N_MOD, D_MODEL), per_b),
            pl.BlockSpec((CONV_K, C_CONV), const2),
            pl.BlockSpec((1, C_CONV), const2),
            pl.BlockSpec((1, C_CONV), const2),
            pl.BlockSpec((1, C_CONV), const2),
            pl.BlockSpec((D_MODEL, D_MODEL), const2),
            pl.BlockSpec((1, D_MODEL), const2),
            pl.BlockSpec((N_EXPERTS, D_MODEL), const2),
            pl.BlockSpec((N_EXPERTS, 1), const2),
        ],
        out_specs=[
            pl.BlockSpec((nseq, tm, D_MODEL), tok),
            pl.BlockSpec((nseq, tm, D_MODEL // 2), tok),
            pl.BlockSpec((nseq, tm, LANES), tok),
            pl.BlockSpec((SUBLANES, nseq * tm), lambda i, j: (0, i * nt + j)),
        ],
        out_shape=[
            jax.ShapeDtypeStruct((b, t, D_MODEL), F32),
            jax.ShapeDtypeStruct((b, t, D_MODEL // 2), jnp.int32),
            jax.ShapeDtypeStruct((b, t, LANES), F32),
            jax.ShapeDtypeStruct((SUBLANES, b * t), F32),
        ],
        scratch_shapes=[
            pltpu.VMEM((tm + 2 * HALO, C_CONV), F32),
            pltpu.VMEM((SUBLANES - 1, tm + 2 * HALO - SUBLANES, C_CONV), F32),
            pltpu.VMEM((nseq * tm, C_CONV), F32),
        ],
        compiler_params=_params("parallel", "arbitrary"),
        name="postmix",
    )(x, attn, z, mod, lw["conv_w"], lw["conv_b"], lw["ln_g"], lw["ln_b"], lw["wout"],
      lw["n2g"], rwt, rb)


N_PAIRS = EXPERTS_PER_GROUP * (EXPERTS_PER_GROUP - 1) // 2
N_CLASSES = N_GROUPS * N_PAIRS
_PAIR_LO = np.array([i for i in range(EXPERTS_PER_GROUP) for j in range(i + 1, EXPERTS_PER_GROUP)])
_PAIR_HI = np.array([j for i in range(EXPERTS_PER_GROUP) for j in range(i + 1, EXPERTS_PER_GROUP)])
MOE_ROWS = 256
MOE_STEP_TILES = 4
SC_CORES = 2
SC_SUBCORES = 16
SC_WORKERS = SC_CORES * SC_SUBCORES
SC_WINDOW = 32


def _slot_rows(n):
    return n + N_CLASSES * MOE_ROWS


def _plan(ids, n_slots):
    e_lo = ids[0].astype(jnp.int32)
    e_hi = ids[1].astype(jnp.int32)
    grp = e_lo // EXPERTS_PER_GROUP
    i = e_lo % EXPERTS_PER_GROUP
    j = e_hi % EXPERTS_PER_GROUP
    cls = grp * N_PAIRS + (i * (2 * EXPERTS_PER_GROUP - 1 - i)) // 2 + (j - i - 1)
    onehot = (cls[None, :] == jnp.arange(N_CLASSES, dtype=jnp.int32)[:, None]).astype(jnp.int32)
    csum = jnp.cumsum(onehot, axis=1)
    counts = csum[:, -1]
    padded = (counts + MOE_ROWS - 1) // MOE_ROWS * MOE_ROWS
    pend = jnp.cumsum(padded)
    pstart = pend - padded
    pos = jnp.sum(onehot * (csum - 1 + pstart[:, None]), axis=0).astype(jnp.int32)
    tile_start = jnp.arange(n_slots // MOE_ROWS, dtype=jnp.int32) * MOE_ROWS
    tcls = jnp.minimum(jnp.sum(tile_start[:, None] >= pend[None, :], axis=1), N_CLASSES - 1)
    tgrp, tpair = tcls // N_PAIRS, tcls % N_PAIRS
    t_lo = tgrp * EXPERTS_PER_GROUP + jnp.asarray(_PAIR_LO, jnp.int32)[tpair]
    t_hi = tgrp * EXPERTS_PER_GROUP + jnp.asarray(_PAIR_HI, jnp.int32)[tpair]
    tile_experts = jnp.stack([t_lo, t_hi]).astype(jnp.int32)
    n_used = (pend[-1] // MOE_ROWS).astype(jnp.int32).reshape(1)
    return pos, tile_experts, n_used


def _sc_mesh():
    return plsc.VectorSubcoreMesh(core_axis_name="core", subcore_axis_name="subcore")


def _sc_worker():
    return lax.axis_index("core") * SC_SUBCORES + lax.axis_index("subcore")


def _dispatch_call(h2, meta, pos, n_slots):
    n = h2.shape[0]
    per_worker = n // SC_WORKERS

    @pl.kernel(
        out_type=[jax.ShapeDtypeStruct((n_slots, h2.shape[1]), h2.dtype),
                  jax.ShapeDtypeStruct((n_slots, LANES), F32)],
        mesh=_sc_mesh(),
        scratch_types=[pltpu.VMEM((per_worker,), jnp.int32),
                       pltpu.VMEM((SC_WINDOW, h2.shape[1]), h2.dtype),
                       pltpu.VMEM((SC_WINDOW, LANES), F32)],
    )
    def dispatch(h2_hbm, meta_hbm, pos_hbm, xs_hbm, ms_hbm, slot_v, row_v, rec_v):
        wid = _sc_worker()
        base = wid * per_worker
        pltpu.sync_copy(pos_hbm.at[wid], slot_v)

        @pl.loop(0, per_worker // SC_WINDOW)
        def _(step):
            rows = pl.ds(base + step * SC_WINDOW, SC_WINDOW)
            slots = slot_v.at[pl.ds(step * SC_WINDOW, SC_WINDOW)]
            pltpu.sync_copy(h2_hbm.at[rows], row_v)
            pltpu.sync_copy(row_v, xs_hbm.at[slots])
            pltpu.sync_copy(meta_hbm.at[rows], rec_v)
            pltpu.sync_copy(rec_v, ms_hbm.at[slots])

    return dispatch(h2, meta, pos.reshape(SC_WORKERS, per_worker))


def _combine_call(ys, pos):
    n = pos.shape[0]
    per_worker = n // SC_WORKERS

    @pl.kernel(
        out_type=jax.ShapeDtypeStruct((n, ys.shape[1]), ys.dtype),
        mesh=_sc_mesh(),
        scratch_types=[pltpu.VMEM((per_worker,), jnp.int32),
                       pltpu.VMEM((SC_WINDOW, ys.shape[1]), ys.dtype)],
    )
    def combine(ys_hbm, pos_hbm, out_hbm, slot_v, row_v):
        wid = _sc_worker()
        base = wid * per_worker
        pltpu.sync_copy(pos_hbm.at[wid], slot_v)

        @pl.loop(0, per_worker // SC_WINDOW)
        def _(step):
            slots = slot_v.at[pl.ds(step * SC_WINDOW, SC_WINDOW)]
            pltpu.sync_copy(ys_hbm.at[slots], row_v)
            pltpu.sync_copy(row_v, out_hbm.at[pl.ds(base + step * SC_WINDOW, SC_WINDOW)])

    return combine(ys, pos.reshape(SC_WORKERS, per_worker))


def _last_used_step(nu_ref):
    return jnp.maximum((nu_ref[0] - 1) // MOE_STEP_TILES, 0)


def _experts_kernel(te_ref, nu_ref, xs_ref, ms_ref, wg_ref, wu_ref, wd_ref, ys_ref):
    step = pl.program_id(0)
    live = step <= _last_used_step(nu_ref)
    for sub in range(MOE_STEP_TILES):
        tile = step * MOE_STEP_TILES + sub
        rows = slice(sub * MOE_ROWS, (sub + 1) * MOE_ROWS)

        @pl.when(tile < nu_ref[0])
        def _(tile=tile, rows=rows):
            x = _unpack_bf16_pairs(xs_ref[rows, :])
            rec = ms_ref[rows, :]
            lane = lax.broadcasted_iota(jnp.int32, rec.shape, 1)

            def expert(e):
                gate = jnp.sum(jnp.where(lane == e, rec, 0.0), axis=-1, keepdims=True)
                hg = jnp.dot(x, wg_ref[e], preferred_element_type=F32)
                hu = jnp.dot(x, wu_ref[e], preferred_element_type=F32)
                hid = (_silu(hg) * hu * gate).astype(BF16)
                return jnp.dot(hid, wd_ref[e], preferred_element_type=F32)

            y = expert(te_ref[0, tile]) + expert(te_ref[1, tile])
            ys_ref[rows, :] = _pack_bf16_pairs(y)

        @pl.when((tile >= nu_ref[0]) & live)
        def _(rows=rows):
            ys_ref[rows, :] = jnp.zeros((MOE_ROWS, D_MODEL // 2), jnp.int32)


def _experts_call(xs, ms, tile_experts, n_used, layer, moe_w):
    n_slots = xs.shape[0]
    step_rows = MOE_ROWS * MOE_STEP_TILES
    rows = lambda i, te, nu: (jnp.minimum(i, _last_used_step(nu)), 0)

    def resident(w):
        return pl.BlockSpec((None,) + w.shape[1:], lambda i, te, nu: (layer, 0, 0, 0),
                            pipeline_mode=pl.Buffered(1))

    wg, wu, wd = moe_w
    return pl.pallas_call(
        _experts_kernel,
        grid_spec=pltpu.PrefetchScalarGridSpec(
            num_scalar_prefetch=2,
            grid=(n_slots // step_rows,),
            in_specs=[
                pl.BlockSpec((step_rows, D_MODEL // 2), rows),
                pl.BlockSpec((step_rows, LANES), rows),
                resident(wg), resident(wu), resident(wd),
            ],
            out_specs=pl.BlockSpec((step_rows, D_MODEL // 2), rows),
        ),
        out_shape=jax.ShapeDtypeStruct((n_slots, D_MODEL // 2), jnp.int32),
        compiler_params=_params("arbitrary"),
        name="experts",
    )(tile_experts, n_used, xs, ms, wg, wu, wd)


def _routed_moe(h2, meta, ids, layer, moe_w):
    b, t, _ = h2.shape
    n = b * t
    n_slots = _slot_rows(n)
    pos, tile_experts, n_used = _plan(ids, n_slots)
    xs, ms = _dispatch_call(h2.reshape(n, h2.shape[2]), meta.reshape(n, LANES), pos, n_slots)
    ys = _experts_call(xs, ms, tile_experts, n_used, layer, moe_w)
    return _combine_call(ys, pos).reshape(b, t, ys.shape[1])


def _final_kernel(x1_ref, moe_ref, mod_ref, fg_ref, o_ref):
    x2 = x1_ref[0] + mod_ref[0][5:6] * _unpack_bf16_pairs(moe_ref[0], F32)
    o_ref[0] = _rms(x2, fg_ref[...])


def _final_call(x1, moe, mod, final_g, tm):
    b, t, _ = x1.shape
    per_b = (lambda i, j: (i, 0, 0)) if mod.shape[0] > 1 else (lambda i, j: (0, 0, 0))
    tok = lambda i, j: (i, j, 0)
    return pl.pallas_call(
        _final_kernel,
        grid=(b, t // tm),
        in_specs=[
            pl.BlockSpec((1, tm, D_MODEL), tok),
            pl.BlockSpec((1, tm, D_MODEL // 2), tok),
            pl.BlockSpec((1, N_MOD, D_MODEL), per_b),
            pl.BlockSpec((1, D_MODEL), lambda i, j: (0, 0)),
        ],
        out_specs=pl.BlockSpec((1, tm, D_MODEL), tok),
        out_shape=jax.ShapeDtypeStruct((b, t, D_MODEL), F32),
        compiler_params=_params("parallel", "parallel"),
        name="final_norm",
    )(x1, moe, mod, final_g)


def _prep_layer(l, norm1_g, w_in, q_norm_g, w_uq, kv_norm_g, w_ukv, conv_w, conv_b, conv_ln_g,
                conv_ln_b, w_out, norm2_g):
    win = w_in[l]
    o_kv, o_kr, o_a, o_g = Q_LORA, Q_LORA + KV_LORA, Q_LORA + KV_LORA + QK_ROPE, Q_LORA + KV_LORA + QK_ROPE + C_CONV
    wkr = win[:, o_kr:o_a]
    pad_r = ((0, 0), (0, HEAD_BLOCK - QK_ROPE))
    win_cat = jnp.concatenate(
        [win[:, :o_kv], win[:, o_kv:o_kr], win[:, o_a:o_g], win[:, o_g:],
         jnp.pad(wkr, pad_r)], axis=1).astype(BF16)

    wuq = w_uq[l].reshape(Q_LORA, N_HEADS, QK_NOPE + QK_ROPE)
    nope, rope = wuq[..., :QK_NOPE], wuq[..., QK_NOPE:]
    zpad = jnp.zeros((Q_LORA, N_HEADS, HEAD_BLOCK - QK_NOPE - QK_ROPE), F32)
    wuq_p = jnp.concatenate([rope, nope, zpad], axis=-1).reshape(Q_LORA, -1).astype(BF16)

    wukv = w_ukv[l].reshape(KV_LORA, N_HEADS, QK_NOPE + V_HEAD)
    kz0 = jnp.zeros((KV_LORA, N_HEADS, QK_ROPE), F32)
    kz1 = jnp.zeros((KV_LORA, N_HEADS, HEAD_BLOCK - QK_NOPE - QK_ROPE), F32)
    wuk = jnp.concatenate([kz0, wukv[..., :QK_NOPE], kz1], axis=-1).reshape(KV_LORA, -1).astype(BF16)
    wuv = wukv[..., QK_NOPE:].reshape(KV_LORA, -1).astype(BF16)
    vz = jnp.zeros((KV_LORA, N_HEADS, HEAD_BLOCK - V_HEAD), F32)
    wuv_ones = jnp.concatenate([wukv[..., QK_NOPE:], vz], axis=-1).reshape(KV_LORA, -1).astype(BF16)

    return {
        "n1g": norm1_g[l][None], "win": win_cat, "qg": q_norm_g[l][None], "wuq": wuq_p,
        "kvg": kv_norm_g[l][None], "wuk": wuk, "wuv": wuv, "wuv_ones": wuv_ones,
        "conv_w": conv_w[l], "conv_b": conv_b[l][None], "ln_g": conv_ln_g[l][None],
        "ln_b": conv_ln_b[l][None], "wout": w_out[l].astype(BF16), "n2g": norm2_g[l][None],
    }


def _rope_tables(n_tokens):
    rows = n_tokens // GRID_W
    row = jnp.repeat(jnp.arange(rows), GRID_W).astype(F32)
    col = jnp.tile(jnp.arange(GRID_W), rows).astype(F32)
    freqs = ROPE_BASE ** (-jnp.arange(ROPE_PAIRS, dtype=F32) / ROPE_PAIRS)
    ar, ac = row[:, None] * freqs, col[:, None] * freqs
    cr, sr, cc, sc = jnp.cos(ar), jnp.sin(ar), jnp.cos(ac), jnp.sin(ac)
    rest = HEAD_BLOCK - QK_ROPE
    cos = jnp.concatenate([cr, cr, cc, cc, jnp.ones((n_tokens, rest), F32)], axis=1)
    sin = jnp.concatenate([-sr, sr, -sc, sc, jnp.zeros((n_tokens, rest), F32)], axis=1)
    return cos, sin


TM_PROMPT = 256
PROMPT_SEQS_PER_STEP = 2
TM_SAMPLE = 512
TQ_SAMPLE = 512


def _fold(a):
    return None if a is None else a.reshape(a.shape[0] // 2, a.shape[1] * 2, a.shape[2])


def _unfold(a):
    return a.reshape(a.shape[0] * 2, a.shape[1] // 2, a.shape[2])


def kernel(x_prompt, x_sample, cache_ckv, cache_krope, c, c_ctx, w_ada, b_ada, norm1_g, w_in,
           q_norm_g, w_uq, kv_norm_g, w_ukv, conv_w, conv_b, conv_ln_g, conv_ln_b, w_out,
           norm2_g, router_w, router_b, w_gate, w_up, w_down, final_g):
    dec_b = x_sample.shape[0]
    cvecs = jnp.concatenate(
        [c_ctx[None], c, jnp.zeros((MOD_ROWS - 1 - dec_b, D_MODEL), F32)], axis=0)
    mod_all = _ada_call(cvecs, w_ada, b_ada).reshape(DEPTH, MOD_ROWS, N_MOD, D_MODEL)

    moe_w = (w_gate.astype(BF16), w_up.astype(BF16), w_down.astype(BF16))
    rwt = router_w.T
    rb = router_b[:, None]
    fg = final_g[None]
    rope_tabs = _rope_tables(x_sample.shape[1])
    cache_kr_pad = jnp.pad(cache_krope, ((0, 0), (0, 0), (0, 0), (0, HEAD_BLOCK - QK_ROPE)))

    xp, xs = x_prompt, x_sample
    moe_p = moe_s = mod_p = mod_s = None
    ckvs, krs = [], []
    for l in range(DEPTH):
        lw = _prep_layer(l, norm1_g, w_in, q_norm_g, w_uq, kv_norm_g, w_ukv, conv_w, conv_b,
                         conv_ln_g, conv_ln_b, w_out, norm2_g)
        prev_p, prev_s = mod_p, mod_s
        mod_p = mod_all[l, 0:1]
        mod_s = mod_all[l, 1:1 + dec_b]

        outs = _premix_call(_fold(xp), mod_p, lw, TM_SAMPLE, moe=_fold(moe_p), mod_prev=prev_p)
        outs = [_unfold(o) for o in outs]
        if l > 0:
            q, k, v, z, xp, ckv, kr = outs
        else:
            q, k, v, z, ckv, kr = outs
        ckvs.append(ckv)
        krs.append(kr)
        attn = _attn_call(q, k, v, xp.shape[1])
        xp, h2, meta, ids = _postmix_call(xp, attn, z, mod_p, lw, rwt, rb, TM_PROMPT,
                                          nseq=PROMPT_SEQS_PER_STEP)
        moe_p = _routed_moe(h2, meta, ids, l, moe_w)

        outs = _premix_call(xs, mod_s, lw, TM_SAMPLE, rope_tabs, moe=moe_s, mod_prev=prev_s)
        if l > 0:
            q, k, v, z, xs = outs
        else:
            q, k, v, z = outs
        ctx_kv = _ctx_call(cache_ckv, cache_kr_pad, l, lw)
        attn = _attn_call(q, k, v, TQ_SAMPLE, ctx_kv)
        xs, h2, meta, ids = _postmix_call(xs, attn, z, mod_s, lw, rwt, rb, TM_SAMPLE)
        moe_s = _routed_moe(h2, meta, ids, l, moe_w)

    y_prompt = _unfold(_final_call(_fold(xp), _fold(moe_p), mod_p, fg, TM_SAMPLE))
    y_sample = _final_call(xs, moe_s, mod_s, fg, TM_SAMPLE)
    return y_prompt, y_sample, jnp.stack(ckvs, axis=1), jnp.stack(krs, axis=1)
```
